```python
import jax
import jax.numpy as jnp
from jax import lax
import numpy as np

D_MODEL = 1024
BATCH = 8
SEQ = 2048
DEPTH = 4

N_MIXERS = 2
N_GLA_LAYERS = (DEPTH + 1) // 2
N_SGU_LAYERS = DEPTH // 2

GLA_HEADS = 4
GLA_DK = D_MODEL // 2
GLA_DV = D_MODEL
GLA_HK = GLA_DK // GLA_HEADS
GLA_HV = GLA_DV // GLA_HEADS
GLA_GATE_RANK = 16
GLA_TAU = 16.0
GLA_CHUNK = 64
GLA_IN = 2 * GLA_DK + 2 * GLA_DV + GLA_GATE_RANK

SGU_WIDTH = 2 * D_MODEL
SGU_GROUPS = 8
SGU_GC = SGU_WIDTH // SGU_GROUPS
SGU_CHUNK = 128

MOE_GROUPS = 4
MOE_EXPERTS_PER_GROUP = 4
MOE_N_EXPERTS = MOE_GROUPS * MOE_EXPERTS_PER_GROUP
MOE_TOP_K = 2
MOE_FF = D_MODEL // 2

PLE_DIM = 256
NORM_EPS = 1e-6

kernel_name = 'hybrid_gla_sgu_hmoe_ple'


def rms_norm(x, g):
    xf = x.astype(jnp.float32)
    y = xf * lax.rsqrt(jnp.mean(xf * xf, axis=-1, keepdims=True) + NORM_EPS)
    return (y * g.astype(jnp.float32)).astype(x.dtype)


def layer_norm(x, g, b):
    xf = x.astype(jnp.float32)
    xc = xf - jnp.mean(xf, axis=-1, keepdims=True)
    y = xc * lax.rsqrt(jnp.mean(xc * xc, axis=-1, keepdims=True) + NORM_EPS)
    return (y * g.astype(jnp.float32) + b.astype(jnp.float32)).astype(x.dtype)


def gla_mixer(h, w_in, w_gate_up, b_gate, g_head, w_out):
    B, S, _ = h.shape
    nc = S // GLA_CHUNK
    z = h @ w_in
    q, k, v, r, a = jnp.split(
        z, [GLA_DK, 2 * GLA_DK, 2 * GLA_DK + GLA_DV, 2 * GLA_DK + 2 * GLA_DV], axis=-1)
    log_alpha = jax.nn.log_sigmoid((a @ w_gate_up + b_gate).astype(jnp.float32)) / GLA_TAU

    def heads(t, dh):
        return t.astype(jnp.float32).reshape(B, nc, GLA_CHUNK, GLA_HEADS, dh).transpose(0, 3, 1, 2, 4)

    q = heads(q, GLA_HK) * (GLA_HK ** -0.5)
    k = heads(k, GLA_HK)
    v = heads(v, GLA_HV)
    bcum = jnp.cumsum(heads(log_alpha, GLA_HK), axis=3)
    b_end = bcum[:, :, :, -1, :]
    q_dec = q * jnp.exp(bcum)
    k_dec = k * jnp.exp(-bcum)
    causal = jnp.tril(jnp.ones((GLA_CHUNK, GLA_CHUNK), dtype=bool))
    attn = jnp.where(causal, jnp.einsum('bhnik,bhnjk->bhnij', q_dec, k_dec), 0.0)
    o_intra = jnp.einsum('bhnij,bhnjv->bhniv', attn, v)
    k_end = k * jnp.exp(b_end[:, :, :, None, :] - bcum)
    chunk_kv = jnp.einsum('bhnjk,bhnjv->bhnkv', k_end, v)

    def step(state, inp):
        decay, kv = inp
        return decay[..., None] * state + kv, state

    s0 = jnp.zeros((B, GLA_HEADS, GLA_HK, GLA_HV), jnp.float32)
    _, states = lax.scan(step, s0, (jnp.moveaxis(jnp.exp(b_end), 2, 0), jnp.moveaxis(chunk_kv, 2, 0)))
    states = jnp.moveaxis(states, 0, 2)
    o = o_intra + jnp.einsum('bhnik,bhnkv->bhniv', q_dec, states)
    o = o.transpose(0, 2, 3, 1, 4).reshape(B, S, GLA_HEADS, GLA_HV)
    o = rms_norm(o, g_head.reshape(GLA_HEADS, GLA_HV)).reshape(B, S, GLA_DV)
    o = jax.nn.silu(r.astype(jnp.float32)) * o
    return o.astype(h.dtype) @ w_out


def sgu_mixer(h, w_in, ln_g, ln_b, w_s, b_s, w_out):
    B, S, _ = h.shape
    nc = S // SGU_CHUNK
    z = jax.nn.gelu(h @ w_in)
    u, v = jnp.split(z, 2, axis=-1)
    v = layer_norm(v, ln_g, ln_b).reshape(B, nc, SGU_CHUNK, SGU_GROUPS, SGU_GC)
    causal = jnp.tril(jnp.ones((SGU_CHUNK, SGU_CHUNK), dtype=bool))
    w_causal = jnp.where(causal, w_s, 0.0)
    s = jnp.einsum('gts,bnsgc->bntgc', w_causal, v) + jnp.transpose(b_s)[None, None, :, :, None]
    return (u * s.reshape(B, S, SGU_WIDTH)) @ w_out


def hier_moe(h, w_rg, b_rg, w_re, b_re, w_gate, w_up, w_down):
    f32 = jnp.float32
    group_logits = (h @ w_rg + b_rg).astype(f32)
    group_probs = jax.nn.softmax(group_logits, axis=-1)
    g_idx = jnp.argmax(group_logits, axis=-1)
    g_w = jnp.take_along_axis(group_probs, g_idx[..., None], axis=-1)
    expert_logits = (jnp.einsum('bsd,dge->bsge', h, w_re) + b_re).astype(f32)
    in_group = jnp.take_along_axis(expert_logits, g_idx[..., None, None], axis=2)[:, :, 0, :]
    top_vals, top_idx = lax.top_k(in_group, MOE_TOP_K)
    top_w = jax.nn.softmax(top_vals, axis=-1) * g_w
    expert_id = g_idx[..., None] * MOE_EXPERTS_PER_GROUP + top_idx
    combine = jnp.sum(jax.nn.one_hot(expert_id, MOE_N_EXPERTS, dtype=f32) * top_w[..., None],
                      axis=-2).astype(h.dtype)

    def per_row(args):
        hb, cb = args
        a = jnp.einsum('sd,edf->sef', hb, w_gate)
        b = jnp.einsum('sd,edf->sef', hb, w_up)
        act = jax.nn.silu(a) * b * cb[:, :, None]
        return jnp.einsum('sef,efd->sd', act, w_down)

    return lax.map(per_row, (h, combine))


def per_layer_embedding(hn, p_i, w_up, w_gate):
    return (p_i @ w_up) * jax.nn.sigmoid(hn @ w_gate)


def setup_inputs(seed: int = 0) -> dict:
    key = jax.random.key(seed)
    ks = jax.random.split(key, 26)

    def nrm(k, shape, scale):
        return jax.random.normal(k, shape, jnp.float32) * scale

    def gain(k, shape):
        return 1.0 + 0.02 * jax.random.normal(k, shape, jnp.float32)

    NA, NB, L, D = N_GLA_LAYERS, N_SGU_LAYERS, DEPTH, D_MODEL
    return {
        'x': nrm(ks[0], (BATCH, SEQ, D), 1.0),
        'p': nrm(ks[1], (DEPTH, BATCH, SEQ, PLE_DIM), 1.0),
        'gla_w_in': nrm(ks[2], (NA, D, GLA_IN), D ** -0.5),
        'gla_w_gate_up': nrm(ks[3], (NA, GLA_GATE_RANK, GLA_DK), GLA_GATE_RANK ** -0.5),
        'gla_b_gate': nrm(ks[4], (NA, GLA_DK), 0.1),
        'gla_g_head': gain(ks[5], (NA, GLA_DV)),
        'gla_w_out': nrm(ks[6], (NA, GLA_DV, D), GLA_DV ** -0.5),
        'sgu_w_in': nrm(ks[7], (NB, D, 2 * SGU_WIDTH), D ** -0.5),
        'sgu_ln_g': gain(ks[8], (NB, SGU_WIDTH)),
        'sgu_ln_b': nrm(ks[9], (NB, SGU_WIDTH), 0.02),
        'sgu_w_s': nrm(ks[10], (NB, SGU_GROUPS, SGU_CHUNK, SGU_CHUNK), 0.5 * SGU_CHUNK ** -0.5),
        'sgu_b_s': gain(ks[11], (NB, SGU_GROUPS, SGU_CHUNK)),
        'sgu_w_out': nrm(ks[12], (NB, SGU_WIDTH, D), SGU_WIDTH ** -0.5),
        'norm_mix_g': gain(ks[13], (L, D)),
        'norm_ffn_g': gain(ks[14], (L, D)),
        'norm_ple_g': gain(ks[15], (L, D)),
        'moe_w_route_group': nrm(ks[16], (L, D, MOE_GROUPS), D ** -0.5),
        'moe_b_route_group': nrm(ks[17], (L, MOE_GROUPS), 0.01),
        'moe_w_route_expert': nrm(ks[18], (L, D, MOE_GROUPS, MOE_EXPERTS_PER_GROUP), D ** -0.5),
        'moe_b_route_expert': nrm(ks[19], (L, MOE_GROUPS, MOE_EXPERTS_PER_GROUP), 0.01),
        'moe_w_gate': nrm(ks[20], (L, MOE_N_EXPERTS, D, MOE_FF), D ** -0.5),
        'moe_w_up': nrm(ks[21], (L, MOE_N_EXPERTS, D, MOE_FF), D ** -0.5),
        'moe_w_down': nrm(ks[22], (L, MOE_N_EXPERTS, MOE_FF, D), MOE_FF ** -0.5),
        'ple_w_up': nrm(ks[23], (L, PLE_DIM, D), PLE_DIM ** -0.5),
        'ple_w_gate': nrm(ks[24], (L, D, D), D ** -0.5),
        'final_norm_g': gain(ks[25], (D,)),
    }


def reference(x, p, gla_w_in, gla_w_gate_up, gla_b_gate, gla_g_head, gla_w_out,
              sgu_w_in, sgu_ln_g, sgu_ln_b, sgu_w_s, sgu_b_s, sgu_w_out,
              norm_mix_g, norm_ffn_g, norm_ple_g,
              moe_w_route_group, moe_b_route_group, moe_w_route_expert, moe_b_route_expert,
              moe_w_gate, moe_w_up, moe_w_down,
              ple_w_up, ple_w_gate, final_norm_g):
    h = x
    for i in range(DEPTH):
        j = i // N_MIXERS
        hn = rms_norm(h, norm_mix_g[i])
        if i % N_MIXERS == 0:
            h = h + gla_mixer(hn, gla_w_in[j], gla_w_gate_up[j], gla_b_gate[j],
                              gla_g_head[j], gla_w_out[j])
        else:
            h = h + sgu_mixer(hn, sgu_w_in[j], sgu_ln_g[j], sgu_ln_b[j],
                              sgu_w_s[j], sgu_b_s[j], sgu_w_out[j])
        h = h + hier_moe(rms_norm(h, norm_ffn_g[i]), moe_w_route_group[i], moe_b_route_group[i],
                         moe_w_route_expert[i], moe_b_route_expert[i],
                         moe_w_gate[i], moe_w_up[i], moe_w_down[i])
        h = h + per_layer_embedding(rms_norm(h, norm_ple_g[i]), p[i], ple_w_up[i], ple_w_gate[i])
    return rms_norm(h, final_norm_g)
```

```python
import functools

import jax
import jax.numpy as jnp
from jax import lax
from jax.experimental import pallas as pl
from jax.experimental.pallas import tpu as pltpu

F32 = jnp.float32
BF16 = jnp.bfloat16

D_MODEL = 1024
DEPTH = 4
GLA_HEADS = 4
GLA_DK = 512
GLA_DV = 1024
GLA_HK = GLA_DK // GLA_HEADS
GLA_HV = GLA_DV // GLA_HEADS
GLA_RANK = 16
GLA_TAU = 16.0
GLA_CHUNK = 64
SGU_WIDTH = 2048
SGU_GROUPS = 8
SGU_GC = SGU_WIDTH // SGU_GROUPS
SGU_CHUNK = 128
MOE_GROUPS = 4
MOE_EPG = 4
MOE_EXPERTS = MOE_GROUPS * MOE_EPG
MOE_FF = 512
PLE_DIM = 256
EPS = 1e-6

LANES = 128
GLA_A_PAD = LANES
GLA_ZW = 2 * GLA_DK + 2 * GLA_DV + GLA_A_PAD
ROUTE_ROWS = 32
PAIRS = ((0, 1), (0, 2), (1, 2), (1, 3), (2, 3), (0, 3))
N_PAIRS = len(PAIRS)
N_BUCKETS = MOE_GROUPS * N_PAIRS

T_GLA = 256
T_SGU = 256
T_ROUTE = 512
T_ROW = 256
T_MOE = 256
VMEM_LIMIT = 56 * 1024 * 1024


def _rms(x, g):
    ms = jnp.mean(x * x, axis=-1, keepdims=True)
    return x * lax.rsqrt(ms + EPS) * g


def _dot(a, b):
    return jnp.dot(a, b, preferred_element_type=F32)


def _dot_nt(a, b):
    return lax.dot_general(a, b, (((1,), (1,)), ((), ())), preferred_element_type=F32)


def _dot_tn(a, b):
    return lax.dot_general(a, b, (((0,), (0,)), ((), ())), preferred_element_type=F32)


def _params(sem):
    return pltpu.CompilerParams(dimension_semantics=sem, vmem_limit_bytes=VMEM_LIMIT)


def _gla_kernel(h_ref, gmix_ref, win_ref, wgu_ref, bg_ref, ghead_ref, wout_ref,
                o_ref, st_ref, oacc_ref, y_ref):
    @pl.when(pl.program_id(1) == 0)
    def _():
        st_ref[...] = jnp.zeros_like(st_ref)

    h = h_ref[...]
    hn = _rms(h, gmix_ref[...]).astype(BF16)
    z = _dot(hn, win_ref[...])
    a = z[:, 2 * GLA_DK + 2 * GLA_DV:].astype(BF16)
    pre = _dot(a, wgu_ref[...]) + bg_ref[...]
    la = (jnp.minimum(pre, 0.0) - jnp.log1p(jnp.exp(-jnp.abs(pre)))) * (1.0 / GLA_TAU)

    C = GLA_CHUNK
    ri = lax.broadcasted_iota(jnp.int32, (C, C), 0)
    ci = lax.broadcasted_iota(jnp.int32, (C, C), 1)
    causal = ri >= ci
    tri = causal.astype(BF16)

    for c in range(T_GLA // C):
        rows = slice(c * C, (c + 1) * C)
        la_c = la[rows]
        hi = la_c.astype(BF16)
        lo = (la_c - hi.astype(F32)).astype(BF16)
        bc = _dot(tri, hi) + _dot(tri, lo)
        bend = bc[C - 1:C, :]
        kc = z[rows, GLA_DK:2 * GLA_DK]
        qd = (z[rows, 0:GLA_DK] * (GLA_HK ** -0.5) * jnp.exp(bc)).astype(BF16)
        kd = (kc * jnp.exp(-bc)).astype(BF16)
        ke = (kc * jnp.exp(bend - bc)).astype(BF16)
        vc = z[rows, 2 * GLA_DK:2 * GLA_DK + GLA_DV].astype(BF16)
        dec = jnp.exp(bend)
        for hd in range(GLA_HEADS):
            ks = slice(hd * GLA_HK, (hd + 1) * GLA_HK)
            vs = slice(hd * GLA_HV, (hd + 1) * GLA_HV)
            att = jnp.where(causal, _dot_nt(qd[:, ks], kd[:, ks]), 0.0).astype(BF16)
            st = st_ref[hd]
            o = _dot(att, vc[:, vs]) + _dot_nt(qd[:, ks], st.astype(BF16))
            st_ref[hd] = st * dec[:, ks] + _dot_tn(vc[:, vs], ke[:, ks])
            oacc_ref[rows, vs] = o

    r0 = 2 * GLA_DK + GLA_DV
    for hd in range(GLA_HEADS):
        vs = slice(hd * GLA_HV, (hd + 1) * GLA_HV)
        on = _rms(oacc_ref[:, vs], ghead_ref[:, vs])
        r = z[:, r0 + hd * GLA_HV:r0 + (hd + 1) * GLA_HV]
        y_ref[:, vs] = (jax.nn.silu(r) * on).astype(BF16)
    o_ref[...] = h + _dot(y_ref[...], wout_ref[...])


def _gla_layer(h, batch, seq, j, i, gmix, win, wgu, bg, ghead, wout):
    n = h.shape[0]
    ns = seq // T_GLA
    const2 = lambda b, s: (0, 0)
    return pl.pallas_call(
        _gla_kernel,
        grid=(batch, ns),
        in_specs=[
            pl.BlockSpec((T_GLA, D_MODEL), lambda b, s: (b * ns + s, 0)),
            pl.BlockSpec((None, 1, D_MODEL), lambda b, s: (i, 0, 0)),
            pl.BlockSpec((None, D_MODEL, GLA_ZW), lambda b, s: (j, 0, 0)),
            pl.BlockSpec((None, GLA_A_PAD, GLA_DK), lambda b, s: (j, 0, 0)),
            pl.BlockSpec((None, 1, GLA_DK), lambda b, s: (j, 0, 0)),
            pl.BlockSpec((None, 1, GLA_DV), lambda b, s: (j, 0, 0)),
            pl.BlockSpec((None, GLA_DV, D_MODEL), lambda b, s: (j, 0, 0)),
        ],
        out_specs=pl.BlockSpec((T_GLA, D_MODEL), lambda b, s: (b * ns + s, 0)),
        out_shape=jax.ShapeDtypeStruct((n, D_MODEL), F32),
        scratch_shapes=[
            pltpu.VMEM((GLA_HEADS, GLA_HV, GLA_HK), F32),
            pltpu.VMEM((T_GLA, GLA_DV), F32),
            pltpu.VMEM((T_GLA, GLA_DV), BF16),
        ],
        compiler_params=_params(("arbitrary", "arbitrary")),
        name="gla_mixer",
    )(h, gmix, win, wgu, bg, ghead, wout)


def _sgu_kernel(h_ref, gmix_ref, win_ref, lng_ref, lnb_ref, ws_ref, bs_ref, wout_ref,
                o_ref, us_ref):
    h = h_ref[...]
    hn = _rms(h, gmix_ref[...]).astype(BF16)
    z = jax.nn.gelu(_dot(hn, win_ref[...]), approximate=True)
    u = z[:, :SGU_WIDTH]
    v = z[:, SGU_WIDTH:]
    xc = v - jnp.mean(v, axis=-1, keepdims=True)
    vn = xc * lax.rsqrt(jnp.mean(xc * xc, axis=-1, keepdims=True) + EPS)
    vn = (vn * lng_ref[...] + lnb_ref[...]).astype(BF16)

    C = SGU_CHUNK
    ri = lax.broadcasted_iota(jnp.int32, (C, C), 0)
    ci = lax.broadcasted_iota(jnp.int32, (C, C), 1)
    causal = ri >= ci
    for g in range(SGU_GROUPS):
        cols = slice(g * SGU_GC, (g + 1) * SGU_GC)
        wc = jnp.where(causal, ws_ref[g], 0.0).astype(BF16)
        for c in range(T_SGU // C):
            rows = slice(c * C, (c + 1) * C)
            s = _dot(wc, vn[rows, cols]) + bs_ref[:, cols]
            us_ref[rows, cols] = (u[rows, cols] * s).astype(BF16)
    o_ref[...] = h + _dot(us_ref[...], wout_ref[...])


def _sgu_layer(h, j, i, gmix, win, lng, lnb, ws, bs_full, wout):
    n = h.shape[0]
    return pl.pallas_call(
        _sgu_kernel,
        grid=(n // T_SGU,),
        in_specs=[
            pl.BlockSpec((T_SGU, D_MODEL), lambda t: (t, 0)),
            pl.BlockSpec((None, 1, D_MODEL), lambda t: (i, 0, 0)),
            pl.BlockSpec((None, D_MODEL, 2 * SGU_WIDTH), lambda t: (j, 0, 0)),
            pl.BlockSpec((None, 1, SGU_WIDTH), lambda t: (j, 0, 0)),
            pl.BlockSpec((None, 1, SGU_WIDTH), lambda t: (j, 0, 0)),
            pl.BlockSpec((None, SGU_GROUPS, SGU_CHUNK, SGU_CHUNK), lambda t: (j, 0, 0, 0)),
            pl.BlockSpec((None, SGU_CHUNK, SGU_WIDTH), lambda t: (j, 0, 0)),
            pl.BlockSpec((None, SGU_WIDTH, D_MODEL), lambda t: (j, 0, 0)),
        ],
        out_specs=pl.BlockSpec((T_SGU, D_MODEL), lambda t: (t, 0)),
        out_shape=jax.ShapeDtypeStruct((n, D_MODEL), F32),
        scratch_shapes=[pltpu.VMEM((T_SGU, SGU_WIDTH), BF16)],
        compiler_params=_params(("arbitrary",)),
        name="sgu_mixer",
    )(h, gmix, win, lng, lnb, ws, bs_full, wout)


def _first_argmax(rows):
    best = rows[0]
    idx = jnp.zeros(best.shape, jnp.int32)
    for j in range(1, len(rows)):
        upd = rows[j] > best
        idx = jnp.where(upd, j, idx)
        best = jnp.where(upd, rows[j], best)
    return idx, best


def _router_kernel(h_ref, gffn_ref, wrt_ref, brt_ref, triu_ref, route_ref, counts_ref, cnt_ref):
    @pl.when(pl.program_id(0) == 0)
    def _():
        cnt_ref[...] = jnp.zeros_like(cnt_ref)

    hb = _rms(h_ref[...], gffn_ref[...]).astype(BF16)
    lg = _dot_nt(wrt_ref[...], hb) + brt_ref[...]
    row = lambda j: lg[j:j + 1, :]
    g, _ = _first_argmax([row(j) for j in range(MOE_GROUPS)])
    ex = []
    for e in range(MOE_EPG):
        v = row(MOE_GROUPS + e)
        for gg in range(1, MOE_GROUPS):
            v = jnp.where(g == gg, row(MOE_GROUPS + gg * MOE_EPG + e), v)
        ex.append(v)
    i1, _ = _first_argmax(ex)
    neg = jnp.full(ex[0].shape, -jnp.inf, F32)
    i2, _ = _first_argmax([jnp.where(i1 == e, neg, ex[e]) for e in range(MOE_EPG)])
    lo = jnp.minimum(i1, i2)
    hi = jnp.maximum(i1, i2)
    pair = jnp.zeros(lo.shape, jnp.int32)
    for p, (a, b) in enumerate(PAIRS):
        pair = jnp.where((lo == a) & (hi == b), p, pair)
    bucket = g * N_PAIRS + pair

    t = bucket.shape[1]
    onehot = (lax.broadcasted_iota(jnp.int32, (ROUTE_ROWS, t), 0) == bucket).astype(F32)
    before = _dot(onehot.astype(BF16), triu_ref[...])
    rank = jnp.sum(onehot * (before + cnt_ref[:, 0:1]), axis=0, keepdims=True).astype(jnp.int32)
    cnt_ref[...] = cnt_ref[...] + jnp.sum(onehot, axis=1, keepdims=True)
    counts_ref[...] = cnt_ref[...]
    r8 = lax.broadcasted_iota(jnp.int32, (8, t), 0)
    route_ref[...] = jnp.where(r8 == 0, bucket, jnp.where(r8 == 1, rank, 0))


def _router(h, i, gffn, wrt, brt, triu):
    n = h.shape[0]
    return pl.pallas_call(
        _router_kernel,
        grid=(n // T_ROUTE,),
        in_specs=[
            pl.BlockSpec((T_ROUTE, D_MODEL), lambda t: (t, 0)),
            pl.BlockSpec((None, 1, D_MODEL), lambda t: (i, 0, 0)),
            pl.BlockSpec((None, ROUTE_ROWS, D_MODEL), lambda t: (i, 0, 0)),
            pl.BlockSpec((None, ROUTE_ROWS, T_ROUTE), lambda t: (i, 0, 0)),
            pl.BlockSpec((T_ROUTE, T_ROUTE), lambda t: (0, 0)),
        ],
        out_specs=[
            pl.BlockSpec((8, T_ROUTE), lambda t: (0, t)),
            pl.BlockSpec((ROUTE_ROWS, LANES), lambda t: (0, 0)),
        ],
        out_shape=[
            jax.ShapeDtypeStruct((8, n), jnp.int32),
            jax.ShapeDtypeStruct((ROUTE_ROWS, LANES), F32),
        ],
        scratch_shapes=[pltpu.VMEM((ROUTE_ROWS, LANES), F32)],
        compiler_params=_params(("arbitrary",)),
        name="moe_router",
    )(h, gffn, wrt, brt, triu)


def _dispatch_kernel(dest_ref, h_ref, xs_in_ref, xs_ref, sem):
    del xs_in_ref

    def issue(r, carry):
        d = dest_ref[0, 0, r]
        pltpu.make_async_copy(h_ref.at[pl.ds(r, 1), :], xs_ref.at[pl.ds(d, 1), :], sem).start()
        return carry

    lax.fori_loop(0, T_ROW, issue, 0)
    pltpu.make_async_copy(h_ref, xs_ref.at[pl.ds(0, T_ROW), :], sem).wait()


def _dispatch(h, dest3, xs_init):
    n = h.shape[0]
    return pl.pallas_call(
        _dispatch_kernel,
        grid=(n // T_ROW,),
        in_specs=[
            pl.BlockSpec((1, 1, T_ROW), lambda t: (t, 0, 0), memory_space=pltpu.SMEM),
            pl.BlockSpec((T_ROW, D_MODEL), lambda t: (t, 0)),
            pl.BlockSpec(memory_space=pl.ANY),
        ],
        out_specs=pl.BlockSpec(memory_space=pl.ANY),
        out_shape=jax.ShapeDtypeStruct(xs_init.shape, F32),
        scratch_shapes=[pltpu.SemaphoreType.DMA(())],
        input_output_aliases={2: 0},
        compiler_params=_params(("arbitrary",)),
        name="moe_dispatch",
    )(dest3, h, xs_init)


def _experts_kernel(tidx_ref, tea_ref, teb_ref, tg_ref, nv_ref,
                    xs_ref, gffn_ref, wr_ref, br_ref, wgua_ref, wgub_ref, wda_ref, wdb_ref,
                    ys_ref):
    del tidx_ref
    i = pl.program_id(0)

    @pl.when(i >= nv_ref[0])
    def _():
        ys_ref[...] = jnp.zeros_like(ys_ref)

    @pl.when(i < nv_ref[0])
    def _():
        x = _rms(xs_ref[...], gffn_ref[...]).astype(BF16)
        lg = _dot(x, wr_ref[...]) + br_ref[...]
        lane = lax.broadcasted_iota(jnp.int32, lg.shape, 1)
        glog = jnp.where(lane < MOE_GROUPS, lg, -jnp.inf)
        pe = jnp.exp(glog - jnp.max(glog, axis=-1, keepdims=True))
        g_w = (jnp.sum(jnp.where(lane == tg_ref[i], pe, 0.0), axis=-1, keepdims=True)
               / jnp.sum(pe, axis=-1, keepdims=True))
        la = jnp.sum(jnp.where(lane == MOE_GROUPS + tea_ref[i], lg, 0.0), axis=-1, keepdims=True)
        lb = jnp.sum(jnp.where(lane == MOE_GROUPS + teb_ref[i], lg, 0.0), axis=-1, keepdims=True)
        m = jnp.maximum(la, lb)
        pa = jnp.exp(la - m)
        pb = jnp.exp(lb - m)
        ca = pa / (pa + pb) * g_w
        cb = pb / (pa + pb) * g_w

        def expert(wgu_ref, c):
            hh = _dot(x, wgu_ref[...])
            return (jax.nn.silu(hh[:, :MOE_FF]) * hh[:, MOE_FF:] * c).astype(BF16)

        ys_ref[...] = (_dot(expert(wgua_ref, ca), wda_ref[...])
                       + _dot(expert(wgub_ref, cb), wdb_ref[...]))


def _experts(xs, i, tidx, tea, teb, tg, nv, gffn, wr, br, wgu, wd):
    np_rows = xs.shape[0]
    e0 = i * MOE_EXPERTS
    grid_spec = pltpu.PrefetchScalarGridSpec(
        num_scalar_prefetch=5,
        grid=(np_rows // T_MOE,),
        in_specs=[
            pl.BlockSpec((T_MOE, D_MODEL), lambda t, ti, ea, eb, g, nv: (ti[t], 0)),
            pl.BlockSpec((None, 1, D_MODEL), lambda t, *_: (i, 0, 0)),
            pl.BlockSpec((None, D_MODEL, LANES), lambda t, *_: (i, 0, 0)),
            pl.BlockSpec((None, 1, LANES), lambda t, *_: (i, 0, 0)),
            pl.BlockSpec((None, D_MODEL, 2 * MOE_FF), lambda t, ti, ea, eb, g, nv: (e0 + ea[t], 0, 0)),
            pl.BlockSpec((None, D_MODEL, 2 * MOE_FF), lambda t, ti, ea, eb, g, nv: (e0 + eb[t], 0, 0)),
            pl.BlockSpec((None, MOE_FF, D_MODEL), lambda t, ti, ea, eb, g, nv: (e0 + ea[t], 0, 0)),
            pl.BlockSpec((None, MOE_FF, D_MODEL), lambda t, ti, ea, eb, g, nv: (e0 + eb[t], 0, 0)),
        ],
        out_specs=pl.BlockSpec((T_MOE, D_MODEL), lambda t, *_: (t, 0)),
    )
    return pl.pallas_call(
        _experts_kernel,
        grid_spec=grid_spec,
        out_shape=jax.ShapeDtypeStruct((np_rows, D_MODEL), F32),
        compiler_params=_params(("arbitrary",)),
        name="moe_experts",
    )(tidx, tea, teb, tg, nv, xs, gffn, wr, br, wgu, wgu, wd, wd)


def _combine_kernel(dcur_ref, dnext_ref, h_ref, p_ref, ys_ref, gple_ref, wup_ref, wgate_ref,
                    gfin_ref, o_ref, ybuf_ref, sem, *, final):
    i = pl.program_id(0)
    n = pl.num_programs(0)

    def issue(dref, slot):
        def body(r, carry):
            d = dref[0, 0, r]
            pltpu.make_async_copy(ys_ref.at[pl.ds(d, 1), :],
                                  ybuf_ref.at[slot, pl.ds(r, 1), :], sem.at[slot]).start()
            return carry
        lax.fori_loop(0, T_ROW, body, 0)

    @pl.when(i == 0)
    def _():
        issue(dcur_ref, 0)

    @pl.when(i + 1 < n)
    def _():
        issue(dnext_ref, (i + 1) % 2)

    slot = i % 2
    pltpu.make_async_copy(ys_ref.at[pl.ds(0, T_ROW), :], ybuf_ref.at[slot], sem.at[slot]).wait()
    h2 = h_ref[...] + ybuf_ref[slot]
    pn = _rms(h2, gple_ref[...]).astype(BF16)
    gate = jax.nn.sigmoid(_dot(pn, wgate_ref[...]))
    up = _dot(p_ref[...].astype(BF16), wup_ref[...])
    out = h2 + up * gate
    if final:
        out = _rms(out, gfin_ref[...])
    o_ref[...] = out


def _combine(h, dest3, p2, ys, i, gple, wup, wgate, gfin):
    n = h.shape[0]
    nt = n // T_ROW
    return pl.pallas_call(
        functools.partial(_combine_kernel, final=(i == DEPTH - 1)),
        grid=(nt,),
        in_specs=[
            pl.BlockSpec((1, 1, T_ROW), lambda t: (t, 0, 0), memory_space=pltpu.SMEM),
            pl.BlockSpec((1, 1, T_ROW), lambda t: (jnp.minimum(t + 1, nt - 1), 0, 0),
                         memory_space=pltpu.SMEM),
            pl.BlockSpec((T_ROW, D_MODEL), lambda t: (t, 0)),
            pl.BlockSpec((T_ROW, PLE_DIM), lambda t: (i * nt + t, 0)),
            pl.BlockSpec(memory_space=pl.ANY),
            pl.BlockSpec((None, 1, D_MODEL), lambda t: (i, 0, 0)),
            pl.BlockSpec((None, PLE_DIM, D_MODEL), lambda t: (i, 0, 0)),
            pl.BlockSpec((None, D_MODEL, D_MODEL), lambda t: (i, 0, 0)),
            pl.BlockSpec((1, D_MODEL), lambda t: (0, 0)),
        ],
        out_specs=pl.BlockSpec((T_ROW, D_MODEL), lambda t: (t, 0)),
        out_shape=jax.ShapeDtypeStruct((n, D_MODEL), F32),
        scratch_shapes=[
            pltpu.VMEM((2, T_ROW, D_MODEL), F32),
            pltpu.SemaphoreType.DMA((2,)),
        ],
        compiler_params=_params(("arbitrary",)),
        name="moe_combine_ple",
    )(dest3, dest3, h, p2, ys, gple, wup, wgate, gfin)


def _plan(route, counts, n_tiles):
    bucket = route[0]
    rank = route[1]
    cnt = counts[:N_BUCKETS, 0].astype(jnp.int32)
    padded = ((cnt + T_MOE - 1) // T_MOE) * T_MOE
    ends = jnp.cumsum(padded)
    starts = ends - padded
    dest = starts[bucket] + rank
    nv = ends[-1] // T_MOE
    tidx = jnp.minimum(jnp.arange(n_tiles, dtype=jnp.int32), nv - 1)
    tb = jnp.sum((tidx * T_MOE)[:, None] >= ends[None, :], axis=1).astype(jnp.int32)
    tg = tb // N_PAIRS
    pa = jnp.asarray([a for a, _ in PAIRS], jnp.int32)[tb % N_PAIRS]
    pb = jnp.asarray([b for _, b in PAIRS], jnp.int32)[tb % N_PAIRS]
    return dest, tidx, tg * MOE_EPG + pa, tg * MOE_EPG + pb, tg, nv.reshape(1).astype(jnp.int32)


def kernel(x, p, gla_w_in, gla_w_gate_up, gla_b_gate, gla_g_head, gla_w_out, sgu_w_in, sgu_ln_g,
           sgu_ln_b, sgu_w_s, sgu_b_s, sgu_w_out, norm_mix_g, norm_ffn_g, norm_ple_g,
           moe_w_route_group, moe_b_route_group, moe_w_route_expert, moe_b_route_expert,
           moe_w_gate, moe_w_up, moe_w_down, ple_w_up, ple_w_gate, final_norm_g):
    batch, seq, d = x.shape
    assert d == D_MODEL and seq % T_GLA == 0
    n = batch * seq
    assert n % T_ROUTE == 0 and n % T_ROW == 0 and n % T_SGU == 0
    depth = p.shape[0]
    assert depth == DEPTH
    na, nb = gla_w_in.shape[0], sgu_w_in.shape[0]

    qkvr = 2 * GLA_DK + 2 * GLA_DV
    gla_win = jnp.concatenate(
        [gla_w_in[:, :, :qkvr], gla_w_in[:, :, qkvr:],
         jnp.zeros((na, D_MODEL, GLA_A_PAD - GLA_RANK), F32)], axis=-1).astype(BF16)
    gla_wgu = jnp.concatenate(
        [gla_w_gate_up, jnp.zeros((na, GLA_A_PAD - GLA_RANK, GLA_DK), F32)], axis=1).astype(BF16)
    gla_bg = gla_b_gate.reshape(na, 1, GLA_DK)
    gla_gh = gla_g_head.reshape(na, 1, GLA_DV)
    gla_wout = gla_w_out.astype(BF16)
    sgu_win = sgu_w_in.astype(BF16)
    sgu_lng = sgu_ln_g.reshape(nb, 1, SGU_WIDTH)
    sgu_lnb = sgu_ln_b.reshape(nb, 1, SGU_WIDTH)
    sgu_bs = jnp.repeat(jnp.transpose(sgu_b_s, (0, 2, 1)), SGU_GC, axis=-1)
    sgu_wout = sgu_w_out.astype(BF16)
    g_mix = norm_mix_g.reshape(depth, 1, D_MODEL)
    g_ffn = norm_ffn_g.reshape(depth, 1, D_MODEL)
    g_ple = norm_ple_g.reshape(depth, 1, D_MODEL)
    g_fin = final_norm_g.reshape(1, D_MODEL)
    w_route = jnp.concatenate(
        [moe_w_route_group, moe_w_route_expert.reshape(depth, D_MODEL, MOE_EXPERTS)], axis=-1)
    b_route = jnp.concatenate(
        [moe_b_route_group, moe_b_route_expert.reshape(depth, MOE_EXPERTS)], axis=-1)
    n_logits = MOE_GROUPS + MOE_EXPERTS
    wr = jnp.pad(w_route, ((0, 0), (0, 0), (0, LANES - n_logits))).astype(BF16)
    br = jnp.pad(b_route, ((0, 0), (0, LANES - n_logits))).reshape(depth, 1, LANES)
    wrt = jnp.pad(jnp.transpose(w_route, (0, 2, 1)),
                  ((0, 0), (0, ROUTE_ROWS - n_logits), (0, 0))).astype(BF16)
    brt = jnp.broadcast_to(
        jnp.pad(b_route, ((0, 0), (0, ROUTE_ROWS - n_logits)))[:, :, None],
        (depth, ROUTE_ROWS, T_ROUTE))
    triu = jnp.triu(jnp.ones((T_ROUTE, T_ROUTE), BF16), k=1)
    wgu = jnp.concatenate([moe_w_gate, moe_w_up], axis=-1).astype(BF16)
    wgu = wgu.reshape(depth * MOE_EXPERTS, D_MODEL, 2 * MOE_FF)
    wd = moe_w_down.astype(BF16).reshape(depth * MOE_EXPERTS, MOE_FF, D_MODEL)
    ple_wup = ple_w_up.astype(BF16)
    ple_wgate = ple_w_gate.astype(BF16)
    p2 = p.reshape(depth * n, PLE_DIM)

    n_tiles = n // T_MOE + N_BUCKETS
    h = x.reshape(n, D_MODEL)
    for i in range(depth):
        j = i // 2
        if i % 2 == 0:
            h = _gla_layer(h, batch, seq, j, i, g_mix, gla_win, gla_wgu, gla_bg, gla_gh, gla_wout)
        else:
            h = _sgu_layer(h, j, i, g_mix, sgu_win, sgu_lng, sgu_lnb, sgu_w_s, sgu_bs, sgu_wout)
        route, counts = _router(h, i, g_ffn, wrt, brt, triu)
        dest, tidx, tea, teb, tg, nv = _plan(route, counts, n_tiles)
        dest3 = dest.reshape(n // T_ROW, 1, T_ROW)
        xs = _dispatch(h, dest3, jnp.zeros((n_tiles * T_MOE, D_MODEL), F32))
        ys = _experts(xs, i, tidx, tea, teb, tg, nv, g_ffn, wr, br, wgu, wd)
        h = _combine(h, dest3, p2, ys, i, g_ple, ple_wup, ple_wgate, g_fin)
    return h.reshape(batch, seq, D_MODEL)
```

```python
import functools

import jax
import jax.numpy as jnp
from jax import lax
from jax.experimental import pallas as pl
from jax.experimental.pallas import tpu as pltpu

F32 = jnp.float32
BF16 = jnp.bfloat16

D_MODEL = 1024
DEPTH = 4
GLA_HEADS = 4
GLA_DK = 512
GLA_DV = 1024
GLA_HK = GLA_DK // GLA_HEADS
GLA_HV = GLA_DV // GLA_HEADS
GLA_RANK = 16
GLA_TAU = 16.0
GLA_CHUNK = 64
SGU_WIDTH = 2048
SGU_GROUPS = 8
SGU_GC = SGU_WIDTH // SGU_GROUPS
SGU_CHUNK = 128
MOE_GROUPS = 4
MOE_EPG = 4
MOE_EXPERTS = MOE_GROUPS * MOE_EPG
MOE_FF = 512
PLE_DIM = 256
EPS = 1e-6

LANES = 128
SUBLANES = 8
ROW_TILES = D_MODEL // LANES
assert ROW_TILES == SUBLANES
GLA_A_PAD = LANES
GLA_ZW = 2 * GLA_DK + 2 * GLA_DV + GLA_A_PAD
ROUTE_ROWS = 32
PAIRS = ((0, 1), (0, 2), (1, 2), (1, 3), (2, 3), (0, 3))
N_PAIRS = len(PAIRS)
N_BUCKETS = MOE_GROUPS * N_PAIRS

T_GLA = 256
T_SGU = 256
T_ROUTE = 512
T_ROW = 256
T_MOE = 256
DMA_UNROLL = 8
CAST_ROWS = 256
VMEM_LIMIT = 56 * 1024 * 1024


def _rms(x, g):
    ms = jnp.mean(x * x, axis=-1, keepdims=True)
    return x * lax.rsqrt(ms + EPS) * g


def _dot(a, b):
    return jnp.dot(a, b, preferred_element_type=F32)


def _dot_nt(a, b):
    return lax.dot_general(a, b, (((1,), (1,)), ((), ())), preferred_element_type=F32)


def _dot_tn(a, b):
    return lax.dot_general(a, b, (((0,), (0,)), ((), ())), preferred_element_type=F32)


def _params(sem):
    return pltpu.CompilerParams(dimension_semantics=sem, vmem_limit_bytes=VMEM_LIMIT)


def _resident(block_shape, index_map):
    return pl.BlockSpec(block_shape, index_map, pipeline_mode=pl.Buffered(1))


def _cast_weight(dst_ref, src_ref):
    rows = src_ref.shape[0]
    for r in range(0, rows, CAST_ROWS):
        dst_ref[r:r + CAST_ROWS, :] = src_ref[r:r + CAST_ROWS, :].astype(BF16)


def _rows_to_tiles(dst_ref, x):
    for c in range(ROW_TILES):
        dst_ref[:, c, :] = x[:, c * LANES:(c + 1) * LANES]


def _tiles_to_rows(src_ref):
    return jnp.concatenate([src_ref[:, c, :] for c in range(ROW_TILES)], axis=-1)


def _gla_kernel(h_ref, gmix_ref, win_ref, wgu_ref, bg_ref, ghead_ref, wout_ref,
                o_ref, winb_ref, woutb_ref, st_ref, oacc_ref, y_ref):
    @pl.when((pl.program_id(0) == 0) & (pl.program_id(1) == 0))
    def _():
        _cast_weight(winb_ref, win_ref)
        _cast_weight(woutb_ref, wout_ref)

    @pl.when(pl.program_id(1) == 0)
    def _():
        st_ref[...] = jnp.zeros_like(st_ref)

    h = h_ref[...]
    hn = _rms(h, gmix_ref[...]).astype(BF16)
    z = _dot(hn, winb_ref[...])
    a = z[:, 2 * GLA_DK + 2 * GLA_DV:].astype(BF16)
    pre = _dot(a, wgu_ref[...].astype(BF16)) + bg_ref[...]
    la = (jnp.minimum(pre, 0.0) - jnp.log1p(jnp.exp(-jnp.abs(pre)))) * (1.0 / GLA_TAU)

    C = GLA_CHUNK
    ri = lax.broadcasted_iota(jnp.int32, (C, C), 0)
    ci = lax.broadcasted_iota(jnp.int32, (C, C), 1)
    causal = ri >= ci
    tri = causal.astype(BF16)

    for c in range(T_GLA // C):
        rows = slice(c * C, (c + 1) * C)
        la_c = la[rows]
        hi = la_c.astype(BF16)
        lo = (la_c - hi.astype(F32)).astype(BF16)
        bc = _dot(tri, hi) + _dot(tri, lo)
        bend = bc[C - 1:C, :]
        kc = z[rows, GLA_DK:2 * GLA_DK]
        qd = (z[rows, 0:GLA_DK] * (GLA_HK ** -0.5) * jnp.exp(bc)).astype(BF16)
        kd = (kc * jnp.exp(-bc)).astype(BF16)
        ke = (kc * jnp.exp(bend - bc)).astype(BF16)
        vc = z[rows, 2 * GLA_DK:2 * GLA_DK + GLA_DV].astype(BF16)
        dec = jnp.exp(bend)
        for hd in range(GLA_HEADS):
            ks = slice(hd * GLA_HK, (hd + 1) * GLA_HK)
            vs = slice(hd * GLA_HV, (hd + 1) * GLA_HV)
            att = jnp.where(causal, _dot_nt(qd[:, ks], kd[:, ks]), 0.0).astype(BF16)
            st = st_ref[hd]
            o = _dot(att, vc[:, vs]) + _dot_nt(qd[:, ks], st.astype(BF16))
            st_ref[hd] = st * dec[:, ks] + _dot_tn(vc[:, vs], ke[:, ks])
            oacc_ref[rows, vs] = o

    r0 = 2 * GLA_DK + GLA_DV
    for hd in range(GLA_HEADS):
        vs = slice(hd * GLA_HV, (hd + 1) * GLA_HV)
        on = _rms(oacc_ref[:, vs], ghead_ref[:, vs])
        r = z[:, r0 + hd * GLA_HV:r0 + (hd + 1) * GLA_HV]
        y_ref[:, vs] = (jax.nn.silu(r) * on).astype(BF16)
    o_ref[...] = h + _dot(y_ref[...], woutb_ref[...])


def _gla_layer(h, batch, seq, j, i, gmix, win, wgu, bg, ghead, wout):
    n = h.shape[0]
    ns = seq // T_GLA
    return pl.pallas_call(
        _gla_kernel,
        grid=(batch, ns),
        in_specs=[
            pl.BlockSpec((T_GLA, D_MODEL), lambda b, s: (b * ns + s, 0)),
            pl.BlockSpec((None, 1, D_MODEL), lambda b, s: (i, 0, 0)),
            _resident((None, D_MODEL, GLA_ZW), lambda b, s: (j, 0, 0)),
            pl.BlockSpec((None, GLA_A_PAD, GLA_DK), lambda b, s: (j, 0, 0)),
            pl.BlockSpec((None, 1, GLA_DK), lambda b, s: (j, 0, 0)),
            pl.BlockSpec((None, 1, GLA_DV), lambda b, s: (j, 0, 0)),
            _resident((None, GLA_DV, D_MODEL), lambda b, s: (j, 0, 0)),
        ],
        out_specs=pl.BlockSpec((T_GLA, D_MODEL), lambda b, s: (b * ns + s, 0)),
        out_shape=jax.ShapeDtypeStruct((n, D_MODEL), F32),
        scratch_shapes=[
            pltpu.VMEM((D_MODEL, GLA_ZW), BF16),
            pltpu.VMEM((GLA_DV, D_MODEL), BF16),
            pltpu.VMEM((GLA_HEADS, GLA_HV, GLA_HK), F32),
            pltpu.VMEM((T_GLA, GLA_DV), F32),
            pltpu.VMEM((T_GLA, GLA_DV), BF16),
        ],
        compiler_params=_params(("arbitrary", "arbitrary")),
        name="gla_mixer",
    )(h, gmix, win, wgu, bg, ghead, wout)


def _sgu_kernel(h_ref, gmix_ref, win_ref, lng_ref, lnb_ref, ws_ref, bs_ref, wout_ref,
                o_ref, winb_ref, woutb_ref, wsb_ref, us_ref):
    C = SGU_CHUNK

    @pl.when(pl.program_id(0) == 0)
    def _():
        _cast_weight(winb_ref, win_ref)
        _cast_weight(woutb_ref, wout_ref)
        ri = lax.broadcasted_iota(jnp.int32, (C, C), 0)
        ci = lax.broadcasted_iota(jnp.int32, (C, C), 1)
        for g in range(SGU_GROUPS):
            wsb_ref[g] = jnp.where(ri >= ci, ws_ref[g], 0.0).astype(BF16)

    h = h_ref[...]
    hn = _rms(h, gmix_ref[...]).astype(BF16)
    z = jax.nn.gelu(_dot(hn, winb_ref[...]), approximate=True)
    u = z[:, :SGU_WIDTH]
    v = z[:, SGU_WIDTH:]
    xc = v - jnp.mean(v, axis=-1, keepdims=True)
    vn = xc * lax.rsqrt(jnp.mean(xc * xc, axis=-1, keepdims=True) + EPS)
    vn = (vn * lng_ref[...] + lnb_ref[...]).astype(BF16)

    for g in range(SGU_GROUPS):
        cols = slice(g * SGU_GC, (g + 1) * SGU_GC)
        wc = wsb_ref[g]
        for c in range(T_SGU // C):
            rows = slice(c * C, (c + 1) * C)
            s = _dot(wc, vn[rows, cols]) + bs_ref[:, cols]
            us_ref[rows, cols] = (u[rows, cols] * s).astype(BF16)
    o_ref[...] = h + _dot(us_ref[...], woutb_ref[...])


def _sgu_layer(h, j, i, gmix, win, lng, lnb, ws, bs_full, wout):
    n = h.shape[0]
    return pl.pallas_call(
        _sgu_kernel,
        grid=(n // T_SGU,),
        in_specs=[
            pl.BlockSpec((T_SGU, D_MODEL), lambda t: (t, 0)),
            pl.BlockSpec((None, 1, D_MODEL), lambda t: (i, 0, 0)),
            _resident((None, D_MODEL, 2 * SGU_WIDTH), lambda t: (j, 0, 0)),
            pl.BlockSpec((None, 1, SGU_WIDTH), lambda t: (j, 0, 0)),
            pl.BlockSpec((None, 1, SGU_WIDTH), lambda t: (j, 0, 0)),
            _resident((None, SGU_GROUPS, SGU_CHUNK, SGU_CHUNK), lambda t: (j, 0, 0, 0)),
            _resident((None, SGU_CHUNK, SGU_WIDTH), lambda t: (j, 0, 0)),
            _resident((None, SGU_WIDTH, D_MODEL), lambda t: (j, 0, 0)),
        ],
        out_specs=pl.BlockSpec((T_SGU, D_MODEL), lambda t: (t, 0)),
        out_shape=jax.ShapeDtypeStruct((n, D_MODEL), F32),
        scratch_shapes=[
            pltpu.VMEM((D_MODEL, 2 * SGU_WIDTH), BF16),
            pltpu.VMEM((SGU_WIDTH, D_MODEL), BF16),
            pltpu.VMEM((SGU_GROUPS, SGU_CHUNK, SGU_CHUNK), BF16),
            pltpu.VMEM((T_SGU, SGU_WIDTH), BF16),
        ],
        compiler_params=_params(("arbitrary",)),
        name="sgu_mixer",
    )(h, gmix, win, lng, lnb, ws, bs_full, wout)


def _first_argmax(rows):
    best = rows[0]
    idx = jnp.zeros(best.shape, jnp.int32)
    for j in range(1, len(rows)):
        upd = rows[j] > best
        idx = jnp.where(upd, j, idx)
        best = jnp.where(upd, rows[j], best)
    return idx, best


def _router_kernel(h_ref, gffn_ref, wrt_ref, brt_ref, triu_ref, route_ref, counts_ref, cnt_ref):
    @pl.when(pl.program_id(0) == 0)
    def _():
        cnt_ref[...] = jnp.zeros_like(cnt_ref)

    hb = _rms(h_ref[...], gffn_ref[...]).astype(BF16)
    lg = _dot_nt(wrt_ref[...], hb) + brt_ref[...]
    row = lambda j: lg[j:j + 1, :]
    g, _ = _first_argmax([row(j) for j in range(MOE_GROUPS)])
    ex = []
    for e in range(MOE_EPG):
        v = row(MOE_GROUPS + e)
        for gg in range(1, MOE_GROUPS):
            v = jnp.where(g == gg, row(MOE_GROUPS + gg * MOE_EPG + e), v)
        ex.append(v)
    i1, _ = _first_argmax(ex)
    neg = jnp.full(ex[0].shape, -jnp.inf, F32)
    i2, _ = _first_argmax([jnp.where(i1 == e, neg, ex[e]) for e in range(MOE_EPG)])
    lo = jnp.minimum(i1, i2)
    hi = jnp.maximum(i1, i2)
    pair = jnp.zeros(lo.shape, jnp.int32)
    for p, (a, b) in enumerate(PAIRS):
        pair = jnp.where((lo == a) & (hi == b), p, pair)
    bucket = g * N_PAIRS + pair

    t = bucket.shape[1]
    onehot = (lax.broadcasted_iota(jnp.int32, (ROUTE_ROWS, t), 0) == bucket).astype(F32)
    before = _dot(onehot.astype(BF16), triu_ref[...])
    rank = jnp.sum(onehot * (before + cnt_ref[:, 0:1]), axis=0, keepdims=True).astype(jnp.int32)
    cnt_ref[...] = cnt_ref[...] + jnp.sum(onehot, axis=1, keepdims=True)
    counts_ref[...] = cnt_ref[...]
    r8 = lax.broadcasted_iota(jnp.int32, (8, t), 0)
    route_ref[...] = jnp.where(r8 == 0, bucket, jnp.where(r8 == 1, rank, 0))


def _router(h, i, gffn, wrt, brt, triu):
    n = h.shape[0]
    return pl.pallas_call(
        _router_kernel,
        grid=(n // T_ROUTE,),
        in_specs=[
            pl.BlockSpec((T_ROUTE, D_MODEL), lambda t: (t, 0)),
            pl.BlockSpec((None, 1, D_MODEL), lambda t: (i, 0, 0)),
            pl.BlockSpec((None, ROUTE_ROWS, D_MODEL), lambda t: (i, 0, 0)),
            pl.BlockSpec((None, ROUTE_ROWS, T_ROUTE), lambda t: (i, 0, 0)),
            pl.BlockSpec((T_ROUTE, T_ROUTE), lambda t: (0, 0)),
        ],
        out_specs=[
            pl.BlockSpec((8, T_ROUTE), lambda t: (0, t)),
            pl.BlockSpec((ROUTE_ROWS, LANES), lambda t: (0, 0)),
        ],
        out_shape=[
            jax.ShapeDtypeStruct((8, n), jnp.int32),
            jax.ShapeDtypeStruct((ROUTE_ROWS, LANES), F32),
        ],
        scratch_shapes=[pltpu.VMEM((ROUTE_ROWS, LANES), F32)],
        compiler_params=_params(("arbitrary",)),
        name="moe_router",
    )(h, gffn, wrt, brt, triu)


def _dispatch_kernel(dest_ref, h_ref, xs_in_ref, xs_ref, sem):
    del xs_in_ref

    def issue(k, carry):
        base = pl.multiple_of(k * DMA_UNROLL, DMA_UNROLL)
        for u in range(DMA_UNROLL):
            d = dest_ref[0, 0, base + u]
            pltpu.make_async_copy(h_ref.at[pl.ds(base + u, 1), :], xs_ref.at[d], sem).start()
        return carry

    lax.fori_loop(0, T_ROW // DMA_UNROLL, issue, 0)
    pltpu.make_async_copy(h_ref, xs_ref.at[pl.ds(0, T_ROW), 0], sem).wait()


def _dispatch(h, dest3, xs_init):
    n = h.shape[0]
    return pl.pallas_call(
        _dispatch_kernel,
        grid=(n // T_ROW,),
        in_specs=[
            pl.BlockSpec((1, 1, T_ROW), lambda t: (t, 0, 0), memory_space=pltpu.SMEM),
            pl.BlockSpec((T_ROW, D_MODEL), lambda t: (t, 0)),
            pl.BlockSpec(memory_space=pl.ANY),
        ],
        out_specs=pl.BlockSpec(memory_space=pl.ANY),
        out_shape=jax.ShapeDtypeStruct(xs_init.shape, F32),
        scratch_shapes=[pltpu.SemaphoreType.DMA(())],
        input_output_aliases={2: 0},
        compiler_params=_params(("arbitrary",)),
        name="moe_dispatch",
    )(dest3, h, xs_init)


def _experts_kernel(tidx_ref, tea_ref, teb_ref, tg_ref, nv_ref,
                    xs_ref, gffn_ref, wr_ref, br_ref,
                    wga_ref, wua_ref, wda_ref, wgb_ref, wub_ref, wdb_ref,
                    ys_ref):
    del tidx_ref
    i = pl.program_id(0)

    @pl.when(i >= nv_ref[0])
    def _():
        ys_ref[...] = jnp.zeros_like(ys_ref)

    @pl.when(i < nv_ref[0])
    def _():
        x = _rms(xs_ref[...], gffn_ref[...]).astype(BF16)
        lg = _dot(x, wr_ref[...]) + br_ref[...]
        lane = lax.broadcasted_iota(jnp.int32, lg.shape, 1)
        glog = jnp.where(lane < MOE_GROUPS, lg, -jnp.inf)
        pe = jnp.exp(glog - jnp.max(glog, axis=-1, keepdims=True))
        g_w = (jnp.sum(jnp.where(lane == tg_ref[i], pe, 0.0), axis=-1, keepdims=True)
               / jnp.sum(pe, axis=-1, keepdims=True))
        la = jnp.sum(jnp.where(lane == MOE_GROUPS + tea_ref[i], lg, 0.0), axis=-1, keepdims=True)
        lb = jnp.sum(jnp.where(lane == MOE_GROUPS + teb_ref[i], lg, 0.0), axis=-1, keepdims=True)
        m = jnp.maximum(la, lb)
        pa = jnp.exp(la - m)
        pb = jnp.exp(lb - m)
        ca = pa / (pa + pb) * g_w
        cb = pb / (pa + pb) * g_w

        def expert(wg_ref, wu_ref, c):
            gate = _dot(x, wg_ref[...].astype(BF16))
            up = _dot(x, wu_ref[...].astype(BF16))
            return (jax.nn.silu(gate) * up * c).astype(BF16)

        y = (_dot(expert(wga_ref, wua_ref, ca), wda_ref[...].astype(BF16))
             + _dot(expert(wgb_ref, wub_ref, cb), wdb_ref[...].astype(BF16)))
        ys_ref[...] = y


def _experts(xs, i, tidx, tea, teb, tg, nv, gffn, wr, br, wg, wu, wd):
    n_tiles = xs.shape[0] // T_MOE
    e0 = i * MOE_EXPERTS
    ea_map = lambda t, ti, ea, eb, g, nv: (e0 + ea[t], 0, 0)
    eb_map = lambda t, ti, ea, eb, g, nv: (e0 + eb[t], 0, 0)
    up_spec = lambda m: pl.BlockSpec((None, D_MODEL, MOE_FF), m)
    down_spec = lambda m: pl.BlockSpec((None, MOE_FF, D_MODEL), m)
    grid_spec = pltpu.PrefetchScalarGridSpec(
        num_scalar_prefetch=5,
        grid=(n_tiles,),
        in_specs=[
            pl.BlockSpec((T_MOE, None, D_MODEL), lambda t, ti, *_: (ti[t], 0, 0)),
            pl.BlockSpec((None, 1, D_MODEL), lambda t, *_: (i, 0, 0)),
            pl.BlockSpec((None, D_MODEL, LANES), lambda t, *_: (i, 0, 0)),
            pl.BlockSpec((None, 1, LANES), lambda t, *_: (i, 0, 0)),
            up_spec(ea_map), up_spec(ea_map), down_spec(ea_map),
            up_spec(eb_map), up_spec(eb_map), down_spec(eb_map),
        ],
        out_specs=pl.BlockSpec((T_MOE, None, D_MODEL), lambda t, *_: (t, 0, 0)),
    )
    return pl.pallas_call(
        _experts_kernel,
        grid_spec=grid_spec,
        out_shape=jax.ShapeDtypeStruct(xs.shape, F32),
        compiler_params=_params(("arbitrary",)),
        name="moe_experts",
    )(tidx, tea, teb, tg, nv, xs, gffn, wr, br, wg, wu, wd, wg, wu, wd)


def _combine_kernel(dcur_ref, dnext_ref, h_ref, p_ref, ys_ref, gple_ref, wup_ref, wgate_ref,
                    gfin_ref, o_ref, wupb_ref, wgateb_ref, ybuf_ref, sem, *, final):
    i = pl.program_id(0)
    n = pl.num_programs(0)

    def issue(dref, slot):
        def body(k, carry):
            base = pl.multiple_of(k * DMA_UNROLL, DMA_UNROLL)
            for u in range(DMA_UNROLL):
                d = dref[0, 0, base + u]
                pltpu.make_async_copy(ys_ref.at[d], ybuf_ref.at[slot, pl.ds(base + u, 1), :],
                                      sem.at[slot]).start()
            return carry
        lax.fori_loop(0, T_ROW // DMA_UNROLL, body, 0)

    @pl.when(i == 0)
    def _():
        issue(dcur_ref, 0)
        _cast_weight(wupb_ref, wup_ref)
        _cast_weight(wgateb_ref, wgate_ref)

    @pl.when(i + 1 < n)
    def _():
        issue(dnext_ref, (i + 1) % 2)

    slot = i % 2
    pltpu.make_async_copy(ys_ref.at[pl.ds(0, T_ROW), 0], ybuf_ref.at[slot], sem.at[slot]).wait()
    h2 = h_ref[...] + ybuf_ref[slot]
    pn = _rms(h2, gple_ref[...]).astype(BF16)
    gate = jax.nn.sigmoid(_dot(pn, wgateb_ref[...]))
    up = _dot(p_ref[...].astype(BF16), wupb_ref[...])
    out = h2 + up * gate
    if final:
        out = _rms(out, gfin_ref[...])
    o_ref[...] = out


def _combine(h, dest3, p2, ys, i, gple, wup, wgate, gfin):
    n = h.shape[0]
    nt = n // T_ROW
    return pl.pallas_call(
        functools.partial(_combine_kernel, final=(i == DEPTH - 1)),
        grid=(nt,),
        in_specs=[
            pl.BlockSpec((1, 1, T_ROW), lambda t: (t, 0, 0), memory_space=pltpu.SMEM),
            pl.BlockSpec((1, 1, T_ROW), lambda t: (jnp.minimum(t + 1, nt - 1), 0, 0),
                         memory_space=pltpu.SMEM),
            pl.BlockSpec((T_ROW, D_MODEL), lambda t: (t, 0)),
            pl.BlockSpec((T_ROW, PLE_DIM), lambda t: (i * nt + t, 0)),
            pl.BlockSpec(memory_space=pl.ANY),
            pl.BlockSpec((None, 1, D_MODEL), lambda t: (i, 0, 0)),
            _resident((None, PLE_DIM, D_MODEL), lambda t: (i, 0, 0)),
            _resident((None, D_MODEL, D_MODEL), lambda t: (i, 0, 0)),
            pl.BlockSpec((1, D_MODEL), lambda t: (0, 0)),
        ],
        out_specs=pl.BlockSpec((T_ROW, D_MODEL), lambda t: (t, 0)),
        out_shape=jax.ShapeDtypeStruct((n, D_MODEL), F32),
        scratch_shapes=[
            pltpu.VMEM((PLE_DIM, D_MODEL), BF16),
            pltpu.VMEM((D_MODEL, D_MODEL), BF16),
            pltpu.VMEM((2, T_ROW, D_MODEL), F32),
            pltpu.SemaphoreType.DMA((2,)),
        ],
        compiler_params=_params(("arbitrary",)),
        name="moe_combine_ple",
    )(dest3, dest3, h, p2, ys, gple, wup, wgate, gfin)


def _plan(route, counts, n_tiles):
    bucket = route[0]
    rank = route[1]
    cnt = counts[:N_BUCKETS, 0].astype(jnp.int32)
    padded = ((cnt + T_MOE - 1) // T_MOE) * T_MOE
    ends = jnp.cumsum(padded)
    starts = ends - padded
    dest = starts[bucket] + rank
    nv = ends[-1] // T_MOE
    tidx = jnp.minimum(jnp.arange(n_tiles, dtype=jnp.int32), nv - 1)
    tb = jnp.sum((tidx * T_MOE)[:, None] >= ends[None, :], axis=1).astype(jnp.int32)
    tg = tb // N_PAIRS
    pa = jnp.asarray([a for a, _ in PAIRS], jnp.int32)[tb % N_PAIRS]
    pb = jnp.asarray([b for _, b in PAIRS], jnp.int32)[tb % N_PAIRS]
    return dest, tidx, tg * MOE_EPG + pa, tg * MOE_EPG + pb, tg, nv.reshape(1).astype(jnp.int32)


def kernel(x, p, gla_w_in, gla_w_gate_up, gla_b_gate, gla_g_head, gla_w_out, sgu_w_in, sgu_ln_g,
           sgu_ln_b, sgu_w_s, sgu_b_s, sgu_w_out, norm_mix_g, norm_ffn_g, norm_ple_g,
           moe_w_route_group, moe_b_route_group, moe_w_route_expert, moe_b_route_expert,
           moe_w_gate, moe_w_up, moe_w_down, ple_w_up, ple_w_gate, final_norm_g):
    batch, seq, d = x.shape
    assert d == D_MODEL and seq % T_GLA == 0
    n = batch * seq
    assert n % T_ROUTE == 0 and n % T_ROW == 0 and n % T_SGU == 0
    depth = p.shape[0]
    assert depth == DEPTH
    na, nb = gla_w_in.shape[0], sgu_w_in.shape[0]

    gla_win = jnp.pad(gla_w_in, ((0, 0), (0, 0), (0, GLA_A_PAD - GLA_RANK)))
    gla_wgu = jnp.pad(gla_w_gate_up, ((0, 0), (0, GLA_A_PAD - GLA_RANK), (0, 0)))
    gla_bg = gla_b_gate.reshape(na, 1, GLA_DK)
    gla_gh = gla_g_head.reshape(na, 1, GLA_DV)
    sgu_lng = sgu_ln_g.reshape(nb, 1, SGU_WIDTH)
    sgu_lnb = sgu_ln_b.reshape(nb, 1, SGU_WIDTH)
    sgu_bs = jnp.repeat(jnp.transpose(sgu_b_s, (0, 2, 1)), SGU_GC, axis=-1)
    g_mix = norm_mix_g.reshape(depth, 1, D_MODEL)
    g_ffn = norm_ffn_g.reshape(depth, 1, D_MODEL)
    g_ple = norm_ple_g.reshape(depth, 1, D_MODEL)
    g_fin = final_norm_g.reshape(1, D_MODEL)
    w_route = jnp.concatenate(
        [moe_w_route_group, moe_w_route_expert.reshape(depth, D_MODEL, MOE_EXPERTS)], axis=-1)
    b_route = jnp.concatenate(
        [moe_b_route_group, moe_b_route_expert.reshape(depth, MOE_EXPERTS)], axis=-1)
    n_logits = MOE_GROUPS + MOE_EXPERTS
    wr = jnp.pad(w_route, ((0, 0), (0, 0), (0, LANES - n_logits))).astype(BF16)
    br = jnp.pad(b_route, ((0, 0), (0, LANES - n_logits))).reshape(depth, 1, LANES)
    wrt = jnp.pad(jnp.transpose(w_route, (0, 2, 1)),
                  ((0, 0), (0, ROUTE_ROWS - n_logits), (0, 0))).astype(BF16)
    brt = jnp.broadcast_to(
        jnp.pad(b_route, ((0, 0), (0, ROUTE_ROWS - n_logits)))[:, :, None],
        (depth, ROUTE_ROWS, T_ROUTE))
    triu = jnp.triu(jnp.ones((T_ROUTE, T_ROUTE), BF16), k=1)
    wg = moe_w_gate.reshape(depth * MOE_EXPERTS, D_MODEL, MOE_FF)
    wu = moe_w_up.reshape(depth * MOE_EXPERTS, D_MODEL, MOE_FF)
    wd = moe_w_down.reshape(depth * MOE_EXPERTS, MOE_FF, D_MODEL)
    p2 = p.reshape(depth * n, PLE_DIM)

    n_tiles = n // T_MOE + N_BUCKETS
    h = x.reshape(n, D_MODEL)
    for i in range(depth):
        j = i // 2
        if i % 2 == 0:
            h = _gla_layer(h, batch, seq, j, i, g_mix, gla_win, gla_wgu, gla_bg, gla_gh, gla_w_out)
        else:
            h = _sgu_layer(h, j, i, g_mix, sgu_w_in, sgu_lng, sgu_lnb, sgu_w_s, sgu_bs, sgu_w_out)
        route, counts = _router(h, i, g_ffn, wrt, brt, triu)
        dest, tidx, tea, teb, tg, nv = _plan(route, counts, n_tiles)
        dest3 = dest.reshape(n // T_ROW, 1, T_ROW)
        xs = _dispatch(h, dest3, jnp.zeros((n_tiles * T_MOE, 1, D_MODEL), F32))
        ys = _experts(xs, i, tidx, tea, teb, tg, nv, g_ffn, wr, br, wg, wu, wd)
        h = _combine(h, dest3, p2, ys, i, g_ple, ple_w_up, ple_w_gate, g_fin)
    return h.reshape(batch, seq, D_MODEL)
```

```python
import functools

import jax
import jax.numpy as jnp
from jax import lax
from jax.experimental import pallas as pl
from jax.experimental.pallas import tpu as pltpu

F32 = jnp.float32
BF16 = jnp.bfloat16

D_MODEL = 1024
DEPTH = 4
GLA_HEADS = 4
GLA_DK = 512
GLA_DV = 1024
GLA_HK = GLA_DK // GLA_HEADS
GLA_HV = GLA_DV // GLA_HEADS
GLA_RANK = 16
GLA_TAU = 16.0
GLA_CHUNK = 64
SGU_WIDTH = 2048
SGU_GROUPS = 8
SGU_GC = SGU_WIDTH // SGU_GROUPS
SGU_CHUNK = 128
MOE_GROUPS = 4
MOE_EPG = 4
MOE_EXPERTS = MOE_GROUPS * MOE_EPG
MOE_FF = 512
PLE_DIM = 256
EPS = 1e-6

LANES = 128
GLA_A_PAD = LANES
GLA_ZW = 2 * GLA_DK + 2 * GLA_DV + GLA_A_PAD
ROUTE_ROWS = 32
PAIRS = ((0, 1), (0, 2), (1, 2), (1, 3), (2, 3), (0, 3))
N_PAIRS = len(PAIRS)
N_BUCKETS = MOE_GROUPS * N_PAIRS

T_GLA = 256
T_SGU = 256
T_ROUTE = 512
T_ROW = 256
T_MOE = 256
MOE_UNIT = 128
N_ZERO_FILLS = 2 * N_BUCKETS
DMA_UNROLL = 8
CAST_ROWS = 256
VMEM_LIMIT = 56 * 1024 * 1024


def _rms(x, g):
    ms = jnp.mean(x * x, axis=-1, keepdims=True)
    return x * lax.rsqrt(ms + EPS) * g


def _dot(a, b):
    return jnp.dot(a, b, preferred_element_type=F32)


def _dot_nt(a, b):
    return lax.dot_general(a, b, (((1,), (1,)), ((), ())), preferred_element_type=F32)


def _dot_tn(a, b):
    return lax.dot_general(a, b, (((0,), (0,)), ((), ())), preferred_element_type=F32)


def _params(sem):
    return pltpu.CompilerParams(dimension_semantics=sem, vmem_limit_bytes=VMEM_LIMIT)


def _resident(block_shape, index_map):
    return pl.BlockSpec(block_shape, index_map, pipeline_mode=pl.Buffered(1))


def _cast_weight(dst_ref, src_ref):
    rows = src_ref.shape[0]
    for r in range(0, rows, CAST_ROWS):
        dst_ref[r:r + CAST_ROWS, :] = src_ref[r:r + CAST_ROWS, :].astype(BF16)


def _gla_kernel(h_ref, gmix_ref, win_ref, wgu_ref, bg_ref, ghead_ref, wout_ref,
                o_ref, winb_ref, woutb_ref, st_ref, oacc_ref, y_ref):
    @pl.when((pl.program_id(0) == 0) & (pl.program_id(1) == 0))
    def _():
        _cast_weight(winb_ref, win_ref)
        _cast_weight(woutb_ref, wout_ref)

    @pl.when(pl.program_id(1) == 0)
    def _():
        st_ref[...] = jnp.zeros_like(st_ref)

    h = h_ref[...]
    hn = _rms(h, gmix_ref[...]).astype(BF16)
    z = _dot(hn, winb_ref[...])
    a = z[:, 2 * GLA_DK + 2 * GLA_DV:].astype(BF16)
    pre = _dot(a, wgu_ref[...].astype(BF16)) + bg_ref[...]
    la = (jnp.minimum(pre, 0.0) - jnp.log1p(jnp.exp(-jnp.abs(pre)))) * (1.0 / GLA_TAU)

    C = GLA_CHUNK
    ri = lax.broadcasted_iota(jnp.int32, (C, C), 0)
    ci = lax.broadcasted_iota(jnp.int32, (C, C), 1)
    causal = ri >= ci
    tri = causal.astype(BF16)

    for c in range(T_GLA // C):
        rows = slice(c * C, (c + 1) * C)
        la_c = la[rows]
        hi = la_c.astype(BF16)
        lo = (la_c - hi.astype(F32)).astype(BF16)
        bc = _dot(tri, hi) + _dot(tri, lo)
        bend = bc[C - 1:C, :]
        kc = z[rows, GLA_DK:2 * GLA_DK]
        qd = (z[rows, 0:GLA_DK] * (GLA_HK ** -0.5) * jnp.exp(bc)).astype(BF16)
        kd = (kc * jnp.exp(-bc)).astype(BF16)
        ke = (kc * jnp.exp(bend - bc)).astype(BF16)
        vc = z[rows, 2 * GLA_DK:2 * GLA_DK + GLA_DV].astype(BF16)
        dec = jnp.exp(bend)
        for hd in range(GLA_HEADS):
            ks = slice(hd * GLA_HK, (hd + 1) * GLA_HK)
            vs = slice(hd * GLA_HV, (hd + 1) * GLA_HV)
            att = jnp.where(causal, _dot_nt(qd[:, ks], kd[:, ks]), 0.0).astype(BF16)
            st = st_ref[hd]
            o = _dot(att, vc[:, vs]) + _dot_nt(qd[:, ks], st.astype(BF16))
            st_ref[hd] = st * dec[:, ks] + _dot_tn(vc[:, vs], ke[:, ks])
            oacc_ref[rows, vs] = o

    r0 = 2 * GLA_DK + GLA_DV
    for hd in range(GLA_HEADS):
        vs = slice(hd * GLA_HV, (hd + 1) * GLA_HV)
        on = _rms(oacc_ref[:, vs], ghead_ref[:, vs])
        r = z[:, r0 + hd * GLA_HV:r0 + (hd + 1) * GLA_HV]
        y_ref[:, vs] = (jax.nn.silu(r) * on).astype(BF16)
    o_ref[...] = h + _dot(y_ref[...], woutb_ref[...])


def _gla_layer(h, batch, seq, j, i, gmix, win, wgu, bg, ghead, wout):
    n = h.shape[0]
    ns = seq // T_GLA
    return pl.pallas_call(
        _gla_kernel,
        grid=(batch, ns),
        in_specs=[
            pl.BlockSpec((T_GLA, D_MODEL), lambda b, s: (b * ns + s, 0)),
            pl.BlockSpec((None, 1, D_MODEL), lambda b, s: (i, 0, 0)),
            _resident((None, D_MODEL, GLA_ZW), lambda b, s: (j, 0, 0)),
            pl.BlockSpec((None, GLA_A_PAD, GLA_DK), lambda b, s: (j, 0, 0)),
            pl.BlockSpec((None, 1, GLA_DK), lambda b, s: (j, 0, 0)),
            pl.BlockSpec((None, 1, GLA_DV), lambda b, s: (j, 0, 0)),
            _resident((None, GLA_DV, D_MODEL), lambda b, s: (j, 0, 0)),
        ],
        out_specs=pl.BlockSpec((T_GLA, D_MODEL), lambda b, s: (b * ns + s, 0)),
        out_shape=jax.ShapeDtypeStruct((n, D_MODEL), F32),
        scratch_shapes=[
            pltpu.VMEM((D_MODEL, GLA_ZW), BF16),
            pltpu.VMEM((GLA_DV, D_MODEL), BF16),
            pltpu.VMEM((GLA_HEADS, GLA_HV, GLA_HK), F32),
            pltpu.VMEM((T_GLA, GLA_DV), F32),
            pltpu.VMEM((T_GLA, GLA_DV), BF16),
        ],
        compiler_params=_params(("arbitrary", "arbitrary")),
        name="gla_mixer",
    )(h, gmix, win, wgu, bg, ghead, wout)


def _sgu_kernel(h_ref, gmix_ref, win_ref, lng_ref, lnb_ref, ws_ref, bs_ref, wout_ref,
                o_ref, winb_ref, woutb_ref, wsb_ref, us_ref):
    C = SGU_CHUNK

    @pl.when(pl.program_id(0) == 0)
    def _():
        _cast_weight(winb_ref, win_ref)
        _cast_weight(woutb_ref, wout_ref)
        ri = lax.broadcasted_iota(jnp.int32, (C, C), 0)
        ci = lax.broadcasted_iota(jnp.int32, (C, C), 1)
        for g in range(SGU_GROUPS):
            wsb_ref[g] = jnp.where(ri >= ci, ws_ref[g], 0.0).astype(BF16)

    h = h_ref[...]
    hn = _rms(h, gmix_ref[...]).astype(BF16)
    z = jax.nn.gelu(_dot(hn, winb_ref[...]), approximate=True)
    u = z[:, :SGU_WIDTH]
    v = z[:, SGU_WIDTH:]
    xc = v - jnp.mean(v, axis=-1, keepdims=True)
    vn = xc * lax.rsqrt(jnp.mean(xc * xc, axis=-1, keepdims=True) + EPS)
    vn = (vn * lng_ref[...] + lnb_ref[...]).astype(BF16)

    for g in range(SGU_GROUPS):
        cols = slice(g * SGU_GC, (g + 1) * SGU_GC)
        wc = wsb_ref[g]
        for c in range(T_SGU // C):
            rows = slice(c * C, (c + 1) * C)
            s = _dot(wc, vn[rows, cols]) + bs_ref[:, cols]
            us_ref[rows, cols] = (u[rows, cols] * s).astype(BF16)
    o_ref[...] = h + _dot(us_ref[...], woutb_ref[...])


def _sgu_layer(h, j, i, gmix, win, lng, lnb, ws, bs_full, wout):
    n = h.shape[0]
    return pl.pallas_call(
        _sgu_kernel,
        grid=(n // T_SGU,),
        in_specs=[
            pl.BlockSpec((T_SGU, D_MODEL), lambda t: (t, 0)),
            pl.BlockSpec((None, 1, D_MODEL), lambda t: (i, 0, 0)),
            _resident((None, D_MODEL, 2 * SGU_WIDTH), lambda t: (j, 0, 0)),
            pl.BlockSpec((None, 1, SGU_WIDTH), lambda t: (j, 0, 0)),
            pl.BlockSpec((None, 1, SGU_WIDTH), lambda t: (j, 0, 0)),
            _resident((None, SGU_GROUPS, SGU_CHUNK, SGU_CHUNK), lambda t: (j, 0, 0, 0)),
            _resident((None, SGU_CHUNK, SGU_WIDTH), lambda t: (j, 0, 0)),
            _resident((None, SGU_WIDTH, D_MODEL), lambda t: (j, 0, 0)),
        ],
        out_specs=pl.BlockSpec((T_SGU, D_MODEL), lambda t: (t, 0)),
        out_shape=jax.ShapeDtypeStruct((n, D_MODEL), F32),
        scratch_shapes=[
            pltpu.VMEM((D_MODEL, 2 * SGU_WIDTH), BF16),
            pltpu.VMEM((SGU_WIDTH, D_MODEL), BF16),
            pltpu.VMEM((SGU_GROUPS, SGU_CHUNK, SGU_CHUNK), BF16),
            pltpu.VMEM((T_SGU, SGU_WIDTH), BF16),
        ],
        compiler_params=_params(("arbitrary",)),
        name="sgu_mixer",
    )(h, gmix, win, lng, lnb, ws, bs_full, wout)


def _first_argmax(rows):
    best = rows[0]
    idx = jnp.zeros(best.shape, jnp.int32)
    for j in range(1, len(rows)):
        upd = rows[j] > best
        idx = jnp.where(upd, j, idx)
        best = jnp.where(upd, rows[j], best)
    return idx, best


def _router_kernel(h_ref, gffn_ref, wrt_ref, brt_ref, triu_ref, route_ref, counts_ref, cnt_ref):
    @pl.when(pl.program_id(0) == 0)
    def _():
        cnt_ref[...] = jnp.zeros_like(cnt_ref)

    hb = _rms(h_ref[...], gffn_ref[...]).astype(BF16)
    lg = _dot_nt(wrt_ref[...], hb) + brt_ref[...]
    row = lambda j: lg[j:j + 1, :]
    g, _ = _first_argmax([row(j) for j in range(MOE_GROUPS)])
    ex = []
    for e in range(MOE_EPG):
        v = row(MOE_GROUPS + e)
        for gg in range(1, MOE_GROUPS):
            v = jnp.where(g == gg, row(MOE_GROUPS + gg * MOE_EPG + e), v)
        ex.append(v)
    i1, _ = _first_argmax(ex)
    neg = jnp.full(ex[0].shape, -jnp.inf, F32)
    i2, _ = _first_argmax([jnp.where(i1 == e, neg, ex[e]) for e in range(MOE_EPG)])
    lo = jnp.minimum(i1, i2)
    hi = jnp.maximum(i1, i2)
    pair = jnp.zeros(lo.shape, jnp.int32)
    for p, (a, b) in enumerate(PAIRS):
        pair = jnp.where((lo == a) & (hi == b), p, pair)
    bucket = g * N_PAIRS + pair

    t = bucket.shape[1]
    onehot = (lax.broadcasted_iota(jnp.int32, (ROUTE_ROWS, t), 0) == bucket).astype(F32)
    before = _dot(onehot.astype(BF16), triu_ref[...])
    rank = jnp.sum(onehot * (before + cnt_ref[:, 0:1]), axis=0, keepdims=True).astype(jnp.int32)
    cnt_ref[...] = cnt_ref[...] + jnp.sum(onehot, axis=1, keepdims=True)
    counts_ref[...] = cnt_ref[...]
    r8 = lax.broadcasted_iota(jnp.int32, (8, t), 0)
    route_ref[...] = jnp.where(r8 == 0, bucket, jnp.where(r8 == 1, rank, 0))


def _router(h, i, gffn, wrt, brt, triu):
    n = h.shape[0]
    return pl.pallas_call(
        _router_kernel,
        grid=(n // T_ROUTE,),
        in_specs=[
            pl.BlockSpec((T_ROUTE, D_MODEL), lambda t: (t, 0)),
            pl.BlockSpec((None, 1, D_MODEL), lambda t: (i, 0, 0)),
            pl.BlockSpec((None, ROUTE_ROWS, D_MODEL), lambda t: (i, 0, 0)),
            pl.BlockSpec((None, ROUTE_ROWS, T_ROUTE), lambda t: (i, 0, 0)),
            pl.BlockSpec((T_ROUTE, T_ROUTE), lambda t: (0, 0)),
        ],
        out_specs=[
            pl.BlockSpec((8, T_ROUTE), lambda t: (0, t)),
            pl.BlockSpec((ROUTE_ROWS, LANES), lambda t: (0, 0)),
        ],
        out_shape=[
            jax.ShapeDtypeStruct((8, n), jnp.int32),
            jax.ShapeDtypeStruct((ROUTE_ROWS, LANES), F32),
        ],
        scratch_shapes=[pltpu.VMEM((ROUTE_ROWS, LANES), F32)],
        compiler_params=_params(("arbitrary",)),
        name="moe_router",
    )(h, gffn, wrt, brt, triu)


def _dispatch_kernel(zs_ref, dest_ref, h_ref, xs_ref, zero_ref, zsem, sem):
    @pl.when(pl.program_id(0) == 0)
    def _():
        zero_ref[...] = jnp.zeros_like(zero_ref)

        def fill(k):
            row = pl.multiple_of(zs_ref[k], MOE_UNIT)
            return pltpu.make_async_copy(zero_ref, xs_ref.at[pl.ds(row, MOE_UNIT), 0], zsem)

        for k in range(N_ZERO_FILLS):
            pl.when(zs_ref[N_ZERO_FILLS + k] != 0)(lambda k=k: fill(k).start())
        for k in range(N_ZERO_FILLS):
            pl.when(zs_ref[N_ZERO_FILLS + k] != 0)(lambda k=k: fill(k).wait())

    def issue(k, carry):
        base = pl.multiple_of(k * DMA_UNROLL, DMA_UNROLL)
        for u in range(DMA_UNROLL):
            d = dest_ref[0, 0, base + u]
            pltpu.make_async_copy(h_ref.at[pl.ds(base + u, 1), :], xs_ref.at[d], sem).start()
        return carry

    lax.fori_loop(0, T_ROW // DMA_UNROLL, issue, 0)
    pltpu.make_async_copy(h_ref, xs_ref.at[pl.ds(0, T_ROW), 0], sem).wait()


def _dispatch(h, dest3, zs):
    n = h.shape[0]
    grid_spec = pltpu.PrefetchScalarGridSpec(
        num_scalar_prefetch=1,
        grid=(n // T_ROW,),
        in_specs=[
            pl.BlockSpec((1, 1, T_ROW), lambda t, zs: (t, 0, 0), memory_space=pltpu.SMEM),
            pl.BlockSpec((T_ROW, D_MODEL), lambda t, zs: (t, 0)),
        ],
        out_specs=pl.BlockSpec(memory_space=pl.ANY),
        scratch_shapes=[
            pltpu.VMEM((MOE_UNIT, D_MODEL), F32),
            pltpu.SemaphoreType.DMA(()),
            pltpu.SemaphoreType.DMA(()),
        ],
    )
    return pl.pallas_call(
        _dispatch_kernel,
        grid_spec=grid_spec,
        out_shape=jax.ShapeDtypeStruct((_sorted_rows(n), 1, D_MODEL), F32),
        compiler_params=_params(("arbitrary",)),
        name="moe_dispatch",
    )(zs, dest3, h)


def _experts_kernel(bstart_ref, bunits_ref, bg_ref, bea_ref, beb_ref,
                    xs_ref, gffn_ref, wr_ref, br_ref,
                    wga_ref, wua_ref, wda_ref, wgb_ref, wub_ref, wdb_ref,
                    ys_ref,
                    wgab_ref, wuab_ref, wdab_ref, wgbb_ref, wubb_ref, wdbb_ref,
                    xbuf_ref, ybuf_ref, isem, osem):
    b = pl.program_id(0)
    pair = b % N_PAIRS

    def cast_a():
        _cast_weight(wgab_ref, wga_ref)
        _cast_weight(wuab_ref, wua_ref)
        _cast_weight(wdab_ref, wda_ref)

    def cast_b():
        _cast_weight(wgbb_ref, wgb_ref)
        _cast_weight(wubb_ref, wub_ref)
        _cast_weight(wdbb_ref, wdb_ref)

    a_changes = [q for q in range(N_PAIRS) if q == 0 or PAIRS[q][0] != PAIRS[q - 1][0]]
    b_changes = [q for q in range(N_PAIRS) if q == 0 or PAIRS[q][1] != PAIRS[q - 1][1]]
    pl.when(functools.reduce(jnp.logical_or, [pair == q for q in a_changes]))(cast_a)
    pl.when(functools.reduce(jnp.logical_or, [pair == q for q in b_changes]))(cast_b)

    row0 = bstart_ref[b]
    units = bunits_ref[b]
    n_full = units // (T_MOE // MOE_UNIT)
    has_tail = units % (T_MOE // MOE_UNIT)
    grp = bg_ref[b]
    ea = bea_ref[b]
    eb = beb_ref[b]

    def compute(x32):
        x = _rms(x32, gffn_ref[...]).astype(BF16)
        lg = _dot(x, wr_ref[...]) + br_ref[...]
        lane = lax.broadcasted_iota(jnp.int32, lg.shape, 1)
        glog = jnp.where(lane < MOE_GROUPS, lg, -jnp.inf)
        pe = jnp.exp(glog - jnp.max(glog, axis=-1, keepdims=True))
        g_w = (jnp.sum(jnp.where(lane == grp, pe, 0.0), axis=-1, keepdims=True)
               / jnp.sum(pe, axis=-1, keepdims=True))
        la = jnp.sum(jnp.where(lane == MOE_GROUPS + ea, lg, 0.0), axis=-1, keepdims=True)
        lb = jnp.sum(jnp.where(lane == MOE_GROUPS + eb, lg, 0.0), axis=-1, keepdims=True)
        m = jnp.maximum(la, lb)
        pa = jnp.exp(la - m)
        pb = jnp.exp(lb - m)
        ca = pa / (pa + pb) * g_w
        cb = pb / (pa + pb) * g_w

        def expert(wg_ref, wu_ref, c):
            return (jax.nn.silu(_dot(x, wg_ref[...])) * _dot(x, wu_ref[...]) * c).astype(BF16)

        return (_dot(expert(wgab_ref, wuab_ref, ca), wdab_ref[...])
                + _dot(expert(wgbb_ref, wubb_ref, cb), wdbb_ref[...]))

    def load(k, slot, rows):
        return pltpu.make_async_copy(xs_ref.at[pl.ds(row0 + k * T_MOE, rows), 0],
                                     xbuf_ref.at[slot, pl.ds(0, rows)], isem.at[slot])

    def store(k, slot, rows):
        return pltpu.make_async_copy(ybuf_ref.at[slot, pl.ds(0, rows)],
                                     ys_ref.at[pl.ds(row0 + k * T_MOE, rows), 0], osem.at[slot])

    @pl.when(n_full > 0)
    def _():
        load(0, 0, T_MOE).start()

    def body(k, carry):
        slot = k % 2

        @pl.when(k + 1 < n_full)
        def _():
            load(k + 1, 1 - slot, T_MOE).start()

        load(k, slot, T_MOE).wait()

        @pl.when(k >= 2)
        def _():
            store(k - 2, slot, T_MOE).wait()

        ybuf_ref[slot] = compute(xbuf_ref[slot])
        store(k, slot, T_MOE).start()
        return carry

    lax.fori_loop(0, n_full, body, 0)

    @pl.when(n_full >= 2)
    def _():
        store(n_full - 2, n_full % 2, T_MOE).wait()

    @pl.when(n_full >= 1)
    def _():
        store(n_full - 1, (n_full - 1) % 2, T_MOE).wait()

    @pl.when(has_tail > 0)
    def _():
        load(n_full, 0, MOE_UNIT).start()
        load(n_full, 0, MOE_UNIT).wait()
        ybuf_ref[0, 0:MOE_UNIT, :] = compute(xbuf_ref[0, 0:MOE_UNIT, :])
        store(n_full, 0, MOE_UNIT).start()
        store(n_full, 0, MOE_UNIT).wait()


def _experts(xs, i, bstart, bunits, gffn, wr, br, wg, wu, wd):
    e0 = i * MOE_EXPERTS
    bg = jnp.asarray([q // N_PAIRS for q in range(N_BUCKETS)], jnp.int32)
    bea = jnp.asarray([(q // N_PAIRS) * MOE_EPG + PAIRS[q % N_PAIRS][0] for q in range(N_BUCKETS)],
                      jnp.int32)
    beb = jnp.asarray([(q // N_PAIRS) * MOE_EPG + PAIRS[q % N_PAIRS][1] for q in range(N_BUCKETS)],
                      jnp.int32)
    ea_map = lambda b, bs, bu, g, ea, eb: (e0 + ea[b], 0, 0)
    eb_map = lambda b, bs, bu, g, ea, eb: (e0 + eb[b], 0, 0)
    up_spec = lambda m: pl.BlockSpec((None, D_MODEL, MOE_FF), m)
    down_spec = lambda m: pl.BlockSpec((None, MOE_FF, D_MODEL), m)
    grid_spec = pltpu.PrefetchScalarGridSpec(
        num_scalar_prefetch=5,
        grid=(N_BUCKETS,),
        in_specs=[
            pl.BlockSpec(memory_space=pl.ANY),
            pl.BlockSpec((None, 1, D_MODEL), lambda b, *_: (i, 0, 0)),
            pl.BlockSpec((None, D_MODEL, LANES), lambda b, *_: (i, 0, 0)),
            pl.BlockSpec((None, 1, LANES), lambda b, *_: (i, 0, 0)),
            up_spec(ea_map), up_spec(ea_map), down_spec(ea_map),
            up_spec(eb_map), up_spec(eb_map), down_spec(eb_map),
        ],
        out_specs=pl.BlockSpec(memory_space=pl.ANY),
        scratch_shapes=[
            pltpu.VMEM((D_MODEL, MOE_FF), BF16), pltpu.VMEM((D_MODEL, MOE_FF), BF16),
            pltpu.VMEM((MOE_FF, D_MODEL), BF16),
            pltpu.VMEM((D_MODEL, MOE_FF), BF16), pltpu.VMEM((D_MODEL, MOE_FF), BF16),
            pltpu.VMEM((MOE_FF, D_MODEL), BF16),
            pltpu.VMEM((2, T_MOE, D_MODEL), F32),
            pltpu.VMEM((2, T_MOE, D_MODEL), F32),
            pltpu.SemaphoreType.DMA((2,)),
            pltpu.SemaphoreType.DMA((2,)),
        ],
    )
    return pl.pallas_call(
        _experts_kernel,
        grid_spec=grid_spec,
        out_shape=jax.ShapeDtypeStruct(xs.shape, F32),
        input_output_aliases={5: 0},
        compiler_params=_params(("arbitrary",)),
        name="moe_experts",
    )(bstart, bunits, bg, bea, beb, xs, gffn, wr, br, wg, wu, wd, wg, wu, wd)


def _combine_kernel(dcur_ref, dnext_ref, h_ref, p_ref, ys_ref, gple_ref, wup_ref, wgate_ref,
                    gfin_ref, o_ref, wupb_ref, wgateb_ref, ybuf_ref, sem, *, final):
    i = pl.program_id(0)
    n = pl.num_programs(0)

    def issue(dref, slot):
        def body(k, carry):
            base = pl.multiple_of(k * DMA_UNROLL, DMA_UNROLL)
            for u in range(DMA_UNROLL):
                d = dref[0, 0, base + u]
                pltpu.make_async_copy(ys_ref.at[d], ybuf_ref.at[slot, pl.ds(base + u, 1), :],
                                      sem.at[slot]).start()
            return carry
        lax.fori_loop(0, T_ROW // DMA_UNROLL, body, 0)

    @pl.when(i == 0)
    def _():
        issue(dcur_ref, 0)
        _cast_weight(wupb_ref, wup_ref)
        _cast_weight(wgateb_ref, wgate_ref)

    @pl.when(i + 1 < n)
    def _():
        issue(dnext_ref, (i + 1) % 2)

    slot = i % 2
    pltpu.make_async_copy(ys_ref.at[pl.ds(0, T_ROW), 0], ybuf_ref.at[slot], sem.at[slot]).wait()
    h2 = h_ref[...] + ybuf_ref[slot]
    pn = _rms(h2, gple_ref[...]).astype(BF16)
    gate = jax.nn.sigmoid(_dot(pn, wgateb_ref[...]))
    up = _dot(p_ref[...].astype(BF16), wupb_ref[...])
    out = h2 + up * gate
    if final:
        out = _rms(out, gfin_ref[...])
    o_ref[...] = out


def _combine(h, dest3, p2, ys, i, gple, wup, wgate, gfin):
    n = h.shape[0]
    nt = n // T_ROW
    return pl.pallas_call(
        functools.partial(_combine_kernel, final=(i == DEPTH - 1)),
        grid=(nt,),
        in_specs=[
            pl.BlockSpec((1, 1, T_ROW), lambda t: (t, 0, 0), memory_space=pltpu.SMEM),
            pl.BlockSpec((1, 1, T_ROW), lambda t: (jnp.minimum(t + 1, nt - 1), 0, 0),
                         memory_space=pltpu.SMEM),
            pl.BlockSpec((T_ROW, D_MODEL), lambda t: (t, 0)),
            pl.BlockSpec((T_ROW, PLE_DIM), lambda t: (i * nt + t, 0)),
            pl.BlockSpec(memory_space=pl.ANY),
            pl.BlockSpec((None, 1, D_MODEL), lambda t: (i, 0, 0)),
            _resident((None, PLE_DIM, D_MODEL), lambda t: (i, 0, 0)),
            _resident((None, D_MODEL, D_MODEL), lambda t: (i, 0, 0)),
            pl.BlockSpec((1, D_MODEL), lambda t: (0, 0)),
        ],
        out_specs=pl.BlockSpec((T_ROW, D_MODEL), lambda t: (t, 0)),
        out_shape=jax.ShapeDtypeStruct((n, D_MODEL), F32),
        scratch_shapes=[
            pltpu.VMEM((PLE_DIM, D_MODEL), BF16),
            pltpu.VMEM((D_MODEL, D_MODEL), BF16),
            pltpu.VMEM((2, T_ROW, D_MODEL), F32),
            pltpu.SemaphoreType.DMA((2,)),
        ],
        compiler_params=_params(("arbitrary",)),
        name="moe_combine_ple",
    )(dest3, dest3, h, p2, ys, gple, wup, wgate, gfin)


def _sorted_rows(n):
    return n + N_BUCKETS * MOE_UNIT


def _plan(route, counts, n):
    bucket = route[0]
    rank = route[1]
    cnt = counts[:N_BUCKETS, 0].astype(jnp.int32)
    units = (cnt + MOE_UNIT - 1) // MOE_UNIT
    ends = jnp.cumsum(units * MOE_UNIT)
    starts = ends - units * MOE_UNIT
    dest = starts[bucket] + rank
    tail_rows = ends[-1] + jnp.arange(N_BUCKETS, dtype=jnp.int32) * MOE_UNIT
    fill_on = jnp.concatenate([cnt % MOE_UNIT != 0, tail_rows < _sorted_rows(n)])
    fill_row = jnp.where(fill_on, jnp.concatenate([ends - MOE_UNIT, tail_rows]), 0)
    zs = jnp.concatenate([fill_row, fill_on.astype(jnp.int32)]).astype(jnp.int32)
    return dest, starts.astype(jnp.int32), units.astype(jnp.int32), zs


def kernel(x, p, gla_w_in, gla_w_gate_up, gla_b_gate, gla_g_head, gla_w_out, sgu_w_in, sgu_ln_g,
           sgu_ln_b, sgu_w_s, sgu_b_s, sgu_w_out, norm_mix_g, norm_ffn_g, norm_ple_g,
           moe_w_route_group, moe_b_route_group, moe_w_route_expert, moe_b_route_expert,
           moe_w_gate, moe_w_up, moe_w_down, ple_w_up, ple_w_gate, final_norm_g):
    batch, seq, d = x.shape
    assert d == D_MODEL and seq % T_GLA == 0
    n = batch * seq
    assert n % T_ROUTE == 0 and n % T_ROW == 0 and n % T_SGU == 0
    depth = p.shape[0]
    assert depth == DEPTH
    na, nb = gla_w_in.shape[0], sgu_w_in.shape[0]

    gla_win = jnp.pad(gla_w_in, ((0, 0), (0, 0), (0, GLA_A_PAD - GLA_RANK)))
    gla_wgu = jnp.pad(gla_w_gate_up, ((0, 0), (0, GLA_A_PAD - GLA_RANK), (0, 0)))
    gla_bg = gla_b_gate.reshape(na, 1, GLA_DK)
    gla_gh = gla_g_head.reshape(na, 1, GLA_DV)
    sgu_lng = sgu_ln_g.reshape(nb, 1, SGU_WIDTH)
    sgu_lnb = sgu_ln_b.reshape(nb, 1, SGU_WIDTH)
    sgu_bs = jnp.repeat(jnp.transpose(sgu_b_s, (0, 2, 1)), SGU_GC, axis=-1)
    g_mix = norm_mix_g.reshape(depth, 1, D_MODEL)
    g_ffn = norm_ffn_g.reshape(depth, 1, D_MODEL)
    g_ple = norm_ple_g.reshape(depth, 1, D_MODEL)
    g_fin = final_norm_g.reshape(1, D_MODEL)
    w_route = jnp.concatenate(
        [moe_w_route_group, moe_w_route_expert.reshape(depth, D_MODEL, MOE_EXPERTS)], axis=-1)
    b_route = jnp.concatenate(
        [moe_b_route_group, moe_b_route_expert.reshape(depth, MOE_EXPERTS)], axis=-1)
    n_logits = MOE_GROUPS + MOE_EXPERTS
    wr = jnp.pad(w_route, ((0, 0), (0, 0), (0, LANES - n_logits))).astype(BF16)
    br = jnp.pad(b_route, ((0, 0), (0, LANES - n_logits))).reshape(depth, 1, LANES)
    wrt = jnp.pad(jnp.transpose(w_route, (0, 2, 1)),
                  ((0, 0), (0, ROUTE_ROWS - n_logits), (0, 0))).astype(BF16)
    brt = jnp.broadcast_to(
        jnp.pad(b_route, ((0, 0), (0, ROUTE_ROWS - n_logits)))[:, :, None],
        (depth, ROUTE_ROWS, T_ROUTE))
    triu = jnp.triu(jnp.ones((T_ROUTE, T_ROUTE), BF16), k=1)
    wg = moe_w_gate.reshape(depth * MOE_EXPERTS, D_MODEL, MOE_FF)
    wu = moe_w_up.reshape(depth * MOE_EXPERTS, D_MODEL, MOE_FF)
    wd = moe_w_down.reshape(depth * MOE_EXPERTS, MOE_FF, D_MODEL)
    p2 = p.reshape(depth * n, PLE_DIM)

    h = x.reshape(n, D_MODEL)
    for i in range(depth):
        j = i // 2
        if i % 2 == 0:
            h = _gla_layer(h, batch, seq, j, i, g_mix, gla_win, gla_wgu, gla_bg, gla_gh, gla_w_out)
        else:
            h = _sgu_layer(h, j, i, g_mix, sgu_w_in, sgu_lng, sgu_lnb, sgu_w_s, sgu_bs, sgu_w_out)
        route, counts = _router(h, i, g_ffn, wrt, brt, triu)
        dest, bstart, bunits, zs = _plan(route, counts, n)
        dest3 = dest.reshape(n // T_ROW, 1, T_ROW)
        xs = _dispatch(h, dest3, zs)
        ys = _experts(xs, i, bstart, bunits, g_ffn, wr, br, wg, wu, wd)
        h = _combine(h, dest3, p2, ys, i, g_ple, ple_w_up, ple_w_gate, g_fin)
    return h.reshape(batch, seq, D_MODEL)
```

```python
import functools

import jax
import jax.numpy as jnp
from jax import lax
from jax.experimental import pallas as pl
from jax.experimental.pallas import tpu as pltpu

F32 = jnp.float32
BF16 = jnp.bfloat16

D_MODEL = 1024
DEPTH = 4
GLA_HEADS = 4
GLA_DK = 512
GLA_DV = 1024
GLA_HK = GLA_DK // GLA_HEADS
GLA_HV = GLA_DV // GLA_HEADS
GLA_RANK = 16
GLA_TAU = 16.0
GLA_CHUNK = 64
SGU_WIDTH = 2048
SGU_GROUPS = 8
SGU_GC = SGU_WIDTH // SGU_GROUPS
SGU_CHUNK = 128
MOE_GROUPS = 4
MOE_EPG = 4
MOE_EXPERTS = MOE_GROUPS * MOE_EPG
MOE_FF = 512
PLE_DIM = 256
EPS = 1e-6

LANES = 128
ROW_SUB = D_MODEL // LANES
GLA_A_PAD = LANES
GLA_ZW = 2 * GLA_DK + 2 * GLA_DV + GLA_A_PAD
ROUTE_ROWS = 32
PAIRS = ((0, 1), (0, 2), (1, 2), (1, 3), (2, 3), (0, 3))
N_PAIRS = len(PAIRS)
N_BUCKETS = MOE_GROUPS * N_PAIRS

T_GLA = 256
T_SGU = 256
T_ROUTE = 512
T_ROW = 256
T_MOE = 256
MOE_UNIT = 128
N_ZERO_FILLS = 2 * N_BUCKETS
DMA_UNROLL = 8
CAST_ROWS = 256
VMEM_LIMIT = 56 * 1024 * 1024


def _rms(x, g):
    ms = jnp.mean(x * x, axis=-1, keepdims=True)
    return x * lax.rsqrt(ms + EPS) * g


def _dot(a, b):
    return jnp.dot(a, b, preferred_element_type=F32)


def _dot_nt(a, b):
    return lax.dot_general(a, b, (((1,), (1,)), ((), ())), preferred_element_type=F32)


def _dot_tn(a, b):
    return lax.dot_general(a, b, (((0,), (0,)), ((), ())), preferred_element_type=F32)


def _params(sem):
    return pltpu.CompilerParams(dimension_semantics=sem, vmem_limit_bytes=VMEM_LIMIT)


def _resident(block_shape, index_map):
    return pl.BlockSpec(block_shape, index_map, pipeline_mode=pl.Buffered(1))


def _cast_weight(dst_ref, src_ref):
    rows = src_ref.shape[0]
    for r in range(0, rows, CAST_ROWS):
        dst_ref[r:r + CAST_ROWS, :] = src_ref[r:r + CAST_ROWS, :].astype(BF16)


def _store_row_major(dst_ref, x):
    m = x.shape[0]
    for c in range(ROW_SUB):
        dst_ref[pl.ds(c, m, stride=ROW_SUB), :] = x[:, c * LANES:(c + 1) * LANES]


def _load_row_major(src_ref, m):
    return jnp.concatenate(
        [src_ref[pl.ds(c, m, stride=ROW_SUB), :] for c in range(ROW_SUB)], axis=-1)


def _row_block(ref, row, rows):
    return ref.at[pl.ds(pl.multiple_of(row * ROW_SUB, ROW_SUB), rows * ROW_SUB), :]


def _gla_kernel(h_ref, gmix_ref, win_ref, wgu_ref, bg_ref, ghead_ref, wout_ref,
                o_ref, winb_ref, woutb_ref, st_ref, oacc_ref, y_ref):
    @pl.when((pl.program_id(0) == 0) & (pl.program_id(1) == 0))
    def _():
        _cast_weight(winb_ref, win_ref)
        _cast_weight(woutb_ref, wout_ref)

    @pl.when(pl.program_id(1) == 0)
    def _():
        st_ref[...] = jnp.zeros_like(st_ref)

    h = h_ref[...]
    hn = _rms(h, gmix_ref[...]).astype(BF16)
    z = _dot(hn, winb_ref[...])
    a = z[:, 2 * GLA_DK + 2 * GLA_DV:].astype(BF16)
    pre = _dot(a, wgu_ref[...].astype(BF16)) + bg_ref[...]
    la = (jnp.minimum(pre, 0.0) - jnp.log1p(jnp.exp(-jnp.abs(pre)))) * (1.0 / GLA_TAU)

    C = GLA_CHUNK
    ri = lax.broadcasted_iota(jnp.int32, (C, C), 0)
    ci = lax.broadcasted_iota(jnp.int32, (C, C), 1)
    causal = ri >= ci
    tri = causal.astype(BF16)

    for c in range(T_GLA // C):
        rows = slice(c * C, (c + 1) * C)
        la_c = la[rows]
        hi = la_c.astype(BF16)
        lo = (la_c - hi.astype(F32)).astype(BF16)
        bc = _dot(tri, hi) + _dot(tri, lo)
        bend = bc[C - 1:C, :]
        kc = z[rows, GLA_DK:2 * GLA_DK]
        qd = (z[rows, 0:GLA_DK] * (GLA_HK ** -0.5) * jnp.exp(bc)).astype(BF16)
        kd = (kc * jnp.exp(-bc)).astype(BF16)
        ke = (kc * jnp.exp(bend - bc)).astype(BF16)
        vc = z[rows, 2 * GLA_DK:2 * GLA_DK + GLA_DV].astype(BF16)
        dec = jnp.exp(bend)
        for hd in range(GLA_HEADS):
            ks = slice(hd * GLA_HK, (hd + 1) * GLA_HK)
            vs = slice(hd * GLA_HV, (hd + 1) * GLA_HV)
            att = jnp.where(causal, _dot_nt(qd[:, ks], kd[:, ks]), 0.0).astype(BF16)
            st = st_ref[hd]
            o = _dot(att, vc[:, vs]) + _dot_nt(qd[:, ks], st.astype(BF16))
            st_ref[hd] = st * dec[:, ks] + _dot_tn(vc[:, vs], ke[:, ks])
            oacc_ref[rows, vs] = o

    r0 = 2 * GLA_DK + GLA_DV
    for hd in range(GLA_HEADS):
        vs = slice(hd * GLA_HV, (hd + 1) * GLA_HV)
        on = _rms(oacc_ref[:, vs], ghead_ref[:, vs])
        r = z[:, r0 + hd * GLA_HV:r0 + (hd + 1) * GLA_HV]
        y_ref[:, vs] = (jax.nn.silu(r) * on).astype(BF16)
    o_ref[...] = h + _dot(y_ref[...], woutb_ref[...])


def _gla_layer(h, batch, seq, j, i, gmix, win, wgu, bg, ghead, wout):
    n = h.shape[0]
    ns = seq // T_GLA
    return pl.pallas_call(
        _gla_kernel,
        grid=(batch, ns),
        in_specs=[
            pl.BlockSpec((T_GLA, D_MODEL), lambda b, s: (b * ns + s, 0)),
            pl.BlockSpec((None, 1, D_MODEL), lambda b, s: (i, 0, 0)),
            _resident((None, D_MODEL, GLA_ZW), lambda b, s: (j, 0, 0)),
            pl.BlockSpec((None, GLA_A_PAD, GLA_DK), lambda b, s: (j, 0, 0)),
            pl.BlockSpec((None, 1, GLA_DK), lambda b, s: (j, 0, 0)),
            pl.BlockSpec((None, 1, GLA_DV), lambda b, s: (j, 0, 0)),
            _resident((None, GLA_DV, D_MODEL), lambda b, s: (j, 0, 0)),
        ],
        out_specs=pl.BlockSpec((T_GLA, D_MODEL), lambda b, s: (b * ns + s, 0)),
        out_shape=jax.ShapeDtypeStruct((n, D_MODEL), F32),
        scratch_shapes=[
            pltpu.VMEM((D_MODEL, GLA_ZW), BF16),
            pltpu.VMEM((GLA_DV, D_MODEL), BF16),
            pltpu.VMEM((GLA_HEADS, GLA_HV, GLA_HK), F32),
            pltpu.VMEM((T_GLA, GLA_DV), F32),
            pltpu.VMEM((T_GLA, GLA_DV), BF16),
        ],
        compiler_params=_params(("arbitrary", "arbitrary")),
        name="gla_mixer",
    )(h, gmix, win, wgu, bg, ghead, wout)


def _sgu_kernel(h_ref, gmix_ref, win_ref, lng_ref, lnb_ref, ws_ref, bs_ref, wout_ref,
                o_ref, winb_ref, woutb_ref, wsb_ref, us_ref):
    C = SGU_CHUNK

    @pl.when(pl.program_id(0) == 0)
    def _():
        _cast_weight(winb_ref, win_ref)
        _cast_weight(woutb_ref, wout_ref)
        ri = lax.broadcasted_iota(jnp.int32, (C, C), 0)
        ci = lax.broadcasted_iota(jnp.int32, (C, C), 1)
        for g in range(SGU_GROUPS):
            wsb_ref[g] = jnp.where(ri >= ci, ws_ref[g], 0.0).astype(BF16)

    h = h_ref[...]
    hn = _rms(h, gmix_ref[...]).astype(BF16)
    z = jax.nn.gelu(_dot(hn, winb_ref[...]), approximate=True)
    u = z[:, :SGU_WIDTH]
    v = z[:, SGU_WIDTH:]
    xc = v - jnp.mean(v, axis=-1, keepdims=True)
    vn = xc * lax.rsqrt(jnp.mean(xc * xc, axis=-1, keepdims=True) + EPS)
    vn = (vn * lng_ref[...] + lnb_ref[...]).astype(BF16)

    for g in range(SGU_GROUPS):
        cols = slice(g * SGU_GC, (g + 1) * SGU_GC)
        wc = wsb_ref[g]
        for c in range(T_SGU // C):
            rows = slice(c * C, (c + 1) * C)
            s = _dot(wc, vn[rows, cols]) + bs_ref[:, cols]
            us_ref[rows, cols] = (u[rows, cols] * s).astype(BF16)
    o_ref[...] = h + _dot(us_ref[...], woutb_ref[...])


def _sgu_layer(h, j, i, gmix, win, lng, lnb, ws, bs_full, wout):
    n = h.shape[0]
    return pl.pallas_call(
        _sgu_kernel,
        grid=(n // T_SGU,),
        in_specs=[
            pl.BlockSpec((T_SGU, D_MODEL), lambda t: (t, 0)),
            pl.BlockSpec((None, 1, D_MODEL), lambda t: (i, 0, 0)),
            _resident((None, D_MODEL, 2 * SGU_WIDTH), lambda t: (j, 0, 0)),
            pl.BlockSpec((None, 1, SGU_WIDTH), lambda t: (j, 0, 0)),
            pl.BlockSpec((None, 1, SGU_WIDTH), lambda t: (j, 0, 0)),
            _resident((None, SGU_GROUPS, SGU_CHUNK, SGU_CHUNK), lambda t: (j, 0, 0, 0)),
            _resident((None, SGU_CHUNK, SGU_WIDTH), lambda t: (j, 0, 0)),
            _resident((None, SGU_WIDTH, D_MODEL), lambda t: (j, 0, 0)),
        ],
        out_specs=pl.BlockSpec((T_SGU, D_MODEL), lambda t: (t, 0)),
        out_shape=jax.ShapeDtypeStruct((n, D_MODEL), F32),
        scratch_shapes=[
            pltpu.VMEM((D_MODEL, 2 * SGU_WIDTH), BF16),
            pltpu.VMEM((SGU_WIDTH, D_MODEL), BF16),
            pltpu.VMEM((SGU_GROUPS, SGU_CHUNK, SGU_CHUNK), BF16),
            pltpu.VMEM((T_SGU, SGU_WIDTH), BF16),
        ],
        compiler_params=_params(("arbitrary",)),
        name="sgu_mixer",
    )(h, gmix, win, lng, lnb, ws, bs_full, wout)


def _first_argmax(rows):
    best = rows[0]
    idx = jnp.zeros(best.shape, jnp.int32)
    for j in range(1, len(rows)):
        upd = rows[j] > best
        idx = jnp.where(upd, j, idx)
        best = jnp.where(upd, rows[j], best)
    return idx, best


def _router_kernel(h_ref, gffn_ref, wrt_ref, brt_ref, triu_ref, route_ref, counts_ref, cnt_ref):
    @pl.when(pl.program_id(0) == 0)
    def _():
        cnt_ref[...] = jnp.zeros_like(cnt_ref)

    hb = _rms(h_ref[...], gffn_ref[...]).astype(BF16)
    lg = _dot_nt(wrt_ref[...], hb) + brt_ref[...]
    row = lambda j: lg[j:j + 1, :]
    g, _ = _first_argmax([row(j) for j in range(MOE_GROUPS)])
    ex = []
    for e in range(MOE_EPG):
        v = row(MOE_GROUPS + e)
        for gg in range(1, MOE_GROUPS):
            v = jnp.where(g == gg, row(MOE_GROUPS + gg * MOE_EPG + e), v)
        ex.append(v)
    i1, _ = _first_argmax(ex)
    neg = jnp.full(ex[0].shape, -jnp.inf, F32)
    i2, _ = _first_argmax([jnp.where(i1 == e, neg, ex[e]) for e in range(MOE_EPG)])
    lo = jnp.minimum(i1, i2)
    hi = jnp.maximum(i1, i2)
    pair = jnp.zeros(lo.shape, jnp.int32)
    for p, (a, b) in enumerate(PAIRS):
        pair = jnp.where((lo == a) & (hi == b), p, pair)
    bucket = g * N_PAIRS + pair

    t = bucket.shape[1]
    onehot = (lax.broadcasted_iota(jnp.int32, (ROUTE_ROWS, t), 0) == bucket).astype(F32)
    before = _dot(onehot.astype(BF16), triu_ref[...])
    rank = jnp.sum(onehot * (before + cnt_ref[:, 0:1]), axis=0, keepdims=True).astype(jnp.int32)
    cnt_ref[...] = cnt_ref[...] + jnp.sum(onehot, axis=1, keepdims=True)
    counts_ref[...] = cnt_ref[...]
    r8 = lax.broadcasted_iota(jnp.int32, (8, t), 0)
    route_ref[...] = jnp.where(r8 == 0, bucket, jnp.where(r8 == 1, rank, 0))


def _router(h, i, gffn, wrt, brt, triu):
    n = h.shape[0]
    return pl.pallas_call(
        _router_kernel,
        grid=(n // T_ROUTE,),
        in_specs=[
            pl.BlockSpec((T_ROUTE, D_MODEL), lambda t: (t, 0)),
            pl.BlockSpec((None, 1, D_MODEL), lambda t: (i, 0, 0)),
            pl.BlockSpec((None, ROUTE_ROWS, D_MODEL), lambda t: (i, 0, 0)),
            pl.BlockSpec((None, ROUTE_ROWS, T_ROUTE), lambda t: (i, 0, 0)),
            pl.BlockSpec((T_ROUTE, T_ROUTE), lambda t: (0, 0)),
        ],
        out_specs=[
            pl.BlockSpec((8, T_ROUTE), lambda t: (0, t)),
            pl.BlockSpec((ROUTE_ROWS, LANES), lambda t: (0, 0)),
        ],
        out_shape=[
            jax.ShapeDtypeStruct((8, n), jnp.int32),
            jax.ShapeDtypeStruct((ROUTE_ROWS, LANES), F32),
        ],
        scratch_shapes=[pltpu.VMEM((ROUTE_ROWS, LANES), F32)],
        compiler_params=_params(("arbitrary",)),
        name="moe_router",
    )(h, gffn, wrt, brt, triu)


def _dispatch_kernel(zs_ref, dest_ref, h_ref, xs_ref, zero_ref, rows_ref, zsem, sem):
    @pl.when(pl.program_id(0) == 0)
    def _():
        zero_ref[...] = jnp.zeros_like(zero_ref)

        def fill(k):
            return pltpu.make_async_copy(zero_ref, _row_block(xs_ref, zs_ref[k], MOE_UNIT), zsem)

        for k in range(N_ZERO_FILLS):
            pl.when(zs_ref[N_ZERO_FILLS + k] != 0)(lambda k=k: fill(k).start())
        for k in range(N_ZERO_FILLS):
            pl.when(zs_ref[N_ZERO_FILLS + k] != 0)(lambda k=k: fill(k).wait())

    _store_row_major(rows_ref, h_ref[...])

    def issue(k, carry):
        base = pl.multiple_of(k * DMA_UNROLL, DMA_UNROLL)
        for u in range(DMA_UNROLL):
            d = dest_ref[0, 0, base + u]
            pltpu.make_async_copy(_row_block(rows_ref, base + u, 1),
                                  _row_block(xs_ref, d, 1), sem).start()
        return carry

    lax.fori_loop(0, T_ROW // DMA_UNROLL, issue, 0)
    pltpu.make_async_copy(rows_ref, _row_block(xs_ref, 0, T_ROW), sem).wait()


def _dispatch(h, dest3, zs):
    n = h.shape[0]
    grid_spec = pltpu.PrefetchScalarGridSpec(
        num_scalar_prefetch=1,
        grid=(n // T_ROW,),
        in_specs=[
            pl.BlockSpec((1, 1, T_ROW), lambda t, zs: (t, 0, 0), memory_space=pltpu.SMEM),
            pl.BlockSpec((T_ROW, D_MODEL), lambda t, zs: (t, 0)),
        ],
        out_specs=pl.BlockSpec(memory_space=pl.ANY),
        scratch_shapes=[
            pltpu.VMEM((MOE_UNIT * ROW_SUB, LANES), F32),
            pltpu.VMEM((T_ROW * ROW_SUB, LANES), F32),
            pltpu.SemaphoreType.DMA(()),
            pltpu.SemaphoreType.DMA(()),
        ],
    )
    return pl.pallas_call(
        _dispatch_kernel,
        grid_spec=grid_spec,
        out_shape=jax.ShapeDtypeStruct((_sorted_rows(n) * ROW_SUB, LANES), F32),
        compiler_params=_params(("arbitrary",)),
        name="moe_dispatch",
    )(zs, dest3, h)


def _experts_kernel(bstart_ref, bunits_ref, bg_ref, bea_ref, beb_ref,
                    xs_ref, gffn_ref, wr_ref, br_ref,
                    wga_ref, wua_ref, wda_ref, wgb_ref, wub_ref, wdb_ref,
                    ys_ref,
                    wgab_ref, wuab_ref, wdab_ref, wgbb_ref, wubb_ref, wdbb_ref,
                    xbuf_ref, ybuf_ref, isem, osem):
    b = pl.program_id(0)
    pair = b % N_PAIRS

    def cast_a():
        _cast_weight(wgab_ref, wga_ref)
        _cast_weight(wuab_ref, wua_ref)
        _cast_weight(wdab_ref, wda_ref)

    def cast_b():
        _cast_weight(wgbb_ref, wgb_ref)
        _cast_weight(wubb_ref, wub_ref)
        _cast_weight(wdbb_ref, wdb_ref)

    a_changes = [q for q in range(N_PAIRS) if q == 0 or PAIRS[q][0] != PAIRS[q - 1][0]]
    b_changes = [q for q in range(N_PAIRS) if q == 0 or PAIRS[q][1] != PAIRS[q - 1][1]]
    pl.when(functools.reduce(jnp.logical_or, [pair == q for q in a_changes]))(cast_a)
    pl.when(functools.reduce(jnp.logical_or, [pair == q for q in b_changes]))(cast_b)

    row0 = bstart_ref[b]
    units = bunits_ref[b]
    n_full = units // (T_MOE // MOE_UNIT)
    has_tail = units % (T_MOE // MOE_UNIT)
    grp = bg_ref[b]
    ea = bea_ref[b]
    eb = beb_ref[b]

    def compute(x32):
        x = _rms(x32, gffn_ref[...]).astype(BF16)
        lg = _dot(x, wr_ref[...]) + br_ref[...]
        lane = lax.broadcasted_iota(jnp.int32, lg.shape, 1)
        glog = jnp.where(lane < MOE_GROUPS, lg, -jnp.inf)
        pe = jnp.exp(glog - jnp.max(glog, axis=-1, keepdims=True))
        g_w = (jnp.sum(jnp.where(lane == grp, pe, 0.0), axis=-1, keepdims=True)
               / jnp.sum(pe, axis=-1, keepdims=True))
        la = jnp.sum(jnp.where(lane == MOE_GROUPS + ea, lg, 0.0), axis=-1, keepdims=True)
        lb = jnp.sum(jnp.where(lane == MOE_GROUPS + eb, lg, 0.0), axis=-1, keepdims=True)
        m = jnp.maximum(la, lb)
        pa = jnp.exp(la - m)
        pb = jnp.exp(lb - m)
        ca = pa / (pa + pb) * g_w
        cb = pb / (pa + pb) * g_w

        def expert(wg_ref, wu_ref, c):
            return (jax.nn.silu(_dot(x, wg_ref[...])) * _dot(x, wu_ref[...]) * c).astype(BF16)

        return (_dot(expert(wgab_ref, wuab_ref, ca), wdab_ref[...])
                + _dot(expert(wgbb_ref, wubb_ref, cb), wdbb_ref[...]))

    def load(k, slot, rows):
        return pltpu.make_async_copy(_row_block(xs_ref, row0 + k * T_MOE, rows),
                                     _row_block(xbuf_ref.at[slot], 0, rows), isem.at[slot])

    def store(k, slot, rows):
        return pltpu.make_async_copy(_row_block(ybuf_ref.at[slot], 0, rows),
                                     _row_block(ys_ref, row0 + k * T_MOE, rows), osem.at[slot])

    @pl.when(n_full > 0)
    def _():
        load(0, 0, T_MOE).start()

    def body(k, carry):
        slot = k % 2

        @pl.when(k + 1 < n_full)
        def _():
            load(k + 1, 1 - slot, T_MOE).start()

        load(k, slot, T_MOE).wait()

        @pl.when(k >= 2)
        def _():
            store(k - 2, slot, T_MOE).wait()

        _store_row_major(ybuf_ref.at[slot], compute(_load_row_major(xbuf_ref.at[slot], T_MOE)))
        store(k, slot, T_MOE).start()
        return carry

    lax.fori_loop(0, n_full, body, 0)

    @pl.when(n_full >= 2)
    def _():
        store(n_full - 2, n_full % 2, T_MOE).wait()

    @pl.when(n_full >= 1)
    def _():
        store(n_full - 1, (n_full - 1) % 2, T_MOE).wait()

    @pl.when(has_tail > 0)
    def _():
        load(n_full, 0, MOE_UNIT).start()
        load(n_full, 0, MOE_UNIT).wait()
        _store_row_major(ybuf_ref.at[0], compute(_load_row_major(xbuf_ref.at[0], MOE_UNIT)))
        store(n_full, 0, MOE_UNIT).start()
        store(n_full, 0, MOE_UNIT).wait()


def _experts(xs, i, bstart, bunits, gffn, wr, br, wg, wu, wd):
    e0 = i * MOE_EXPERTS
    bg = jnp.asarray([q // N_PAIRS for q in range(N_BUCKETS)], jnp.int32)
    bea = jnp.asarray([(q // N_PAIRS) * MOE_EPG + PAIRS[q % N_PAIRS][0] for q in range(N_BUCKETS)],
                      jnp.int32)
    beb = jnp.asarray([(q // N_PAIRS) * MOE_EPG + PAIRS[q % N_PAIRS][1] for q in range(N_BUCKETS)],
                      jnp.int32)
    ea_map = lambda b, bs, bu, g, ea, eb: (e0 + ea[b], 0, 0)
    eb_map = lambda b, bs, bu, g, ea, eb: (e0 + eb[b], 0, 0)
    up_spec = lambda m: pl.BlockSpec((None, D_MODEL, MOE_FF), m)
    down_spec = lambda m: pl.BlockSpec((None, MOE_FF, D_MODEL), m)
    grid_spec = pltpu.PrefetchScalarGridSpec(
        num_scalar_prefetch=5,
        grid=(N_BUCKETS,),
        in_specs=[
            pl.BlockSpec(memory_space=pl.ANY),
            pl.BlockSpec((None, 1, D_MODEL), lambda b, *_: (i, 0, 0)),
            pl.BlockSpec((None, D_MODEL, LANES), lambda b, *_: (i, 0, 0)),
            pl.BlockSpec((None, 1, LANES), lambda b, *_: (i, 0, 0)),
            up_spec(ea_map), up_spec(ea_map), down_spec(ea_map),
            up_spec(eb_map), up_spec(eb_map), down_spec(eb_map),
        ],
        out_specs=pl.BlockSpec(memory_space=pl.ANY),
        scratch_shapes=[
            pltpu.VMEM((D_MODEL, MOE_FF), BF16), pltpu.VMEM((D_MODEL, MOE_FF), BF16),
            pltpu.VMEM((MOE_FF, D_MODEL), BF16),
            pltpu.VMEM((D_MODEL, MOE_FF), BF16), pltpu.VMEM((D_MODEL, MOE_FF), BF16),
            pltpu.VMEM((MOE_FF, D_MODEL), BF16),
            pltpu.VMEM((2, T_MOE * ROW_SUB, LANES), F32),
            pltpu.VMEM((2, T_MOE * ROW_SUB, LANES), F32),
            pltpu.SemaphoreType.DMA((2,)),
            pltpu.SemaphoreType.DMA((2,)),
        ],
    )
    return pl.pallas_call(
        _experts_kernel,
        grid_spec=grid_spec,
        out_shape=jax.ShapeDtypeStruct(xs.shape, F32),
        input_output_aliases={5: 0},
        compiler_params=_params(("arbitrary",)),
        name="moe_experts",
    )(bstart, bunits, bg, bea, beb, xs, gffn, wr, br, wg, wu, wd, wg, wu, wd)


def _combine_kernel(dcur_ref, dnext_ref, h_ref, p_ref, ys_ref, gple_ref, wup_ref, wgate_ref,
                    gfin_ref, o_ref, wupb_ref, wgateb_ref, ybuf_ref, sem, *, final):
    i = pl.program_id(0)
    n = pl.num_programs(0)

    def issue(dref, slot):
        def body(k, carry):
            base = pl.multiple_of(k * DMA_UNROLL, DMA_UNROLL)
            for u in range(DMA_UNROLL):
                d = dref[0, 0, base + u]
                pltpu.make_async_copy(_row_block(ys_ref, d, 1),
                                      _row_block(ybuf_ref.at[slot], base + u, 1),
                                      sem.at[slot]).start()
            return carry
        lax.fori_loop(0, T_ROW // DMA_UNROLL, body, 0)

    @pl.when(i == 0)
    def _():
        issue(dcur_ref, 0)
        _cast_weight(wupb_ref, wup_ref)
        _cast_weight(wgateb_ref, wgate_ref)

    @pl.when(i + 1 < n)
    def _():
        issue(dnext_ref, (i + 1) % 2)

    slot = i % 2
    pltpu.make_async_copy(_row_block(ys_ref, 0, T_ROW), ybuf_ref.at[slot], sem.at[slot]).wait()
    h2 = h_ref[...] + _load_row_major(ybuf_ref.at[slot], T_ROW)
    pn = _rms(h2, gple_ref[...]).astype(BF16)
    gate = jax.nn.sigmoid(_dot(pn, wgateb_ref[...]))
    up = _dot(p_ref[...].astype(BF16), wupb_ref[...])
    out = h2 + up * gate
    if final:
        out = _rms(out, gfin_ref[...])
    o_ref[...] = out


def _combine(h, dest3, p2, ys, i, gple, wup, wgate, gfin):
    n = h.shape[0]
    nt = n // T_ROW
    return pl.pallas_call(
        functools.partial(_combine_kernel, final=(i == DEPTH - 1)),
        grid=(nt,),
        in_specs=[
            pl.BlockSpec((1, 1, T_ROW), lambda t: (t, 0, 0), memory_space=pltpu.SMEM),
            pl.BlockSpec((1, 1, T_ROW), lambda t: (jnp.minimum(t + 1, nt - 1), 0, 0),
                         memory_space=pltpu.SMEM),
            pl.BlockSpec((T_ROW, D_MODEL), lambda t: (t, 0)),
            pl.BlockSpec((T_ROW, PLE_DIM), lambda t: (i * nt + t, 0)),
            pl.BlockSpec(memory_space=pl.ANY),
            pl.BlockSpec((None, 1, D_MODEL), lambda t: (i, 0, 0)),
            _resident((None, PLE_DIM, D_MODEL), lambda t: (i, 0, 0)),
            _resident((None, D_MODEL, D_MODEL), lambda t: (i, 0, 0)),
            pl.BlockSpec((1, D_MODEL), lambda t: (0, 0)),
        ],
        out_specs=pl.BlockSpec((T_ROW, D_MODEL), lambda t: (t, 0)),
        out_shape=jax.ShapeDtypeStruct((n, D_MODEL), F32),
        scratch_shapes=[
            pltpu.VMEM((PLE_DIM, D_MODEL), BF16),
            pltpu.VMEM((D_MODEL, D_MODEL), BF16),
            pltpu.VMEM((2, T_ROW * ROW_SUB, LANES), F32),
            pltpu.SemaphoreType.DMA((2,)),
        ],
        compiler_params=_params(("arbitrary",)),
        name="moe_combine_ple",
    )(dest3, dest3, h, p2, ys, gple, wup, wgate, gfin)


def _sorted_rows(n):
    return n + N_BUCKETS * MOE_UNIT


def _plan(route, counts, n):
    bucket = route[0]
    rank = route[1]
    cnt = counts[:N_BUCKETS, 0].astype(jnp.int32)
    units = (cnt + MOE_UNIT - 1) // MOE_UNIT
    ends = jnp.cumsum(units * MOE_UNIT)
    starts = ends - units * MOE_UNIT
    dest = starts[bucket] + rank
    tail_rows = ends[-1] + jnp.arange(N_BUCKETS, dtype=jnp.int32) * MOE_UNIT
    fill_on = jnp.concatenate([cnt % MOE_UNIT != 0, tail_rows < _sorted_rows(n)])
    fill_row = jnp.where(fill_on, jnp.concatenate([ends - MOE_UNIT, tail_rows]), 0)
    zs = jnp.concatenate([fill_row, fill_on.astype(jnp.int32)]).astype(jnp.int32)
    return dest, starts.astype(jnp.int32), units.astype(jnp.int32), zs


def kernel(x, p, gla_w_in, gla_w_gate_up, gla_b_gate, gla_g_head, gla_w_out, sgu_w_in, sgu_ln_g,
           sgu_ln_b, sgu_w_s, sgu_b_s, sgu_w_out, norm_mix_g, norm_ffn_g, norm_ple_g,
           moe_w_route_group, moe_b_route_group, moe_w_route_expert, moe_b_route_expert,
           moe_w_gate, moe_w_up, moe_w_down, ple_w_up, ple_w_gate, final_norm_g):
    batch, seq, d = x.shape
    assert d == D_MODEL and seq % T_GLA == 0
    n = batch * seq
    assert n % T_ROUTE == 0 and n % T_ROW == 0 and n % T_SGU == 0
    depth = p.shape[0]
    assert depth == DEPTH
    na, nb = gla_w_in.shape[0], sgu_w_in.shape[0]

    gla_win = jnp.pad(gla_w_in, ((0, 0), (0, 0), (0, GLA_A_PAD - GLA_RANK)))
    gla_wgu = jnp.pad(gla_w_gate_up, ((0, 0), (0, GLA_A_PAD - GLA_RANK), (0, 0)))
    gla_bg = gla_b_gate.reshape(na, 1, GLA_DK)
    gla_gh = gla_g_head.reshape(na, 1, GLA_DV)
    sgu_lng = sgu_ln_g.reshape(nb, 1, SGU_WIDTH)
    sgu_lnb = sgu_ln_b.reshape(nb, 1, SGU_WIDTH)
    sgu_bs = jnp.repeat(jnp.transpose(sgu_b_s, (0, 2, 1)), SGU_GC, axis=-1)
    g_mix = norm_mix_g.reshape(depth, 1, D_MODEL)
    g_ffn = norm_ffn_g.reshape(depth, 1, D_MODEL)
    g_ple = norm_ple_g.reshape(depth, 1, D_MODEL)
    g_fin = final_norm_g.reshape(1, D_MODEL)
    w_route = jnp.concatenate(
        [moe_w_route_group, moe_w_route_expert.reshape(depth, D_MODEL, MOE_EXPERTS)], axis=-1)
    b_route = jnp.concatenate(
        [moe_b_route_group, moe_b_route_expert.reshape(depth, MOE_EXPERTS)], axis=-1)
    n_logits = MOE_GROUPS + MOE_EXPERTS
    wr = jnp.pad(w_route, ((0, 0), (0, 0), (0, LANES - n_logits))).astype(BF16)
    br = jnp.pad(b_route, ((0, 0), (0, LANES - n_logits))).reshape(depth, 1, LANES)
    wrt = jnp.pad(jnp.transpose(w_route, (0, 2, 1)),
                  ((0, 0), (0, ROUTE_ROWS - n_logits), (0, 0))).astype(BF16)
    brt = jnp.broadcast_to(
        jnp.pad(b_route, ((0, 0), (0, ROUTE_ROWS - n_logits)))[:, :, None],
        (depth, ROUTE_ROWS, T_ROUTE))
    triu = jnp.triu(jnp.ones((T_ROUTE, T_ROUTE), BF16), k=1)
    wg = moe_w_gate.reshape(depth * MOE_EXPERTS, D_MODEL, MOE_FF)
    wu = moe_w_up.reshape(depth * MOE_EXPERTS, D_MODEL, MOE_FF)
    wd = moe_w_down.reshape(depth * MOE_EXPERTS, MOE_FF, D_MODEL)
    p2 = p.reshape(depth * n, PLE_DIM)

    h = x.reshape(n, D_MODEL)
    for i in range(depth):
        j = i // 2
        if i % 2 == 0:
            h = _gla_layer(h, batch, seq, j, i, g_mix, gla_win, gla_wgu, gla_bg, gla_gh, gla_w_out)
        else:
            h = _sgu_layer(h, j, i, g_mix, sgu_w_in, sgu_lng, sgu_lnb, sgu_w_s, sgu_bs, sgu_w_out)
        route, counts = _router(h, i, g_ffn, wrt, brt, triu)
        dest, bstart, bunits, zs = _plan(route, counts, n)
        dest3 = dest.reshape(n // T_ROW, 1, T_ROW)
        xs = _dispatch(h, dest3, zs)
        ys = _experts(xs, i, bstart, bunits, g_ffn, wr, br, wg, wu, wd)
        h = _combine(h, dest3, p2, ys, i, g_ple, ple_w_up, ple_w_gate, g_fin)
    return h.reshape(batch, seq, D_MODEL)
```

```python
import functools

import jax
import jax.numpy as jnp
from jax import lax
from jax.experimental import pallas as pl
from jax.experimental.pallas import tpu as pltpu

F32 = jnp.float32
BF16 = jnp.bfloat16

D_MODEL = 1024
DEPTH = 4
GLA_HEADS = 4
GLA_DK = 512
GLA_DV = 1024
GLA_HK = GLA_DK // GLA_HEADS
GLA_HV = GLA_DV // GLA_HEADS
GLA_RANK = 16
GLA_TAU = 16.0
GLA_CHUNK = 64
SGU_WIDTH = 2048
SGU_GROUPS = 8
SGU_GC = SGU_WIDTH // SGU_GROUPS
SGU_CHUNK = 128
MOE_GROUPS = 4
MOE_EPG = 4
MOE_EXPERTS = MOE_GROUPS * MOE_EPG
MOE_FF = 512
PLE_DIM = 256
EPS = 1e-6

LANES = 128
ROW_SUB = D_MODEL // LANES
GLA_A_PAD = LANES
GLA_ZW = 2 * GLA_DK + 2 * GLA_DV + GLA_A_PAD
ROUTE_ROWS = 32
PAIRS = ((0, 1), (0, 2), (1, 2), (1, 3), (2, 3), (0, 3))
N_PAIRS = len(PAIRS)
N_BUCKETS = MOE_GROUPS * N_PAIRS

T_GLA = 256
T_SGU = 256
T_ROUTE = 512
T_ROW = 256
T_MOE = 256
MOE_UNIT = 128
N_ZERO_FILLS = 2 * N_BUCKETS
DMA_UNROLL = 8
CAST_ROWS = 256
VMEM_LIMIT = 56 * 1024 * 1024


def _rms(x, g):
    ms = jnp.mean(x * x, axis=-1, keepdims=True)
    return x * lax.rsqrt(ms + EPS) * g


def _dot(a, b):
    return jnp.dot(a, b, preferred_element_type=F32)


def _dot_nt(a, b):
    return lax.dot_general(a, b, (((1,), (1,)), ((), ())), preferred_element_type=F32)


def _dot_tn(a, b):
    return lax.dot_general(a, b, (((0,), (0,)), ((), ())), preferred_element_type=F32)


def _params(sem):
    return pltpu.CompilerParams(dimension_semantics=sem, vmem_limit_bytes=VMEM_LIMIT)


def _resident(block_shape, index_map):
    return pl.BlockSpec(block_shape, index_map, pipeline_mode=pl.Buffered(1))


def _cast_weight(dst_ref, src_ref):
    rows = src_ref.shape[0]
    for r in range(0, rows, CAST_ROWS):
        dst_ref[r:r + CAST_ROWS, :] = src_ref[r:r + CAST_ROWS, :].astype(BF16)


def _store_row_major(dst_ref, x):
    m = x.shape[0]
    for c in range(ROW_SUB):
        dst_ref[pl.ds(c, m, stride=ROW_SUB), :] = x[:, c * LANES:(c + 1) * LANES]


def _load_row_major(src_ref, m):
    return jnp.concatenate(
        [src_ref[pl.ds(c, m, stride=ROW_SUB), :] for c in range(ROW_SUB)], axis=-1)


def _row_block(ref, row, rows):
    return ref.at[pl.ds(pl.multiple_of(row * ROW_SUB, ROW_SUB), rows * ROW_SUB), :]


def _gla_kernel(h_ref, gmix_ref, win_ref, wgu_ref, bg_ref, ghead_ref, wout_ref,
                o_ref, winb_ref, woutb_ref, st_ref, oacc_ref, y_ref):
    @pl.when((pl.program_id(0) == 0) & (pl.program_id(1) == 0))
    def _():
        _cast_weight(winb_ref, win_ref)
        _cast_weight(woutb_ref, wout_ref)

    @pl.when(pl.program_id(1) == 0)
    def _():
        st_ref[...] = jnp.zeros_like(st_ref)

    h = h_ref[...]
    hn = _rms(h, gmix_ref[...]).astype(BF16)
    z = _dot(hn, winb_ref[...])
    a = z[:, 2 * GLA_DK + 2 * GLA_DV:].astype(BF16)
    pre = _dot(a, wgu_ref[...].astype(BF16)) + bg_ref[...]
    la = (jnp.minimum(pre, 0.0) - jnp.log1p(jnp.exp(-jnp.abs(pre)))) * (1.0 / GLA_TAU)

    C = GLA_CHUNK
    ri = lax.broadcasted_iota(jnp.int32, (C, C), 0)
    ci = lax.broadcasted_iota(jnp.int32, (C, C), 1)
    causal = ri >= ci
    tri = causal.astype(BF16)

    for c in range(T_GLA // C):
        rows = slice(c * C, (c + 1) * C)
        la_c = la[rows]
        hi = la_c.astype(BF16)
        lo = (la_c - hi.astype(F32)).astype(BF16)
        bc = _dot(tri, hi) + _dot(tri, lo)
        bend = bc[C - 1:C, :]
        kc = z[rows, GLA_DK:2 * GLA_DK]
        qd = (z[rows, 0:GLA_DK] * (GLA_HK ** -0.5) * jnp.exp(bc)).astype(BF16)
        kd = (kc * jnp.exp(-bc)).astype(BF16)
        ke = (kc * jnp.exp(bend - bc)).astype(BF16)
        vc = z[rows, 2 * GLA_DK:2 * GLA_DK + GLA_DV].astype(BF16)
        dec = jnp.exp(bend)
        for hd in range(GLA_HEADS):
            ks = slice(hd * GLA_HK, (hd + 1) * GLA_HK)
            vs = slice(hd * GLA_HV, (hd + 1) * GLA_HV)
            att = jnp.where(causal, _dot_nt(qd[:, ks], kd[:, ks]), 0.0).astype(BF16)
            st = st_ref[hd]
            o = _dot(att, vc[:, vs]) + _dot_nt(qd[:, ks], st.astype(BF16))
            st_ref[hd] = st * dec[:, ks] + _dot_tn(vc[:, vs], ke[:, ks])
            oacc_ref[rows, vs] = o

    r0 = 2 * GLA_DK + GLA_DV
    for hd in range(GLA_HEADS):
        vs = slice(hd * GLA_HV, (hd + 1) * GLA_HV)
        on = _rms(oacc_ref[:, vs], ghead_ref[:, vs])
        r = z[:, r0 + hd * GLA_HV:r0 + (hd + 1) * GLA_HV]
        y_ref[:, vs] = (jax.nn.silu(r) * on).astype(BF16)
    o_ref[...] = h + _dot(y_ref[...], woutb_ref[...])


def _gla_layer(h, batch, seq, j, i, gmix, win, wgu, bg, ghead, wout):
    n = h.shape[0]
    ns = seq // T_GLA
    return pl.pallas_call(
        _gla_kernel,
        grid=(batch, ns),
        in_specs=[
            pl.BlockSpec((T_GLA, D_MODEL), lambda b, s: (b * ns + s, 0)),
            pl.BlockSpec((None, 1, D_MODEL), lambda b, s: (i, 0, 0)),
            _resident((None, D_MODEL, GLA_ZW), lambda b, s: (j, 0, 0)),
            pl.BlockSpec((None, GLA_A_PAD, GLA_DK), lambda b, s: (j, 0, 0)),
            pl.BlockSpec((None, 1, GLA_DK), lambda b, s: (j, 0, 0)),
            pl.BlockSpec((None, 1, GLA_DV), lambda b, s: (j, 0, 0)),
            _resident((None, GLA_DV, D_MODEL), lambda b, s: (j, 0, 0)),
        ],
        out_specs=pl.BlockSpec((T_GLA, D_MODEL), lambda b, s: (b * ns + s, 0)),
        out_shape=jax.ShapeDtypeStruct((n, D_MODEL), F32),
        scratch_shapes=[
            pltpu.VMEM((D_MODEL, GLA_ZW), BF16),
            pltpu.VMEM((GLA_DV, D_MODEL), BF16),
            pltpu.VMEM((GLA_HEADS, GLA_HV, GLA_HK), F32),
            pltpu.VMEM((T_GLA, GLA_DV), F32),
            pltpu.VMEM((T_GLA, GLA_DV), BF16),
        ],
        compiler_params=_params(("arbitrary", "arbitrary")),
        name="gla_mixer",
    )(h, gmix, win, wgu, bg, ghead, wout)


def _sgu_kernel(h_ref, gmix_ref, win_ref, lng_ref, lnb_ref, ws_ref, bs_ref, wout_ref,
                o_ref, winb_ref, woutb_ref, wsb_ref, us_ref):
    C = SGU_CHUNK

    @pl.when(pl.program_id(0) == 0)
    def _():
        _cast_weight(winb_ref, win_ref)
        _cast_weight(woutb_ref, wout_ref)
        ri = lax.broadcasted_iota(jnp.int32, (C, C), 0)
        ci = lax.broadcasted_iota(jnp.int32, (C, C), 1)
        for g in range(SGU_GROUPS):
            wsb_ref[g] = jnp.where(ri >= ci, ws_ref[g], 0.0).astype(BF16)

    h = h_ref[...]
    hn = _rms(h, gmix_ref[...]).astype(BF16)
    z = jax.nn.gelu(_dot(hn, winb_ref[...]), approximate=True)
    u = z[:, :SGU_WIDTH]
    v = z[:, SGU_WIDTH:]
    xc = v - jnp.mean(v, axis=-1, keepdims=True)
    vn = xc * lax.rsqrt(jnp.mean(xc * xc, axis=-1, keepdims=True) + EPS)
    vn = (vn * lng_ref[...] + lnb_ref[...]).astype(BF16)

    for g in range(SGU_GROUPS):
        cols = slice(g * SGU_GC, (g + 1) * SGU_GC)
        wc = wsb_ref[g]
        for c in range(T_SGU // C):
            rows = slice(c * C, (c + 1) * C)
            s = _dot(wc, vn[rows, cols]) + bs_ref[:, cols]
            us_ref[rows, cols] = (u[rows, cols] * s).astype(BF16)
    o_ref[...] = h + _dot(us_ref[...], woutb_ref[...])


def _sgu_layer(h, j, i, gmix, win, lng, lnb, ws, bs_full, wout):
    n = h.shape[0]
    return pl.pallas_call(
        _sgu_kernel,
        grid=(n // T_SGU,),
        in_specs=[
            pl.BlockSpec((T_SGU, D_MODEL), lambda t: (t, 0)),
            pl.BlockSpec((None, 1, D_MODEL), lambda t: (i, 0, 0)),
            _resident((None, D_MODEL, 2 * SGU_WIDTH), lambda t: (j, 0, 0)),
            pl.BlockSpec((None, 1, SGU_WIDTH), lambda t: (j, 0, 0)),
            pl.BlockSpec((None, 1, SGU_WIDTH), lambda t: (j, 0, 0)),
            _resident((None, SGU_GROUPS, SGU_CHUNK, SGU_CHUNK), lambda t: (j, 0, 0, 0)),
            _resident((None, SGU_CHUNK, SGU_WIDTH), lambda t: (j, 0, 0)),
            _resident((None, SGU_WIDTH, D_MODEL), lambda t: (j, 0, 0)),
        ],
        out_specs=pl.BlockSpec((T_SGU, D_MODEL), lambda t: (t, 0)),
        out_shape=jax.ShapeDtypeStruct((n, D_MODEL), F32),
        scratch_shapes=[
            pltpu.VMEM((D_MODEL, 2 * SGU_WIDTH), BF16),
            pltpu.VMEM((SGU_WIDTH, D_MODEL), BF16),
            pltpu.VMEM((SGU_GROUPS, SGU_CHUNK, SGU_CHUNK), BF16),
            pltpu.VMEM((T_SGU, SGU_WIDTH), BF16),
        ],
        compiler_params=_params(("arbitrary",)),
        name="sgu_mixer",
    )(h, gmix, win, lng, lnb, ws, bs_full, wout)


def _first_argmax(rows):
    best = rows[0]
    idx = jnp.zeros(best.shape, jnp.int32)
    for j in range(1, len(rows)):
        upd = rows[j] > best
        idx = jnp.where(upd, j, idx)
        best = jnp.where(upd, rows[j], best)
    return idx, best


def _router_kernel(h_ref, gffn_ref, wrt_ref, brt_ref, triu_ref, route_ref, counts_ref, cnt_ref):
    @pl.when(pl.program_id(0) == 0)
    def _():
        cnt_ref[...] = jnp.zeros_like(cnt_ref)

    hb = _rms(h_ref[...], gffn_ref[...]).astype(BF16)
    lg = _dot_nt(wrt_ref[...], hb) + brt_ref[...]
    row = lambda j: lg[j:j + 1, :]
    g, _ = _first_argmax([row(j) for j in range(MOE_GROUPS)])
    ex = []
    for e in range(MOE_EPG):
        v = row(MOE_GROUPS + e)
        for gg in range(1, MOE_GROUPS):
            v = jnp.where(g == gg, row(MOE_GROUPS + gg * MOE_EPG + e), v)
        ex.append(v)
    i1, _ = _first_argmax(ex)
    neg = jnp.full(ex[0].shape, -jnp.inf, F32)
    i2, _ = _first_argmax([jnp.where(i1 == e, neg, ex[e]) for e in range(MOE_EPG)])
    lo = jnp.minimum(i1, i2)
    hi = jnp.maximum(i1, i2)
    pair = jnp.zeros(lo.shape, jnp.int32)
    for p, (a, b) in enumerate(PAIRS):
        pair = jnp.where((lo == a) & (hi == b), p, pair)
    bucket = g * N_PAIRS + pair

    t = bucket.shape[1]
    onehot = (lax.broadcasted_iota(jnp.int32, (ROUTE_ROWS, t), 0) == bucket).astype(F32)
    before = _dot(onehot.astype(BF16), triu_ref[...])
    rank = jnp.sum(onehot * (before + cnt_ref[:, 0:1]), axis=0, keepdims=True).astype(jnp.int32)
    cnt_ref[...] = cnt_ref[...] + jnp.sum(onehot, axis=1, keepdims=True)
    counts_ref[...] = cnt_ref[...]
    r8 = lax.broadcasted_iota(jnp.int32, (8, t), 0)
    route_ref[...] = jnp.where(r8 == 0, bucket, jnp.where(r8 == 1, rank, 0))


def _router(h, i, gffn, wrt, brt, triu):
    n = h.shape[0]
    return pl.pallas_call(
        _router_kernel,
        grid=(n // T_ROUTE,),
        in_specs=[
            pl.BlockSpec((T_ROUTE, D_MODEL), lambda t: (t, 0)),
            pl.BlockSpec((None, 1, D_MODEL), lambda t: (i, 0, 0)),
            pl.BlockSpec((None, ROUTE_ROWS, D_MODEL), lambda t: (i, 0, 0)),
            pl.BlockSpec((None, ROUTE_ROWS, T_ROUTE), lambda t: (i, 0, 0)),
            pl.BlockSpec((T_ROUTE, T_ROUTE), lambda t: (0, 0)),
        ],
        out_specs=[
            pl.BlockSpec((8, T_ROUTE), lambda t: (0, t)),
            pl.BlockSpec((ROUTE_ROWS, LANES), lambda t: (0, 0)),
        ],
        out_shape=[
            jax.ShapeDtypeStruct((8, n), jnp.int32),
            jax.ShapeDtypeStruct((ROUTE_ROWS, LANES), F32),
        ],
        scratch_shapes=[pltpu.VMEM((ROUTE_ROWS, LANES), F32)],
        compiler_params=_params(("arbitrary",)),
        name="moe_router",
    )(h, gffn, wrt, brt, triu)


def _dispatch_kernel(zs_ref, dest_ref, h_ref, xs_ref, zero_ref, rows_ref, zsem, sem):
    @pl.when(pl.program_id(0) == 0)
    def _():
        zero_ref[...] = jnp.zeros_like(zero_ref)

        def fill(k):
            return pltpu.make_async_copy(zero_ref, _row_block(xs_ref, zs_ref[k], MOE_UNIT), zsem)

        for k in range(N_ZERO_FILLS):
            pl.when(zs_ref[N_ZERO_FILLS + k] != 0)(lambda k=k: fill(k).start())
        for k in range(N_ZERO_FILLS):
            pl.when(zs_ref[N_ZERO_FILLS + k] != 0)(lambda k=k: fill(k).wait())

    _store_row_major(rows_ref, h_ref[...])

    def issue(k, carry):
        base = pl.multiple_of(k * DMA_UNROLL, DMA_UNROLL)
        for u in range(DMA_UNROLL):
            d = dest_ref[0, 0, base + u]
            pltpu.make_async_copy(_row_block(rows_ref, base + u, 1),
                                  _row_block(xs_ref, d, 1), sem).start(priority=u % 2)
        return carry

    lax.fori_loop(0, T_ROW // DMA_UNROLL, issue, 0)
    pltpu.make_async_copy(rows_ref, _row_block(xs_ref, 0, T_ROW), sem).wait()


def _dispatch(h, dest3, zs):
    n = h.shape[0]
    grid_spec = pltpu.PrefetchScalarGridSpec(
        num_scalar_prefetch=1,
        grid=(n // T_ROW,),
        in_specs=[
            pl.BlockSpec((1, 1, T_ROW), lambda t, zs: (t, 0, 0), memory_space=pltpu.SMEM),
            pl.BlockSpec((T_ROW, D_MODEL), lambda t, zs: (t, 0)),
        ],
        out_specs=pl.BlockSpec(memory_space=pl.ANY),
        scratch_shapes=[
            pltpu.VMEM((MOE_UNIT * ROW_SUB, LANES), F32),
            pltpu.VMEM((T_ROW * ROW_SUB, LANES), F32),
            pltpu.SemaphoreType.DMA(()),
            pltpu.SemaphoreType.DMA(()),
        ],
    )
    return pl.pallas_call(
        _dispatch_kernel,
        grid_spec=grid_spec,
        out_shape=jax.ShapeDtypeStruct((_sorted_rows(n) * ROW_SUB, LANES), F32),
        compiler_params=_params(("arbitrary",)),
        name="moe_dispatch",
    )(zs, dest3, h)


def _experts_kernel(bg0_ref, bnt_ref, bg_ref, bea_ref, beb_ref, trow_ref, tunits_ref, ntot_ref,
                    xs_ref, gffn_ref, wr_ref, br_ref,
                    wga_ref, wua_ref, wda_ref, wgb_ref, wub_ref, wdb_ref,
                    ys_ref,
                    wgab_ref, wuab_ref, wdab_ref, wgbb_ref, wubb_ref, wdbb_ref,
                    xbuf_ref, ybuf_ref, isem, osem):
    b = pl.program_id(0)
    pair = b % N_PAIRS

    def cast_a():
        _cast_weight(wgab_ref, wga_ref)
        _cast_weight(wuab_ref, wua_ref)
        _cast_weight(wdab_ref, wda_ref)

    def cast_b():
        _cast_weight(wgbb_ref, wgb_ref)
        _cast_weight(wubb_ref, wub_ref)
        _cast_weight(wdbb_ref, wdb_ref)

    a_changes = [q for q in range(N_PAIRS) if q == 0 or PAIRS[q][0] != PAIRS[q - 1][0]]
    b_changes = [q for q in range(N_PAIRS) if q == 0 or PAIRS[q][1] != PAIRS[q - 1][1]]
    pl.when(functools.reduce(jnp.logical_or, [pair == q for q in a_changes]))(cast_a)
    pl.when(functools.reduce(jnp.logical_or, [pair == q for q in b_changes]))(cast_b)

    g0 = bg0_ref[b]
    ntot = ntot_ref[0]
    grp = bg_ref[b]
    ea = bea_ref[b]
    eb = beb_ref[b]

    def compute(x32):
        x = _rms(x32, gffn_ref[...]).astype(BF16)
        lg = _dot(x, wr_ref[...]) + br_ref[...]
        lane = lax.broadcasted_iota(jnp.int32, lg.shape, 1)
        glog = jnp.where(lane < MOE_GROUPS, lg, -jnp.inf)
        pe = jnp.exp(glog - jnp.max(glog, axis=-1, keepdims=True))
        g_w = (jnp.sum(jnp.where(lane == grp, pe, 0.0), axis=-1, keepdims=True)
               / jnp.sum(pe, axis=-1, keepdims=True))
        la = jnp.sum(jnp.where(lane == MOE_GROUPS + ea, lg, 0.0), axis=-1, keepdims=True)
        lb = jnp.sum(jnp.where(lane == MOE_GROUPS + eb, lg, 0.0), axis=-1, keepdims=True)
        m = jnp.maximum(la, lb)
        pa = jnp.exp(la - m)
        pb = jnp.exp(lb - m)
        ca = pa / (pa + pb) * g_w
        cb = pb / (pa + pb) * g_w

        def expert(wg_ref, wu_ref, c):
            return (jax.nn.silu(_dot(x, wg_ref[...])) * _dot(x, wu_ref[...]) * c).astype(BF16)

        return (_dot(expert(wgab_ref, wuab_ref, ca), wdab_ref[...])
                + _dot(expert(wgbb_ref, wubb_ref, cb), wdbb_ref[...]))

    def load(g, rows):
        return pltpu.make_async_copy(_row_block(xs_ref, trow_ref[g], rows),
                                     _row_block(xbuf_ref.at[g % 2], 0, rows), isem.at[g % 2])

    def store(g, rows):
        return pltpu.make_async_copy(_row_block(ybuf_ref.at[g % 2], 0, rows),
                                     _row_block(ys_ref, trow_ref[g], rows), osem.at[g % 2])

    def by_size(g, fn):
        pl.when(tunits_ref[g] == T_MOE // MOE_UNIT)(lambda: fn(T_MOE))
        pl.when(tunits_ref[g] == 1)(lambda: fn(MOE_UNIT))

    @pl.when((b == 0) & (ntot > 0))
    def _():
        by_size(0, lambda rows: load(0, rows).start())

    def body(j, carry):
        g = g0 + j

        @pl.when(g + 1 < ntot)
        def _():
            by_size(g + 1, lambda rows: load(g + 1, rows).start())

        @pl.when(g >= 2)
        def _():
            by_size(g - 2, lambda rows: store(g - 2, rows).wait())

        def run(rows):
            load(g, rows).wait()
            y = compute(_load_row_major(xbuf_ref.at[g % 2], rows))
            _store_row_major(ybuf_ref.at[g % 2], y)
            store(g, rows).start()

        by_size(g, run)
        return carry

    lax.fori_loop(0, bnt_ref[b], body, 0)

    @pl.when(b == N_BUCKETS - 1)
    def _():
        for back in (2, 1):
            @pl.when(ntot >= back)
            def _():
                by_size(ntot - back, lambda rows: store(ntot - back, rows).wait())


def _experts(xs, i, tiles, gffn, wr, br, wg, wu, wd):
    bg0, bnt, trow, tunits, ntot = tiles
    e0 = i * MOE_EXPERTS
    bg = jnp.asarray([q // N_PAIRS for q in range(N_BUCKETS)], jnp.int32)
    bea = jnp.asarray([(q // N_PAIRS) * MOE_EPG + PAIRS[q % N_PAIRS][0] for q in range(N_BUCKETS)],
                      jnp.int32)
    beb = jnp.asarray([(q // N_PAIRS) * MOE_EPG + PAIRS[q % N_PAIRS][1] for q in range(N_BUCKETS)],
                      jnp.int32)
    ea_map = lambda b, g0, nt, g, ea, eb, *_: (e0 + ea[b], 0, 0)
    eb_map = lambda b, g0, nt, g, ea, eb, *_: (e0 + eb[b], 0, 0)
    up_spec = lambda m: pl.BlockSpec((None, D_MODEL, MOE_FF), m)
    down_spec = lambda m: pl.BlockSpec((None, MOE_FF, D_MODEL), m)
    grid_spec = pltpu.PrefetchScalarGridSpec(
        num_scalar_prefetch=8,
        grid=(N_BUCKETS,),
        in_specs=[
            pl.BlockSpec(memory_space=pl.ANY),
            pl.BlockSpec((None, 1, D_MODEL), lambda b, *_: (i, 0, 0)),
            pl.BlockSpec((None, D_MODEL, LANES), lambda b, *_: (i, 0, 0)),
            pl.BlockSpec((None, 1, LANES), lambda b, *_: (i, 0, 0)),
            up_spec(ea_map), up_spec(ea_map), down_spec(ea_map),
            up_spec(eb_map), up_spec(eb_map), down_spec(eb_map),
        ],
        out_specs=pl.BlockSpec(memory_space=pl.ANY),
        scratch_shapes=[
            pltpu.VMEM((D_MODEL, MOE_FF), BF16), pltpu.VMEM((D_MODEL, MOE_FF), BF16),
            pltpu.VMEM((MOE_FF, D_MODEL), BF16),
            pltpu.VMEM((D_MODEL, MOE_FF), BF16), pltpu.VMEM((D_MODEL, MOE_FF), BF16),
            pltpu.VMEM((MOE_FF, D_MODEL), BF16),
            pltpu.VMEM((2, T_MOE * ROW_SUB, LANES), F32),
            pltpu.VMEM((2, T_MOE * ROW_SUB, LANES), F32),
            pltpu.SemaphoreType.DMA((2,)),
            pltpu.SemaphoreType.DMA((2,)),
        ],
    )
    return pl.pallas_call(
        _experts_kernel,
        grid_spec=grid_spec,
        out_shape=jax.ShapeDtypeStruct(xs.shape, F32),
        input_output_aliases={8: 0},
        compiler_params=_params(("arbitrary",)),
        name="moe_experts",
    )(bg0, bnt, bg, bea, beb, trow, tunits, ntot, xs, gffn, wr, br, wg, wu, wd, wg, wu, wd)


def _combine_kernel(dcur_ref, dnext_ref, h_ref, p_ref, ys_ref, gple_ref, wup_ref, wgate_ref,
                    gfin_ref, o_ref, wupb_ref, wgateb_ref, ybuf_ref, sem, *, final):
    i = pl.program_id(0)
    n = pl.num_programs(0)

    def issue(dref, slot):
        def body(k, carry):
            base = pl.multiple_of(k * DMA_UNROLL, DMA_UNROLL)
            for u in range(DMA_UNROLL):
                d = dref[0, 0, base + u]
                pltpu.make_async_copy(_row_block(ys_ref, d, 1),
                                      _row_block(ybuf_ref.at[slot], base + u, 1),
                                      sem.at[slot]).start()
            return carry
        lax.fori_loop(0, T_ROW // DMA_UNROLL, body, 0)

    @pl.when(i == 0)
    def _():
        issue(dcur_ref, 0)
        _cast_weight(wupb_ref, wup_ref)
        _cast_weight(wgateb_ref, wgate_ref)

    @pl.when(i + 1 < n)
    def _():
        issue(dnext_ref, (i + 1) % 2)

    slot = i % 2
    pltpu.make_async_copy(_row_block(ys_ref, 0, T_ROW), ybuf_ref.at[slot], sem.at[slot]).wait()
    h2 = h_ref[...] + _load_row_major(ybuf_ref.at[slot], T_ROW)
    pn = _rms(h2, gple_ref[...]).astype(BF16)
    gate = jax.nn.sigmoid(_dot(pn, wgateb_ref[...]))
    up = _dot(p_ref[...].astype(BF16), wupb_ref[...])
    out = h2 + up * gate
    if final:
        out = _rms(out, gfin_ref[...])
    o_ref[...] = out


def _combine(h, dest3, p2, ys, i, gple, wup, wgate, gfin):
    n = h.shape[0]
    nt = n // T_ROW
    return pl.pallas_call(
        functools.partial(_combine_kernel, final=(i == DEPTH - 1)),
        grid=(nt,),
        in_specs=[
            pl.BlockSpec((1, 1, T_ROW), lambda t: (t, 0, 0), memory_space=pltpu.SMEM),
            pl.BlockSpec((1, 1, T_ROW), lambda t: (jnp.minimum(t + 1, nt - 1), 0, 0),
                         memory_space=pltpu.SMEM),
            pl.BlockSpec((T_ROW, D_MODEL), lambda t: (t, 0)),
            pl.BlockSpec((T_ROW, PLE_DIM), lambda t: (i * nt + t, 0)),
            pl.BlockSpec(memory_space=pl.ANY),
            pl.BlockSpec((None, 1, D_MODEL), lambda t: (i, 0, 0)),
            _resident((None, PLE_DIM, D_MODEL), lambda t: (i, 0, 0)),
            _resident((None, D_MODEL, D_MODEL), lambda t: (i, 0, 0)),
            pl.BlockSpec((1, D_MODEL), lambda t: (0, 0)),
        ],
        out_specs=pl.BlockSpec((T_ROW, D_MODEL), lambda t: (t, 0)),
        out_shape=jax.ShapeDtypeStruct((n, D_MODEL), F32),
        scratch_shapes=[
            pltpu.VMEM((PLE_DIM, D_MODEL), BF16),
            pltpu.VMEM((D_MODEL, D_MODEL), BF16),
            pltpu.VMEM((2, T_ROW * ROW_SUB, LANES), F32),
            pltpu.SemaphoreType.DMA((2,)),
        ],
        compiler_params=_params(("arbitrary",)),
        name="moe_combine_ple",
    )(dest3, dest3, h, p2, ys, gple, wup, wgate, gfin)


def _sorted_rows(n):
    return n + N_BUCKETS * MOE_UNIT


def _max_tiles(n):
    return n // T_MOE + N_BUCKETS


def _plan(route, counts, n):
    bucket = route[0]
    rank = route[1]
    cnt = counts[:N_BUCKETS, 0].astype(jnp.int32)
    units = (cnt + MOE_UNIT - 1) // MOE_UNIT
    ends = jnp.cumsum(units * MOE_UNIT)
    starts = ends - units * MOE_UNIT
    dest = starts[bucket] + rank
    tail_rows = ends[-1] + jnp.arange(N_BUCKETS, dtype=jnp.int32) * MOE_UNIT
    fill_on = jnp.concatenate([cnt % MOE_UNIT != 0, tail_rows < _sorted_rows(n)])
    fill_row = jnp.where(fill_on, jnp.concatenate([ends - MOE_UNIT, tail_rows]), 0)
    zs = jnp.concatenate([fill_row, fill_on.astype(jnp.int32)]).astype(jnp.int32)
    per_tile = T_MOE // MOE_UNIT
    n_full = units // per_tile
    bnt = n_full + units % per_tile
    bend = jnp.cumsum(bnt)
    bg0 = bend - bnt
    g = jnp.arange(_max_tiles(n), dtype=jnp.int32)
    tb = jnp.minimum(jnp.sum(g[:, None] >= bend[None, :], axis=1), N_BUCKETS - 1)
    j = g - bg0[tb]
    live = g < bend[-1]
    trow = jnp.where(live, starts[tb] + j * T_MOE, 0)
    tunits = jnp.where(live, jnp.where(j < n_full[tb], per_tile, 1), 0)
    i32 = lambda a: a.astype(jnp.int32)
    tiles = (i32(bg0), i32(bnt), i32(trow), i32(tunits), i32(bend[-1:]))
    return dest, zs, tiles


def kernel(x, p, gla_w_in, gla_w_gate_up, gla_b_gate, gla_g_head, gla_w_out, sgu_w_in, sgu_ln_g,
           sgu_ln_b, sgu_w_s, sgu_b_s, sgu_w_out, norm_mix_g, norm_ffn_g, norm_ple_g,
           moe_w_route_group, moe_b_route_group, moe_w_route_expert, moe_b_route_expert,
           moe_w_gate, moe_w_up, moe_w_down, ple_w_up, ple_w_gate, final_norm_g):
    batch, seq, d = x.shape
    assert d == D_MODEL and seq % T_GLA == 0
    n = batch * seq
    assert n % T_ROUTE == 0 and n % T_ROW == 0 and n % T_SGU == 0
    depth = p.shape[0]
    assert depth == DEPTH
    na, nb = gla_w_in.shape[0], sgu_w_in.shape[0]

    gla_win = jnp.pad(gla_w_in, ((0, 0), (0, 0), (0, GLA_A_PAD - GLA_RANK)))
    gla_wgu = jnp.pad(gla_w_gate_up, ((0, 0), (0, GLA_A_PAD - GLA_RANK), (0, 0)))
    gla_bg = gla_b_gate.reshape(na, 1, GLA_DK)
    gla_gh = gla_g_head.reshape(na, 1, GLA_DV)
    sgu_lng = sgu_ln_g.reshape(nb, 1, SGU_WIDTH)
    sgu_lnb = sgu_ln_b.reshape(nb, 1, SGU_WIDTH)
    sgu_bs = jnp.repeat(jnp.transpose(sgu_b_s, (0, 2, 1)), SGU_GC, axis=-1)
    g_mix = norm_mix_g.reshape(depth, 1, D_MODEL)
    g_ffn = norm_ffn_g.reshape(depth, 1, D_MODEL)
    g_ple = norm_ple_g.reshape(depth, 1, D_MODEL)
    g_fin = final_norm_g.reshape(1, D_MODEL)
    w_route = jnp.concatenate(
        [moe_w_route_group, moe_w_route_expert.reshape(depth, D_MODEL, MOE_EXPERTS)], axis=-1)
    b_route = jnp.concatenate(
        [moe_b_route_group, moe_b_route_expert.reshape(depth, MOE_EXPERTS)], axis=-1)
    n_logits = MOE_GROUPS + MOE_EXPERTS
    wr = jnp.pad(w_route, ((0, 0), (0, 0), (0, LANES - n_logits))).astype(BF16)
    br = jnp.pad(b_route, ((0, 0), (0, LANES - n_logits))).reshape(depth, 1, LANES)
    wrt = jnp.pad(jnp.transpose(w_route, (0, 2, 1)),
                  ((0, 0), (0, ROUTE_ROWS - n_logits), (0, 0))).astype(BF16)
    brt = jnp.broadcast_to(
        jnp.pad(b_route, ((0, 0), (0, ROUTE_ROWS - n_logits)))[:, :, None],
        (depth, ROUTE_ROWS, T_ROUTE))
    triu = jnp.triu(jnp.ones((T_ROUTE, T_ROUTE), BF16), k=1)
    wg = moe_w_gate.reshape(depth * MOE_EXPERTS, D_MODEL, MOE_FF)
    wu = moe_w_up.reshape(depth * MOE_EXPERTS, D_MODEL, MOE_FF)
    wd = moe_w_down.reshape(depth * MOE_EXPERTS, MOE_FF, D_MODEL)
    p2 = p.reshape(depth * n, PLE_DIM)

    h = x.reshape(n, D_MODEL)
    for i in range(depth):
        j = i // 2
        if i % 2 == 0:
            h = _gla_layer(h, batch, seq, j, i, g_mix, gla_win, gla_wgu, gla_bg, gla_gh, gla_w_out)
        else:
            h = _sgu_layer(h, j, i, g_mix, sgu_w_in, sgu_lng, sgu_lnb, sgu_w_s, sgu_bs, sgu_w_out)
        route, counts = _router(h, i, g_ffn, wrt, brt, triu)
        dest, zs, tiles = _plan(route, counts, n)
        dest3 = dest.reshape(n // T_ROW, 1, T_ROW)
        xs = _dispatch(h, dest3, zs)
        ys = _experts(xs, i, tiles, g_ffn, wr, br, wg, wu, wd)
        h = _combine(h, dest3, p2, ys, i, g_ple, ple_w_up, ple_w_gate, g_fin)
    return h.reshape(batch, seq, D_MODEL)
```

```python
import functools

import jax
import jax.numpy as jnp
from jax import lax
from jax.experimental import pallas as pl
from jax.experimental.pallas import tpu as pltpu

F32 = jnp.float32
BF16 = jnp.bfloat16

D_MODEL = 1024
DEPTH = 4
GLA_HEADS = 4
GLA_DK = 512
GLA_DV = 1024
GLA_HK = GLA_DK // GLA_HEADS
GLA_HV = GLA_DV // GLA_HEADS
GLA_RANK = 16
GLA_TAU = 16.0
GLA_CHUNK = 64
SGU_WIDTH = 2048
SGU_GROUPS = 8
SGU_GC = SGU_WIDTH // SGU_GROUPS
SGU_CHUNK = 128
MOE_GROUPS = 4
MOE_EPG = 4
MOE_EXPERTS = MOE_GROUPS * MOE_EPG
MOE_FF = 512
PLE_DIM = 256
EPS = 1e-6

LANES = 128
ROW_SUB = D_MODEL // LANES
GLA_A_PAD = LANES
GLA_ZW = 2 * GLA_DK + 2 * GLA_DV + GLA_A_PAD
ROUTE_ROWS = 32
PAIRS = ((0, 1), (0, 2), (1, 2), (1, 3), (2, 3), (0, 3))
N_PAIRS = len(PAIRS)
N_BUCKETS = MOE_GROUPS * N_PAIRS

T_MIX = 256
SGU_VBLK = 512
T_ROUTE = 512
T_ROW = T_MIX
T_MOE = 256
MOE_UNIT = 128
N_ZERO_FILLS = 2 * N_BUCKETS
DMA_UNROLL = 8
CAST_ROWS = 256
STAGE_ROWS = 64
VMEM_LIMIT = 56 * 1024 * 1024


def _rms(x, g):
    ms = jnp.mean(x * x, axis=-1, keepdims=True)
    return x * lax.rsqrt(ms + EPS) * g


def _dot(a, b):
    return jnp.dot(a, b, preferred_element_type=F32)


def _dot_nt(a, b):
    return lax.dot_general(a, b, (((1,), (1,)), ((), ())), preferred_element_type=F32)


def _dot_tn(a, b):
    return lax.dot_general(a, b, (((0,), (0,)), ((), ())), preferred_element_type=F32)


def _params(sem):
    return pltpu.CompilerParams(dimension_semantics=sem, vmem_limit_bytes=VMEM_LIMIT)


def _resident(block_shape, index_map):
    return pl.BlockSpec(block_shape, index_map, pipeline_mode=pl.Buffered(1))


def _cast_weight(dst_ref, src_ref):
    rows = src_ref.shape[0]
    for r in range(0, rows, CAST_ROWS):
        dst_ref[r:r + CAST_ROWS, :] = src_ref[r:r + CAST_ROWS, :].astype(BF16)


def _store_row_major(dst_ref, x):
    m = x.shape[0]
    for c in range(ROW_SUB):
        dst_ref[pl.ds(c, m, stride=ROW_SUB), :] = x[:, c * LANES:(c + 1) * LANES]


def _load_row_major(src_ref, m):
    return jnp.concatenate(
        [src_ref[pl.ds(c, m, stride=ROW_SUB), :] for c in range(ROW_SUB)], axis=-1)


def _row_block(ref, row, rows):
    start = row * ROW_SUB
    if not isinstance(start, int):
        start = pl.multiple_of(start, ROW_SUB)
    return ref.at[pl.ds(start, rows * ROW_SUB), :]


def _stage_weight(dst_ref, w_hbm_ref, stage_ref, sem):
    rows, ch = dst_ref.shape[0], stage_ref.shape[1]

    def chunk(k):
        return pltpu.make_async_copy(w_hbm_ref.at[pl.ds(k * ch, ch), :], stage_ref.at[k % 2],
                                     sem.at[k % 2])

    chunk(0).start()
    for k in range(rows // ch):
        if (k + 1) * ch < rows:
            chunk(k + 1).start()
        chunk(k).wait()
        dst_ref[k * ch:(k + 1) * ch, :] = stage_ref[k % 2].astype(BF16)


def _gather_start(t, dcur_ref, dnext_ref, ys_ref, ybuf_ref, sem):
    def row_copy(dref, slot, r):
        return pltpu.make_async_copy(_row_block(ys_ref, dref[0, 0, r], 1),
                                     _row_block(ybuf_ref.at[slot], r, 1), sem.at[slot])

    @pl.when(t == 0)
    def _():
        def body(k, carry):
            base = pl.multiple_of(k * DMA_UNROLL, DMA_UNROLL)
            for u in range(DMA_UNROLL):
                row_copy(dcur_ref, 0, base + u).start()
            return carry
        lax.fori_loop(0, T_ROW // DMA_UNROLL, body, 0)

    def issue_next(k, n):
        for r in range(k * T_ROW // n, (k + 1) * T_ROW // n):
            row_copy(dnext_ref, (t + 1) % 2, r).start()

    return issue_next


def _gather_wait(slot, ys_ref, ybuf_ref, sem):
    pltpu.make_async_copy(_row_block(ys_ref, 0, T_ROW), ybuf_ref.at[slot], sem.at[slot]).wait()


def _gather_finish(t, ys_ref, ybuf_ref, sem):
    _gather_wait(t % 2, ys_ref, ybuf_ref, sem)
    return _load_row_major(ybuf_ref.at[t % 2], T_ROW)


def _gather_drain(t, nt, ys_ref, ybuf_ref, sem):
    @pl.when(t == nt - 1)
    def _():
        _gather_wait((t + 1) % 2, ys_ref, ybuf_ref, sem)


def _ple(h2, p_ref, gple_ref, wupb_ref, wgateb_ref):
    pn = _rms(h2, gple_ref[...]).astype(BF16)
    gate = jax.nn.sigmoid(_dot(pn, wgateb_ref[...]))
    up = _dot(p_ref[...].astype(BF16), wupb_ref[...])
    return h2 + up * gate


def _finish_specs(i, dest_map, next_map, tile_map, p_map):
    return [
        pl.BlockSpec((1, 1, T_ROW), dest_map, memory_space=pltpu.SMEM),
        pl.BlockSpec((1, 1, T_ROW), next_map, memory_space=pltpu.SMEM),
        pl.BlockSpec((T_ROW, D_MODEL), tile_map),
        pl.BlockSpec((T_ROW, PLE_DIM), p_map),
        pl.BlockSpec(memory_space=pl.ANY),
        pl.BlockSpec((None, 1, D_MODEL), lambda t: (i, 0, 0)),
        _resident((None, PLE_DIM, D_MODEL), lambda t: (i, 0, 0)),
        _resident((None, D_MODEL, D_MODEL), lambda t: (i, 0, 0)),
    ]


_FINISH_SCRATCH = [
    pltpu.VMEM((PLE_DIM, D_MODEL), BF16),
    pltpu.VMEM((D_MODEL, D_MODEL), BF16),
    pltpu.VMEM((2, T_ROW * ROW_SUB, LANES), F32),
    pltpu.SemaphoreType.DMA((2,)),
]


def _gla_body(h, t, ns, tick, gmix_ref, wgu_ref, bg_ref, ghead_ref, winb_ref, woutb_ref,
              st_ref, oacc_ref, y_ref):
    @pl.when(t % ns == 0)
    def _():
        st_ref[...] = jnp.zeros_like(st_ref)

    hn = _rms(h, gmix_ref[...]).astype(BF16)
    z = _dot(hn, winb_ref[...])
    a = z[:, 2 * GLA_DK + 2 * GLA_DV:].astype(BF16)
    pre = _dot(a, wgu_ref[...].astype(BF16)) + bg_ref[...]
    la = (jnp.minimum(pre, 0.0) - jnp.log1p(jnp.exp(-jnp.abs(pre)))) * (1.0 / GLA_TAU)

    C = GLA_CHUNK
    ri = lax.broadcasted_iota(jnp.int32, (C, C), 0)
    ci = lax.broadcasted_iota(jnp.int32, (C, C), 1)
    causal = ri >= ci
    tri = causal.astype(BF16)

    for c in range(T_MIX // C):
        rows = slice(c * C, (c + 1) * C)
        la_c = la[rows]
        hi = la_c.astype(BF16)
        lo = (la_c - hi.astype(F32)).astype(BF16)
        bc = _dot(tri, hi) + _dot(tri, lo)
        bend = bc[C - 1:C, :]
        kc = z[rows, GLA_DK:2 * GLA_DK]
        qd = (z[rows, 0:GLA_DK] * (GLA_HK ** -0.5) * jnp.exp(bc)).astype(BF16)
        kd = (kc * jnp.exp(-bc)).astype(BF16)
        ke = (kc * jnp.exp(bend - bc)).astype(BF16)
        vc = z[rows, 2 * GLA_DK:2 * GLA_DK + GLA_DV].astype(BF16)
        dec = jnp.exp(bend)
        for hd in range(GLA_HEADS):
            ks = slice(hd * GLA_HK, (hd + 1) * GLA_HK)
            vs = slice(hd * GLA_HV, (hd + 1) * GLA_HV)
            att = jnp.where(causal, _dot_nt(qd[:, ks], kd[:, ks]), 0.0).astype(BF16)
            st = st_ref[hd]
            o = _dot(att, vc[:, vs]) + _dot_nt(qd[:, ks], st.astype(BF16))
            st_ref[hd] = st * dec[:, ks] + _dot_tn(vc[:, vs], ke[:, ks])
            oacc_ref[rows, vs] = o
            tick(c * GLA_HEADS + hd, (T_MIX // C) * GLA_HEADS)

    r0 = 2 * GLA_DK + GLA_DV
    for hd in range(GLA_HEADS):
        vs = slice(hd * GLA_HV, (hd + 1) * GLA_HV)
        on = _rms(oacc_ref[:, vs], ghead_ref[:, vs])
        r = z[:, r0 + hd * GLA_HV:r0 + (hd + 1) * GLA_HV]
        y_ref[:, vs] = (jax.nn.silu(r) * on).astype(BF16)
    return h + _dot(y_ref[...], woutb_ref[...])


def _sgu_body(h, tick, gmix_ref, lng_ref, lnb_ref, bs_ref, winb_ref, woutb_ref, wsb_ref, us_ref,
              v_ref):
    C = SGU_CHUNK
    hn = _rms(h, gmix_ref[...]).astype(BF16)

    vsum = jnp.zeros((T_MIX, 1), F32)
    for k in range(SGU_WIDTH // SGU_VBLK):
        cols = slice(k * SGU_VBLK, (k + 1) * SGU_VBLK)
        vb = jax.nn.gelu(_dot(hn, winb_ref[:, SGU_WIDTH + k * SGU_VBLK:SGU_WIDTH + (k + 1) * SGU_VBLK]),
                         approximate=True)
        v_ref[:, cols] = vb
        vsum = vsum + jnp.sum(vb, axis=-1, keepdims=True)
    mean = vsum * (1.0 / SGU_WIDTH)
    vsq = jnp.zeros((T_MIX, 1), F32)
    for k in range(SGU_WIDTH // SGU_VBLK):
        cols = slice(k * SGU_VBLK, (k + 1) * SGU_VBLK)
        xc = v_ref[:, cols] - mean
        vsq = vsq + jnp.sum(xc * xc, axis=-1, keepdims=True)
    rstd = lax.rsqrt(vsq * (1.0 / SGU_WIDTH) + EPS)

    for g in range(SGU_GROUPS):
        cols = slice(g * SGU_GC, (g + 1) * SGU_GC)
        u = jax.nn.gelu(_dot(hn, winb_ref[:, cols]), approximate=True)
        vn = (((v_ref[:, cols] - mean) * rstd) * lng_ref[:, cols] + lnb_ref[:, cols]).astype(BF16)
        wc = wsb_ref[g]
        for c in range(T_MIX // C):
            rows = slice(c * C, (c + 1) * C)
            s = _dot(wc, vn[rows]) + bs_ref[:, cols]
            us_ref[rows, cols] = (u[rows] * s).astype(BF16)
        tick(g, SGU_GROUPS)
    return h + _dot(us_ref[...], woutb_ref[...])


def _mixer_kernel(*refs, kind, fused, j, ns):
    refs = list(refs)
    t = pl.program_id(0)
    nt = pl.num_programs(0)
    if fused:
        dcur_ref, dnext_ref, hprev_ref, p_ref, ys_ref, gple_ref, wup_ref, wgate_ref = refs[:8]
        wupb_ref, wgateb_ref, ybuf_ref, gsem = refs[-4:]
        refs = refs[8:-4]
        tick = _gather_start(t, dcur_ref, dnext_ref, ys_ref, ybuf_ref, gsem)
    else:
        h_ref, refs = refs[0], refs[1:]
        tick = lambda k, n: None
    if kind == "gla":
        (gmix_ref, win_hbm, wgu_ref, bg_ref, ghead_ref, wout_ref, o_ref,
         winb_ref, woutb_ref, stage_ref, ssem, st_ref, oacc_ref, y_ref) = refs
    else:
        (gmix_ref, win_hbm, lng_ref, lnb_ref, ws_ref, bs_ref, wout_ref, o_ref,
         winb_ref, woutb_ref, wsb_ref, stage_ref, ssem, us_ref, v_ref) = refs

    @pl.when(t == 0)
    def _():
        _stage_weight(winb_ref, win_hbm.at[j], stage_ref, ssem)
        _cast_weight(woutb_ref, wout_ref)
        if fused:
            _cast_weight(wupb_ref, wup_ref)
            _cast_weight(wgateb_ref, wgate_ref)
        if kind == "sgu":
            ri = lax.broadcasted_iota(jnp.int32, (SGU_CHUNK, SGU_CHUNK), 0)
            ci = lax.broadcasted_iota(jnp.int32, (SGU_CHUNK, SGU_CHUNK), 1)
            for g in range(SGU_GROUPS):
                wsb_ref[g] = jnp.where(ri >= ci, ws_ref[g], 0.0).astype(BF16)

    if fused:
        h2 = hprev_ref[...] + _gather_finish(t, ys_ref, ybuf_ref, gsem)
        h = _ple(h2, p_ref, gple_ref, wupb_ref, wgateb_ref)
    else:
        h = h_ref[...]
    if kind == "gla":
        o_ref[...] = _gla_body(h, t, ns, tick, gmix_ref, wgu_ref, bg_ref, ghead_ref, winb_ref,
                               woutb_ref, st_ref, oacc_ref, y_ref)
    else:
        o_ref[...] = _sgu_body(h, tick, gmix_ref, lng_ref, lnb_ref, bs_ref, winb_ref, woutb_ref,
                               wsb_ref, us_ref, v_ref)
    if fused:
        _gather_drain(t, nt, ys_ref, ybuf_ref, gsem)


def _mixer_layer(kind, i, ns, h, finish, params):
    n = h.shape[0]
    nt = n // T_MIX
    j = i // 2
    tile = lambda t: (t, 0)
    if kind == "gla":
        zw = GLA_ZW
        mixer_specs = [
            pl.BlockSpec((None, 1, D_MODEL), lambda t: (i, 0, 0)),
            pl.BlockSpec(memory_space=pl.ANY),
            pl.BlockSpec((None, GLA_A_PAD, GLA_DK), lambda t: (j, 0, 0)),
            pl.BlockSpec((None, 1, GLA_DK), lambda t: (j, 0, 0)),
            pl.BlockSpec((None, 1, GLA_DV), lambda t: (j, 0, 0)),
            _resident((None, GLA_DV, D_MODEL), lambda t: (j, 0, 0)),
        ]
        mixer_scratch = [
            pltpu.VMEM((D_MODEL, zw), BF16),
            pltpu.VMEM((GLA_DV, D_MODEL), BF16),
            pltpu.VMEM((2, STAGE_ROWS, zw), F32),
            pltpu.SemaphoreType.DMA((2,)),
            pltpu.VMEM((GLA_HEADS, GLA_HV, GLA_HK), F32),
            pltpu.VMEM((T_MIX, GLA_DV), F32),
            pltpu.VMEM((T_MIX, GLA_DV), BF16),
        ]
    else:
        zw = 2 * SGU_WIDTH
        mixer_specs = [
            pl.BlockSpec((None, 1, D_MODEL), lambda t: (i, 0, 0)),
            pl.BlockSpec(memory_space=pl.ANY),
            pl.BlockSpec((None, 1, SGU_WIDTH), lambda t: (j, 0, 0)),
            pl.BlockSpec((None, 1, SGU_WIDTH), lambda t: (j, 0, 0)),
            _resident((None, SGU_GROUPS, SGU_CHUNK, SGU_CHUNK), lambda t: (j, 0, 0, 0)),
            _resident((None, SGU_CHUNK, SGU_WIDTH), lambda t: (j, 0, 0)),
            _resident((None, SGU_WIDTH, D_MODEL), lambda t: (j, 0, 0)),
        ]
        mixer_scratch = [
            pltpu.VMEM((D_MODEL, zw), BF16),
            pltpu.VMEM((SGU_WIDTH, D_MODEL), BF16),
            pltpu.VMEM((SGU_GROUPS, SGU_CHUNK, SGU_CHUNK), BF16),
            pltpu.VMEM((2, STAGE_ROWS, zw), F32),
            pltpu.SemaphoreType.DMA((2,)),
            pltpu.VMEM((T_MIX, SGU_WIDTH), BF16),
            pltpu.VMEM((T_MIX, SGU_WIDTH), F32),
        ]
    if finish is None:
        head_specs = [pl.BlockSpec((T_MIX, D_MODEL), tile)]
        head_args = (h,)
        tail_scratch = []
    else:
        dest3, p2, ys, (gple, wup, wgate) = finish
        li = i - 1
        head_specs = _finish_specs(li, lambda t: (t, 0, 0),
                                   lambda t: (jnp.minimum(t + 1, nt - 1), 0, 0),
                                   tile, lambda t: (li * nt + t, 0))
        head_args = (dest3, dest3, h, p2, ys, gple, wup, wgate)
        tail_scratch = _FINISH_SCRATCH
    return pl.pallas_call(
        functools.partial(_mixer_kernel, kind=kind, fused=finish is not None, j=j, ns=ns),
        grid=(nt,),
        in_specs=head_specs + mixer_specs,
        out_specs=pl.BlockSpec((T_MIX, D_MODEL), tile),
        out_shape=jax.ShapeDtypeStruct((n, D_MODEL), F32),
        scratch_shapes=mixer_scratch + tail_scratch,
        compiler_params=_params(("arbitrary",)),
        name=kind + ("_fused" if finish is not None else "") + "_mixer",
    )(*head_args, *params)


def _first_argmax(rows):
    best = rows[0]
    idx = jnp.zeros(best.shape, jnp.int32)
    for j in range(1, len(rows)):
        upd = rows[j] > best
        idx = jnp.where(upd, j, idx)
        best = jnp.where(upd, rows[j], best)
    return idx, best


def _router_kernel(h_ref, gffn_ref, wrt_ref, brt_ref, triu_ref, route_ref, counts_ref, cnt_ref):
    @pl.when(pl.program_id(0) == 0)
    def _():
        cnt_ref[...] = jnp.zeros_like(cnt_ref)

    hb = _rms(h_ref[...], gffn_ref[...]).astype(BF16)
    lg = _dot_nt(wrt_ref[...], hb) + brt_ref[...]
    row = lambda j: lg[j:j + 1, :]
    g, _ = _first_argmax([row(j) for j in range(MOE_GROUPS)])
    ex = []
    for e in range(MOE_EPG):
        v = row(MOE_GROUPS + e)
        for gg in range(1, MOE_GROUPS):
            v = jnp.where(g == gg, row(MOE_GROUPS + gg * MOE_EPG + e), v)
        ex.append(v)
    i1, _ = _first_argmax(ex)
    neg = jnp.full(ex[0].shape, -jnp.inf, F32)
    i2, _ = _first_argmax([jnp.where(i1 == e, neg, ex[e]) for e in range(MOE_EPG)])
    lo = jnp.minimum(i1, i2)
    hi = jnp.maximum(i1, i2)
    pair = jnp.zeros(lo.shape, jnp.int32)
    for p, (a, b) in enumerate(PAIRS):
        pair = jnp.where((lo == a) & (hi == b), p, pair)
    bucket = g * N_PAIRS + pair

    t = bucket.shape[1]
    onehot = (lax.broadcasted_iota(jnp.int32, (ROUTE_ROWS, t), 0) == bucket).astype(F32)
    before = _dot(onehot.astype(BF16), triu_ref[...])
    rank = jnp.sum(onehot * (before + cnt_ref[:, 0:1]), axis=0, keepdims=True).astype(jnp.int32)
    cnt_ref[...] = cnt_ref[...] + jnp.sum(onehot, axis=1, keepdims=True)
    counts_ref[...] = cnt_ref[...]
    r8 = lax.broadcasted_iota(jnp.int32, (8, t), 0)
    route_ref[...] = jnp.where(r8 == 0, bucket, jnp.where(r8 == 1, rank, 0))


def _router(h, i, gffn, wrt, brt, triu):
    n = h.shape[0]
    return pl.pallas_call(
        _router_kernel,
        grid=(n // T_ROUTE,),
        in_specs=[
            pl.BlockSpec((T_ROUTE, D_MODEL), lambda t: (t, 0)),
            pl.BlockSpec((None, 1, D_MODEL), lambda t: (i, 0, 0)),
            pl.BlockSpec((None, ROUTE_ROWS, D_MODEL), lambda t: (i, 0, 0)),
            pl.BlockSpec((None, ROUTE_ROWS, T_ROUTE), lambda t: (i, 0, 0)),
            pl.BlockSpec((T_ROUTE, T_ROUTE), lambda t: (0, 0)),
        ],
        out_specs=[
            pl.BlockSpec((8, T_ROUTE), lambda t: (0, t)),
            pl.BlockSpec((ROUTE_ROWS, LANES), lambda t: (0, 0)),
        ],
        out_shape=[
            jax.ShapeDtypeStruct((8, n), jnp.int32),
            jax.ShapeDtypeStruct((ROUTE_ROWS, LANES), F32),
        ],
        scratch_shapes=[pltpu.VMEM((ROUTE_ROWS, LANES), F32)],
        compiler_params=_params(("arbitrary",)),
        name="moe_router",
    )(h, gffn, wrt, brt, triu)


def _dispatch_kernel(zs_ref, dest_ref, h_ref, xs_ref, zero_ref, rows_ref, zsem, sem):
    @pl.when(pl.program_id(0) == 0)
    def _():
        zero_ref[...] = jnp.zeros_like(zero_ref)

        def fill(k):
            return pltpu.make_async_copy(zero_ref, _row_block(xs_ref, zs_ref[k], MOE_UNIT), zsem)

        for k in range(N_ZERO_FILLS):
            pl.when(zs_ref[N_ZERO_FILLS + k] != 0)(lambda k=k: fill(k).start())
        for k in range(N_ZERO_FILLS):
            pl.when(zs_ref[N_ZERO_FILLS + k] != 0)(lambda k=k: fill(k).wait())

    _store_row_major(rows_ref, h_ref[...])

    def issue(k, carry):
        base = pl.multiple_of(k * DMA_UNROLL, DMA_UNROLL)
        for u in range(DMA_UNROLL):
            d = dest_ref[0, 0, base + u]
            pltpu.make_async_copy(_row_block(rows_ref, base + u, 1),
                                  _row_block(xs_ref, d, 1), sem).start(priority=u % 2)
        return carry

    lax.fori_loop(0, T_ROW // DMA_UNROLL, issue, 0)
    pltpu.make_async_copy(rows_ref, _row_block(xs_ref, 0, T_ROW), sem).wait()


def _dispatch(h, dest3, zs):
    n = h.shape[0]
    grid_spec = pltpu.PrefetchScalarGridSpec(
        num_scalar_prefetch=1,
        grid=(n // T_ROW,),
        in_specs=[
            pl.BlockSpec((1, 1, T_ROW), lambda t, zs: (t, 0, 0), memory_space=pltpu.SMEM),
            pl.BlockSpec((T_ROW, D_MODEL), lambda t, zs: (t, 0)),
        ],
        out_specs=pl.BlockSpec(memory_space=pl.ANY),
        scratch_shapes=[
            pltpu.VMEM((MOE_UNIT * ROW_SUB, LANES), F32),
            pltpu.VMEM((T_ROW * ROW_SUB, LANES), F32),
            pltpu.SemaphoreType.DMA(()),
            pltpu.SemaphoreType.DMA(()),
        ],
    )
    return pl.pallas_call(
        _dispatch_kernel,
        grid_spec=grid_spec,
        out_shape=jax.ShapeDtypeStruct((_sorted_rows(n) * ROW_SUB, LANES), F32),
        compiler_params=_params(("arbitrary",)),
        name="moe_dispatch",
    )(zs, dest3, h)


def _experts_kernel(bg0_ref, bnt_ref, bg_ref, bea_ref, beb_ref, trow_ref, tunits_ref, ntot_ref,
                    xs_ref, gffn_ref, wr_ref, br_ref,
                    wga_ref, wua_ref, wda_ref, wgb_ref, wub_ref, wdb_ref,
                    ys_ref,
                    wgab_ref, wuab_ref, wdab_ref, wgbb_ref, wubb_ref, wdbb_ref,
                    xbuf_ref, ybuf_ref, isem, osem):
    b = pl.program_id(0)
    pair = b % N_PAIRS

    def cast_a():
        _cast_weight(wgab_ref, wga_ref)
        _cast_weight(wuab_ref, wua_ref)
        _cast_weight(wdab_ref, wda_ref)

    def cast_b():
        _cast_weight(wgbb_ref, wgb_ref)
        _cast_weight(wubb_ref, wub_ref)
        _cast_weight(wdbb_ref, wdb_ref)

    a_changes = [q for q in range(N_PAIRS) if q == 0 or PAIRS[q][0] != PAIRS[q - 1][0]]
    b_changes = [q for q in range(N_PAIRS) if q == 0 or PAIRS[q][1] != PAIRS[q - 1][1]]
    pl.when(functools.reduce(jnp.logical_or, [pair == q for q in a_changes]))(cast_a)
    pl.when(functools.reduce(jnp.logical_or, [pair == q for q in b_changes]))(cast_b)

    g0 = bg0_ref[b]
    ntot = ntot_ref[0]
    grp = bg_ref[b]
    ea = bea_ref[b]
    eb = beb_ref[b]

    def compute(x32):
        x = _rms(x32, gffn_ref[...]).astype(BF16)
        lg = _dot(x, wr_ref[...]) + br_ref[...]
        lane = lax.broadcasted_iota(jnp.int32, lg.shape, 1)
        glog = jnp.where(lane < MOE_GROUPS, lg, -jnp.inf)
        pe = jnp.exp(glog - jnp.max(glog, axis=-1, keepdims=True))
        g_w = (jnp.sum(jnp.where(lane == grp, pe, 0.0), axis=-1, keepdims=True)
               / jnp.sum(pe, axis=-1, keepdims=True))
        la = jnp.sum(jnp.where(lane == MOE_GROUPS + ea, lg, 0.0), axis=-1, keepdims=True)
        lb = jnp.sum(jnp.where(lane == MOE_GROUPS + eb, lg, 0.0), axis=-1, keepdims=True)
        m = jnp.maximum(la, lb)
        pa = jnp.exp(la - m)
        pb = jnp.exp(lb - m)
        ca = pa / (pa + pb) * g_w
        cb = pb / (pa + pb) * g_w

        def expert(wg_ref, wu_ref, c):
            return (jax.nn.silu(_dot(x, wg_ref[...])) * _dot(x, wu_ref[...]) * c).astype(BF16)

        return (_dot(expert(wgab_ref, wuab_ref, ca), wdab_ref[...])
                + _dot(expert(wgbb_ref, wubb_ref, cb), wdbb_ref[...]))

    def load(g, rows):
        return pltpu.make_async_copy(_row_block(xs_ref, trow_ref[g], rows),
                                     _row_block(xbuf_ref.at[g % 2], 0, rows), isem.at[g % 2])

    def store(g, rows):
        return pltpu.make_async_copy(_row_block(ybuf_ref.at[g % 2], 0, rows),
                                     _row_block(ys_ref, trow_ref[g], rows), osem.at[g % 2])

    def by_size(g, fn):
        pl.when(tunits_ref[g] == T_MOE // MOE_UNIT)(lambda: fn(T_MOE))
        pl.when(tunits_ref[g] == 1)(lambda: fn(MOE_UNIT))

    @pl.when((b == 0) & (ntot > 0))
    def _():
        by_size(0, lambda rows: load(0, rows).start())

    def body(j, carry):
        g = g0 + j

        @pl.when(g + 1 < ntot)
        def _():
            by_size(g + 1, lambda rows: load(g + 1, rows).start())

        @pl.when(g >= 2)
        def _():
            by_size(g - 2, lambda rows: store(g - 2, rows).wait())

        def run(rows):
            load(g, rows).wait()
            y = compute(_load_row_major(xbuf_ref.at[g % 2], rows))
            _store_row_major(ybuf_ref.at[g % 2], y)
            store(g, rows).start()

        by_size(g, run)
        return carry

    lax.fori_loop(0, bnt_ref[b], body, 0)

    @pl.when(b == N_BUCKETS - 1)
    def _():
        for back in (2, 1):
            @pl.when(ntot >= back)
            def _():
                by_size(ntot - back, lambda rows: store(ntot - back, rows).wait())


def _experts(xs, i, tiles, gffn, wr, br, wg, wu, wd):
    bg0, bnt, trow, tunits, ntot = tiles
    e0 = i * MOE_EXPERTS
    bg = jnp.asarray([q // N_PAIRS for q in range(N_BUCKETS)], jnp.int32)
    bea = jnp.asarray([(q // N_PAIRS) * MOE_EPG + PAIRS[q % N_PAIRS][0] for q in range(N_BUCKETS)],
                      jnp.int32)
    beb = jnp.asarray([(q // N_PAIRS) * MOE_EPG + PAIRS[q % N_PAIRS][1] for q in range(N_BUCKETS)],
                      jnp.int32)
    ea_map = lambda b, g0, nt, g, ea, eb, *_: (e0 + ea[b], 0, 0)
    eb_map = lambda b, g0, nt, g, ea, eb, *_: (e0 + eb[b], 0, 0)
    up_spec = lambda m: pl.BlockSpec((None, D_MODEL, MOE_FF), m)
    down_spec = lambda m: pl.BlockSpec((None, MOE_FF, D_MODEL), m)
    grid_spec = pltpu.PrefetchScalarGridSpec(
        num_scalar_prefetch=8,
        grid=(N_BUCKETS,),
        in_specs=[
            pl.BlockSpec(memory_space=pl.ANY),
            pl.BlockSpec((None, 1, D_MODEL), lambda b, *_: (i, 0, 0)),
            pl.BlockSpec((None, D_MODEL, LANES), lambda b, *_: (i, 0, 0)),
            pl.BlockSpec((None, 1, LANES), lambda b, *_: (i, 0, 0)),
            up_spec(ea_map), up_spec(ea_map), down_spec(ea_map),
            up_spec(eb_map), up_spec(eb_map), down_spec(eb_map),
        ],
        out_specs=pl.BlockSpec(memory_space=pl.ANY),
        scratch_shapes=[
            pltpu.VMEM((D_MODEL, MOE_FF), BF16), pltpu.VMEM((D_MODEL, MOE_FF), BF16),
            pltpu.VMEM((MOE_FF, D_MODEL), BF16),
            pltpu.VMEM((D_MODEL, MOE_FF), BF16), pltpu.VMEM((D_MODEL, MOE_FF), BF16),
            pltpu.VMEM((MOE_FF, D_MODEL), BF16),
            pltpu.VMEM((2, T_MOE * ROW_SUB, LANES), F32),
            pltpu.VMEM((2, T_MOE * ROW_SUB, LANES), F32),
            pltpu.SemaphoreType.DMA((2,)),
            pltpu.SemaphoreType.DMA((2,)),
        ],
    )
    return pl.pallas_call(
        _experts_kernel,
        grid_spec=grid_spec,
        out_shape=jax.ShapeDtypeStruct(xs.shape, F32),
        input_output_aliases={8: 0},
        compiler_params=_params(("arbitrary",)),
        name="moe_experts",
    )(bg0, bnt, bg, bea, beb, trow, tunits, ntot, xs, gffn, wr, br, wg, wu, wd, wg, wu, wd)


def _final_kernel(dcur_ref, dnext_ref, h_ref, p_ref, ys_ref, gple_ref, wup_ref, wgate_ref,
                  gfin_ref, o_ref, wupb_ref, wgateb_ref, ybuf_ref, sem):
    t = pl.program_id(0)
    _gather_start(t, dcur_ref, dnext_ref, ys_ref, ybuf_ref, sem)(0, 1)

    @pl.when(t == 0)
    def _():
        _cast_weight(wupb_ref, wup_ref)
        _cast_weight(wgateb_ref, wgate_ref)

    h2 = h_ref[...] + _gather_finish(t, ys_ref, ybuf_ref, sem)
    o_ref[...] = _rms(_ple(h2, p_ref, gple_ref, wupb_ref, wgateb_ref), gfin_ref[...])
    _gather_drain(t, pl.num_programs(0), ys_ref, ybuf_ref, sem)


def _final_layer(h, dest3, p2, ys, i, gple, wup, wgate, gfin):
    n = h.shape[0]
    nt = n // T_ROW
    tile = lambda t: (t, 0)
    return pl.pallas_call(
        _final_kernel,
        grid=(nt,),
        in_specs=_finish_specs(i, lambda t: (t, 0, 0),
                               lambda t: (jnp.minimum(t + 1, nt - 1), 0, 0),
                               tile, lambda t: (i * nt + t, 0))
        + [pl.BlockSpec((1, D_MODEL), lambda t: (0, 0))],
        out_specs=pl.BlockSpec((T_ROW, D_MODEL), tile),
        out_shape=jax.ShapeDtypeStruct((n, D_MODEL), F32),
        scratch_shapes=_FINISH_SCRATCH,
        compiler_params=_params(("arbitrary",)),
        name="final_combine_ple",
    )(dest3, dest3, h, p2, ys, gple, wup, wgate, gfin)


def _sorted_rows(n):
    return n + N_BUCKETS * MOE_UNIT


def _max_tiles(n):
    return n // T_MOE + N_BUCKETS


def _plan(route, counts, n):
    bucket = route[0]
    rank = route[1]
    cnt = counts[:N_BUCKETS, 0].astype(jnp.int32)
    units = (cnt + MOE_UNIT - 1) // MOE_UNIT
    ends = jnp.cumsum(units * MOE_UNIT)
    starts = ends - units * MOE_UNIT
    dest = starts[bucket] + rank
    tail_rows = ends[-1] + jnp.arange(N_BUCKETS, dtype=jnp.int32) * MOE_UNIT
    fill_on = jnp.concatenate([cnt % MOE_UNIT != 0, tail_rows < _sorted_rows(n)])
    fill_row = jnp.where(fill_on, jnp.concatenate([ends - MOE_UNIT, tail_rows]), 0)
    zs = jnp.concatenate([fill_row, fill_on.astype(jnp.int32)]).astype(jnp.int32)
    per_tile = T_MOE // MOE_UNIT
    n_full = units // per_tile
    bnt = n_full + units % per_tile
    bend = jnp.cumsum(bnt)
    bg0 = bend - bnt
    g = jnp.arange(_max_tiles(n), dtype=jnp.int32)
    tb = jnp.minimum(jnp.sum(g[:, None] >= bend[None, :], axis=1), N_BUCKETS - 1)
    j = g - bg0[tb]
    live = g < bend[-1]
    trow = jnp.where(live, starts[tb] + j * T_MOE, 0)
    tunits = jnp.where(live, jnp.where(j < n_full[tb], per_tile, 1), 0)
    i32 = lambda a: a.astype(jnp.int32)
    tiles = (i32(bg0), i32(bnt), i32(trow), i32(tunits), i32(bend[-1:]))
    return dest, zs, tiles


def kernel(x, p, gla_w_in, gla_w_gate_up, gla_b_gate, gla_g_head, gla_w_out, sgu_w_in, sgu_ln_g,
           sgu_ln_b, sgu_w_s, sgu_b_s, sgu_w_out, norm_mix_g, norm_ffn_g, norm_ple_g,
           moe_w_route_group, moe_b_route_group, moe_w_route_expert, moe_b_route_expert,
           moe_w_gate, moe_w_up, moe_w_down, ple_w_up, ple_w_gate, final_norm_g):
    batch, seq, d = x.shape
    assert d == D_MODEL and seq % T_MIX == 0
    n = batch * seq
    assert n % T_ROUTE == 0
    depth = p.shape[0]
    assert depth == DEPTH
    na, nb = gla_w_in.shape[0], sgu_w_in.shape[0]

    gla_win = jnp.pad(gla_w_in, ((0, 0), (0, 0), (0, GLA_A_PAD - GLA_RANK)))
    gla_wgu = jnp.pad(gla_w_gate_up, ((0, 0), (0, GLA_A_PAD - GLA_RANK), (0, 0)))
    gla_bg = gla_b_gate.reshape(na, 1, GLA_DK)
    gla_gh = gla_g_head.reshape(na, 1, GLA_DV)
    sgu_lng = sgu_ln_g.reshape(nb, 1, SGU_WIDTH)
    sgu_lnb = sgu_ln_b.reshape(nb, 1, SGU_WIDTH)
    sgu_bs = jnp.repeat(jnp.transpose(sgu_b_s, (0, 2, 1)), SGU_GC, axis=-1)
    g_mix = norm_mix_g.reshape(depth, 1, D_MODEL)
    g_ffn = norm_ffn_g.reshape(depth, 1, D_MODEL)
    g_ple = norm_ple_g.reshape(depth, 1, D_MODEL)
    g_fin = final_norm_g.reshape(1, D_MODEL)
    w_route = jnp.concatenate(
        [moe_w_route_group, moe_w_route_expert.reshape(depth, D_MODEL, MOE_EXPERTS)], axis=-1)
    b_route = jnp.concatenate(
        [moe_b_route_group, moe_b_route_expert.reshape(depth, MOE_EXPERTS)], axis=-1)
    n_logits = MOE_GROUPS + MOE_EXPERTS
    wr = jnp.pad(w_route, ((0, 0), (0, 0), (0, LANES - n_logits))).astype(BF16)
    br = jnp.pad(b_route, ((0, 0), (0, LANES - n_logits))).reshape(depth, 1, LANES)
    wrt = jnp.pad(jnp.transpose(w_route, (0, 2, 1)),
                  ((0, 0), (0, ROUTE_ROWS - n_logits), (0, 0))).astype(BF16)
    brt = jnp.broadcast_to(
        jnp.pad(b_route, ((0, 0), (0, ROUTE_ROWS - n_logits)))[:, :, None],
        (depth, ROUTE_ROWS, T_ROUTE))
    triu = jnp.triu(jnp.ones((T_ROUTE, T_ROUTE), BF16), k=1)
    wg = moe_w_gate.reshape(depth * MOE_EXPERTS, D_MODEL, MOE_FF)
    wu = moe_w_up.reshape(depth * MOE_EXPERTS, D_MODEL, MOE_FF)
    wd = moe_w_down.reshape(depth * MOE_EXPERTS, MOE_FF, D_MODEL)
    p2 = p.reshape(depth * n, PLE_DIM)

    ns = seq // T_MIX
    h = x.reshape(n, D_MODEL)
    finish = None
    for i in range(depth):
        if i % 2 == 0:
            kind, params = "gla", (g_mix, gla_win, gla_wgu, gla_bg, gla_gh, gla_w_out)
        else:
            kind, params = "sgu", (g_mix, sgu_w_in, sgu_lng, sgu_lnb, sgu_w_s, sgu_bs, sgu_w_out)
        h = _mixer_layer(kind, i, ns, h, finish, params)
        route, counts = _router(h, i, g_ffn, wrt, brt, triu)
        dest, zs, tiles = _plan(route, counts, n)
        dest3 = dest.reshape(n // T_ROW, 1, T_ROW)
        xs = _dispatch(h, dest3, zs)
        ys = _experts(xs, i, tiles, g_ffn, wr, br, wg, wu, wd)
        finish = (dest3, p2, ys, (g_ple, ple_w_up, ple_w_gate))
    h = _final_layer(h, dest3, p2, ys, depth - 1, g_ple, ple_w_up, ple_w_gate, g_fin)
    return h.reshape(batch, seq, D_MODEL)
```

```python
import functools

import jax
import jax.numpy as jnp
from jax import lax
from jax.experimental import pallas as pl
from jax.experimental.pallas import tpu as pltpu

F32 = jnp.float32
BF16 = jnp.bfloat16

D_MODEL = 1024
DEPTH = 4
GLA_HEADS = 4
GLA_DK = 512
GLA_DV = 1024
GLA_HK = GLA_DK // GLA_HEADS
GLA_HV = GLA_DV // GLA_HEADS
GLA_RANK = 16
GLA_TAU = 16.0
GLA_CHUNK = 64
SGU_WIDTH = 2048
SGU_GROUPS = 8
SGU_GC = SGU_WIDTH // SGU_GROUPS
SGU_CHUNK = 128
MOE_GROUPS = 4
MOE_EPG = 4
MOE_EXPERTS = MOE_GROUPS * MOE_EPG
MOE_FF = 512
PLE_DIM = 256
EPS = 1e-6

LANES = 128
ROW_SUB = D_MODEL // LANES
GLA_A_PAD = LANES
GLA_ZW = 2 * GLA_DK + 2 * GLA_DV + GLA_A_PAD
ROUTE_ROWS = 32
ROUTE_OUT_ROWS = 8
PAIRS = ((0, 1), (0, 2), (1, 2), (1, 3), (2, 3), (0, 3))
N_PAIRS = len(PAIRS)
N_BUCKETS = MOE_GROUPS * N_PAIRS

T_MIX = 256
SGU_VBLK = 512
T_ROUTE = 512
T_ROW = T_MIX
T_MOE = 256
MOE_UNIT = 128
MOE_UNIT_LOG2 = 7
assert 1 << MOE_UNIT_LOG2 == MOE_UNIT and T_MOE == 2 * MOE_UNIT
DMA_UNROLL = 8
CAST_ROWS = 256
STAGE_ROWS = 64
VMEM_LIMIT = 56 * 1024 * 1024


def _rms(x, g):
    ms = jnp.mean(x * x, axis=-1, keepdims=True)
    return x * lax.rsqrt(ms + EPS) * g


def _dot(a, b):
    return jnp.dot(a, b, preferred_element_type=F32)


def _dot_nt(a, b):
    return lax.dot_general(a, b, (((1,), (1,)), ((), ())), preferred_element_type=F32)


def _dot_tn(a, b):
    return lax.dot_general(a, b, (((0,), (0,)), ((), ())), preferred_element_type=F32)


def _params(sem):
    return pltpu.CompilerParams(dimension_semantics=sem, vmem_limit_bytes=VMEM_LIMIT)


def _resident(block_shape, index_map):
    return pl.BlockSpec(block_shape, index_map, pipeline_mode=pl.Buffered(1))


def _cast_weight(dst_ref, src_ref):
    rows = src_ref.shape[0]
    for r in range(0, rows, CAST_ROWS):
        dst_ref[r:r + CAST_ROWS, :] = src_ref[r:r + CAST_ROWS, :].astype(BF16)


def _store_row_major(dst_ref, x):
    m = x.shape[0]
    for c in range(ROW_SUB):
        dst_ref[pl.ds(c, m, stride=ROW_SUB), :] = x[:, c * LANES:(c + 1) * LANES]


def _load_row_major(src_ref, m):
    return jnp.concatenate(
        [src_ref[pl.ds(c, m, stride=ROW_SUB), :] for c in range(ROW_SUB)], axis=-1)


def _row_block(ref, row, rows):
    start = row * ROW_SUB
    if not isinstance(start, int):
        start = pl.multiple_of(start, ROW_SUB)
    return ref.at[pl.ds(start, rows * ROW_SUB), :]


def _stage_weight(dst_ref, w_hbm_ref, stage_ref, sem):
    rows, ch = dst_ref.shape[0], stage_ref.shape[1]

    def chunk(k):
        return pltpu.make_async_copy(w_hbm_ref.at[pl.ds(k * ch, ch), :], stage_ref.at[k % 2],
                                     sem.at[k % 2])

    chunk(0).start()
    for k in range(rows // ch):
        if (k + 1) * ch < rows:
            chunk(k + 1).start()
        chunk(k).wait()
        dst_ref[k * ch:(k + 1) * ch, :] = stage_ref[k % 2].astype(BF16)


def _units(count):
    return lax.shift_right_logical(count + (MOE_UNIT - 1), MOE_UNIT_LOG2)


def _bucket_layout(cnt_ref, starts_ref):
    def body(b, row):
        starts_ref[b] = row
        return row + _units(cnt_ref[b]) * MOE_UNIT
    return lax.fori_loop(0, N_BUCKETS, body, jnp.int32(0))


def _gather_start(t, dcur_ref, dnext_ref, ys_ref, ybuf_ref, sem):
    def row_copy(dref, slot, r):
        return pltpu.make_async_copy(_row_block(ys_ref, dref[0, 0, r], 1),
                                     _row_block(ybuf_ref.at[slot], r, 1), sem.at[slot])

    @pl.when(t == 0)
    def _():
        def body(k, carry):
            base = pl.multiple_of(k * DMA_UNROLL, DMA_UNROLL)
            for u in range(DMA_UNROLL):
                row_copy(dcur_ref, 0, base + u).start()
            return carry
        lax.fori_loop(0, T_ROW // DMA_UNROLL, body, 0)

    def issue_next(k, n):
        for r in range(k * T_ROW // n, (k + 1) * T_ROW // n):
            row_copy(dnext_ref, (t + 1) % 2, r).start()

    return issue_next


def _gather_wait(slot, ys_ref, ybuf_ref, sem):
    pltpu.make_async_copy(_row_block(ys_ref, 0, T_ROW), ybuf_ref.at[slot], sem.at[slot]).wait()


def _gather_finish(t, ys_ref, ybuf_ref, sem):
    _gather_wait(t % 2, ys_ref, ybuf_ref, sem)
    return _load_row_major(ybuf_ref.at[t % 2], T_ROW)


def _gather_drain(t, nt, ys_ref, ybuf_ref, sem):
    @pl.when(t == nt - 1)
    def _():
        _gather_wait((t + 1) % 2, ys_ref, ybuf_ref, sem)


def _ple(h2, p_ref, gple_ref, wupb_ref, wgateb_ref):
    pn = _rms(h2, gple_ref[...]).astype(BF16)
    gate = jax.nn.sigmoid(_dot(pn, wgateb_ref[...]))
    up = _dot(p_ref[...].astype(BF16), wupb_ref[...])
    return h2 + up * gate


def _finish_specs(i, nt):
    return [
        pl.BlockSpec((1, 1, T_ROW), lambda t: (t, 0, 0), memory_space=pltpu.SMEM),
        pl.BlockSpec((1, 1, T_ROW), lambda t: (jnp.minimum(t + 1, nt - 1), 0, 0),
                     memory_space=pltpu.SMEM),
        pl.BlockSpec((T_ROW, D_MODEL), lambda t: (t, 0)),
        pl.BlockSpec((T_ROW, PLE_DIM), lambda t: (i * nt + t, 0)),
        pl.BlockSpec(memory_space=pl.ANY),
        pl.BlockSpec((None, 1, D_MODEL), lambda t: (i, 0, 0)),
        _resident((None, PLE_DIM, D_MODEL), lambda t: (i, 0, 0)),
        _resident((None, D_MODEL, D_MODEL), lambda t: (i, 0, 0)),
    ]


N_FINISH_INPUTS = 8

_FINISH_SCRATCH = [
    pltpu.VMEM((PLE_DIM, D_MODEL), BF16),
    pltpu.VMEM((D_MODEL, D_MODEL), BF16),
    pltpu.VMEM((2, T_ROW * ROW_SUB, LANES), F32),
    pltpu.SemaphoreType.DMA((2,)),
]


def _gla_body(h, t, ns, tick, gmix_ref, wgu_ref, bg_ref, ghead_ref, winb_ref, woutb_ref,
              st_ref, oacc_ref, y_ref):
    @pl.when(t % ns == 0)
    def _():
        st_ref[...] = jnp.zeros_like(st_ref)

    hn = _rms(h, gmix_ref[...]).astype(BF16)
    z = _dot(hn, winb_ref[...])
    a = z[:, 2 * GLA_DK + 2 * GLA_DV:].astype(BF16)
    pre = _dot(a, wgu_ref[...].astype(BF16)) + bg_ref[...]
    la = (jnp.minimum(pre, 0.0) - jnp.log1p(jnp.exp(-jnp.abs(pre)))) * (1.0 / GLA_TAU)

    C = GLA_CHUNK
    ri = lax.broadcasted_iota(jnp.int32, (C, C), 0)
    ci = lax.broadcasted_iota(jnp.int32, (C, C), 1)
    causal = ri >= ci
    tri = causal.astype(BF16)

    for c in range(T_MIX // C):
        rows = slice(c * C, (c + 1) * C)
        la_c = la[rows]
        hi = la_c.astype(BF16)
        lo = (la_c - hi.astype(F32)).astype(BF16)
        bc = _dot(tri, hi) + _dot(tri, lo)
        bend = bc[C - 1:C, :]
        kc = z[rows, GLA_DK:2 * GLA_DK]
        qd = (z[rows, 0:GLA_DK] * (GLA_HK ** -0.5) * jnp.exp(bc)).astype(BF16)
        kd = (kc * jnp.exp(-bc)).astype(BF16)
        ke = (kc * jnp.exp(bend - bc)).astype(BF16)
        vc = z[rows, 2 * GLA_DK:2 * GLA_DK + GLA_DV].astype(BF16)
        dec = jnp.exp(bend)
        for hd in range(GLA_HEADS):
            ks = slice(hd * GLA_HK, (hd + 1) * GLA_HK)
            vs = slice(hd * GLA_HV, (hd + 1) * GLA_HV)
            att = jnp.where(causal, _dot_nt(qd[:, ks], kd[:, ks]), 0.0).astype(BF16)
            st = st_ref[hd]
            o = _dot(att, vc[:, vs]) + _dot_nt(qd[:, ks], st.astype(BF16))
            st_ref[hd] = st * dec[:, ks] + _dot_tn(vc[:, vs], ke[:, ks])
            oacc_ref[rows, vs] = o
            tick(c * GLA_HEADS + hd, (T_MIX // C) * GLA_HEADS)

    r0 = 2 * GLA_DK + GLA_DV
    for hd in range(GLA_HEADS):
        vs = slice(hd * GLA_HV, (hd + 1) * GLA_HV)
        on = _rms(oacc_ref[:, vs], ghead_ref[:, vs])
        r = z[:, r0 + hd * GLA_HV:r0 + (hd + 1) * GLA_HV]
        y_ref[:, vs] = (jax.nn.silu(r) * on).astype(BF16)
    return h + _dot(y_ref[...], woutb_ref[...])


def _sgu_body(h, tick, gmix_ref, lng_ref, lnb_ref, bs_ref, winb_ref, woutb_ref, wsb_ref, us_ref,
              v_ref):
    C = SGU_CHUNK
    hn = _rms(h, gmix_ref[...]).astype(BF16)

    vsum = jnp.zeros((T_MIX, 1), F32)
    for k in range(SGU_WIDTH // SGU_VBLK):
        cols = slice(k * SGU_VBLK, (k + 1) * SGU_VBLK)
        vb = jax.nn.gelu(_dot(hn, winb_ref[:, SGU_WIDTH + k * SGU_VBLK:SGU_WIDTH + (k + 1) * SGU_VBLK]),
                         approximate=True)
        v_ref[:, cols] = vb
        vsum = vsum + jnp.sum(vb, axis=-1, keepdims=True)
    mean = vsum * (1.0 / SGU_WIDTH)
    vsq = jnp.zeros((T_MIX, 1), F32)
    for k in range(SGU_WIDTH // SGU_VBLK):
        cols = slice(k * SGU_VBLK, (k + 1) * SGU_VBLK)
        xc = v_ref[:, cols] - mean
        vsq = vsq + jnp.sum(xc * xc, axis=-1, keepdims=True)
    rstd = lax.rsqrt(vsq * (1.0 / SGU_WIDTH) + EPS)

    for g in range(SGU_GROUPS):
        cols = slice(g * SGU_GC, (g + 1) * SGU_GC)
        u = jax.nn.gelu(_dot(hn, winb_ref[:, cols]), approximate=True)
        vn = (((v_ref[:, cols] - mean) * rstd) * lng_ref[:, cols] + lnb_ref[:, cols]).astype(BF16)
        wc = wsb_ref[g]
        for c in range(T_MIX // C):
            rows = slice(c * C, (c + 1) * C)
            s = _dot(wc, vn[rows]) + bs_ref[:, cols]
            us_ref[rows, cols] = (u[rows] * s).astype(BF16)
        tick(g, SGU_GROUPS)
    return h + _dot(us_ref[...], woutb_ref[...])


def _mixer_kernel(*refs, kind, fused, j, ns):
    refs = list(refs)
    t = pl.program_id(0)
    nt = pl.num_programs(0)
    if fused:
        (dcur_ref, dnext_ref, hprev_ref, p_ref, ys_ref, gple_ref, wup_ref,
         wgate_ref) = refs[:N_FINISH_INPUTS]
        wupb_ref, wgateb_ref, ybuf_ref, gsem = refs[-len(_FINISH_SCRATCH):]
        refs = refs[N_FINISH_INPUTS:-len(_FINISH_SCRATCH)]
        tick = _gather_start(t, dcur_ref, dnext_ref, ys_ref, ybuf_ref, gsem)
    else:
        h_ref, refs = refs[0], refs[1:]
        tick = lambda k, n: None
    if kind == "gla":
        (gmix_ref, win_hbm, wgu_ref, bg_ref, ghead_ref, wout_ref, o_ref,
         winb_ref, woutb_ref, stage_ref, ssem, st_ref, oacc_ref, y_ref) = refs
    else:
        (gmix_ref, win_hbm, lng_ref, lnb_ref, ws_ref, bs_ref, wout_ref, o_ref,
         winb_ref, woutb_ref, wsb_ref, stage_ref, ssem, us_ref, v_ref) = refs

    @pl.when(t == 0)
    def _():
        _stage_weight(winb_ref, win_hbm.at[j], stage_ref, ssem)
        _cast_weight(woutb_ref, wout_ref)
        if fused:
            _cast_weight(wupb_ref, wup_ref)
            _cast_weight(wgateb_ref, wgate_ref)
        if kind == "sgu":
            ri = lax.broadcasted_iota(jnp.int32, (SGU_CHUNK, SGU_CHUNK), 0)
            ci = lax.broadcasted_iota(jnp.int32, (SGU_CHUNK, SGU_CHUNK), 1)
            for g in range(SGU_GROUPS):
                wsb_ref[g] = jnp.where(ri >= ci, ws_ref[g], 0.0).astype(BF16)

    if fused:
        h2 = hprev_ref[...] + _gather_finish(t, ys_ref, ybuf_ref, gsem)
        h = _ple(h2, p_ref, gple_ref, wupb_ref, wgateb_ref)
    else:
        h = h_ref[...]
    if kind == "gla":
        o_ref[...] = _gla_body(h, t, ns, tick, gmix_ref, wgu_ref, bg_ref, ghead_ref, winb_ref,
                               woutb_ref, st_ref, oacc_ref, y_ref)
    else:
        o_ref[...] = _sgu_body(h, tick, gmix_ref, lng_ref, lnb_ref, bs_ref, winb_ref, woutb_ref,
                               wsb_ref, us_ref, v_ref)
    if fused:
        _gather_drain(t, nt, ys_ref, ybuf_ref, gsem)


def _mixer_layer(kind, i, ns, h, finish, params):
    n = h.shape[0]
    nt = n // T_MIX
    j = i // 2
    tile = lambda t: (t, 0)
    if kind == "gla":
        zw = GLA_ZW
        mixer_specs = [
            pl.BlockSpec((None, 1, D_MODEL), lambda t: (i, 0, 0)),
            pl.BlockSpec(memory_space=pl.ANY),
            pl.BlockSpec((None, GLA_A_PAD, GLA_DK), lambda t: (j, 0, 0)),
            pl.BlockSpec((None, 1, GLA_DK), lambda t: (j, 0, 0)),
            pl.BlockSpec((None, 1, GLA_DV), lambda t: (j, 0, 0)),
            _resident((None, GLA_DV, D_MODEL), lambda t: (j, 0, 0)),
        ]
        mixer_scratch = [
            pltpu.VMEM((D_MODEL, zw), BF16),
            pltpu.VMEM((GLA_DV, D_MODEL), BF16),
            pltpu.VMEM((2, STAGE_ROWS, zw), F32),
            pltpu.SemaphoreType.DMA((2,)),
            pltpu.VMEM((GLA_HEADS, GLA_HV, GLA_HK), F32),
            pltpu.VMEM((T_MIX, GLA_DV), F32),
            pltpu.VMEM((T_MIX, GLA_DV), BF16),
        ]
    else:
        zw = 2 * SGU_WIDTH
        mixer_specs = [
            pl.BlockSpec((None, 1, D_MODEL), lambda t: (i, 0, 0)),
            pl.BlockSpec(memory_space=pl.ANY),
            pl.BlockSpec((None, 1, SGU_WIDTH), lambda t: (j, 0, 0)),
            pl.BlockSpec((None, 1, SGU_WIDTH), lambda t: (j, 0, 0)),
            _resident((None, SGU_GROUPS, SGU_CHUNK, SGU_CHUNK), lambda t: (j, 0, 0, 0)),
            _resident((None, SGU_CHUNK, SGU_WIDTH), lambda t: (j, 0, 0)),
            _resident((None, SGU_WIDTH, D_MODEL), lambda t: (j, 0, 0)),
        ]
        mixer_scratch = [
            pltpu.VMEM((D_MODEL, zw), BF16),
            pltpu.VMEM((SGU_WIDTH, D_MODEL), BF16),
            pltpu.VMEM((SGU_GROUPS, SGU_CHUNK, SGU_CHUNK), BF16),
            pltpu.VMEM((2, STAGE_ROWS, zw), F32),
            pltpu.SemaphoreType.DMA((2,)),
            pltpu.VMEM((T_MIX, SGU_WIDTH), BF16),
            pltpu.VMEM((T_MIX, SGU_WIDTH), F32),
        ]
    if finish is None:
        head_specs = [pl.BlockSpec((T_MIX, D_MODEL), tile)]
        head_args = (h,)
        tail_scratch = []
    else:
        dest3, p2, ys, (gple, wup, wgate) = finish
        head_specs = _finish_specs(i - 1, nt)
        head_args = (dest3, dest3, h, p2, ys, gple, wup, wgate)
        tail_scratch = _FINISH_SCRATCH
    return pl.pallas_call(
        functools.partial(_mixer_kernel, kind=kind, fused=finish is not None, j=j, ns=ns),
        grid=(nt,),
        in_specs=head_specs + mixer_specs,
        out_specs=pl.BlockSpec((T_MIX, D_MODEL), tile),
        out_shape=jax.ShapeDtypeStruct((n, D_MODEL), F32),
        scratch_shapes=mixer_scratch + tail_scratch,
        compiler_params=_params(("arbitrary",)),
        name=kind + ("_fused" if finish is not None else "") + "_mixer",
    )(*head_args, *params)


def _first_argmax(rows):
    best = rows[0]
    idx = jnp.zeros(best.shape, jnp.int32)
    for j in range(1, len(rows)):
        upd = rows[j] > best
        idx = jnp.where(upd, j, idx)
        best = jnp.where(upd, rows[j], best)
    return idx, best


def _router_kernel(h_ref, gffn_ref, wrt_ref, brt_ref, triu_ref, route_ref, counts_ref, cnt_ref):
    @pl.when(pl.program_id(0) == 0)
    def _():
        cnt_ref[...] = jnp.zeros_like(cnt_ref)

    hb = _rms(h_ref[...], gffn_ref[...]).astype(BF16)
    lg = _dot_nt(wrt_ref[...], hb) + brt_ref[...]
    row = lambda j: lg[j:j + 1, :]
    g, _ = _first_argmax([row(j) for j in range(MOE_GROUPS)])
    ex = []
    for e in range(MOE_EPG):
        v = row(MOE_GROUPS + e)
        for gg in range(1, MOE_GROUPS):
            v = jnp.where(g == gg, row(MOE_GROUPS + gg * MOE_EPG + e), v)
        ex.append(v)
    i1, _ = _first_argmax(ex)
    neg = jnp.full(ex[0].shape, -jnp.inf, F32)
    i2, _ = _first_argmax([jnp.where(i1 == e, neg, ex[e]) for e in range(MOE_EPG)])
    lo = jnp.minimum(i1, i2)
    hi = jnp.maximum(i1, i2)
    pair = jnp.zeros(lo.shape, jnp.int32)
    for p, (a, b) in enumerate(PAIRS):
        pair = jnp.where((lo == a) & (hi == b), p, pair)
    bucket = g * N_PAIRS + pair

    t = bucket.shape[1]
    onehot = (lax.broadcasted_iota(jnp.int32, (ROUTE_ROWS, t), 0) == bucket).astype(F32)
    before = _dot(onehot.astype(BF16), triu_ref[...])
    rank = jnp.sum(onehot * (before + cnt_ref[:, 0:1]), axis=0, keepdims=True).astype(jnp.int32)
    cnt_ref[...] = cnt_ref[...] + jnp.sum(onehot, axis=1, keepdims=True)
    counts_ref[...] = cnt_ref[...].astype(jnp.int32)
    r8 = lax.broadcasted_iota(jnp.int32, (ROUTE_OUT_ROWS, t), 0)
    route_ref[...] = jnp.where(r8 == 0, bucket, jnp.where(r8 == 1, rank, 0))


def _router(h, i, gffn, wrt, brt, triu):
    n = h.shape[0]
    return pl.pallas_call(
        _router_kernel,
        grid=(n // T_ROUTE,),
        in_specs=[
            pl.BlockSpec((T_ROUTE, D_MODEL), lambda t: (t, 0)),
            pl.BlockSpec((None, 1, D_MODEL), lambda t: (i, 0, 0)),
            pl.BlockSpec((None, ROUTE_ROWS, D_MODEL), lambda t: (i, 0, 0)),
            pl.BlockSpec((None, ROUTE_ROWS, T_ROUTE), lambda t: (i, 0, 0)),
            pl.BlockSpec((T_ROUTE, T_ROUTE), lambda t: (0, 0)),
        ],
        out_specs=[
            pl.BlockSpec((ROUTE_OUT_ROWS, T_ROUTE), lambda t: (0, t)),
            pl.BlockSpec((ROUTE_ROWS, LANES), lambda t: (0, 0)),
        ],
        out_shape=[
            jax.ShapeDtypeStruct((ROUTE_OUT_ROWS, n), jnp.int32),
            jax.ShapeDtypeStruct((ROUTE_ROWS, LANES), jnp.int32),
        ],
        scratch_shapes=[pltpu.VMEM((ROUTE_ROWS, LANES), F32)],
        compiler_params=_params(("arbitrary",)),
        name="moe_router",
    )(h, gffn, wrt, brt, triu)


def _dispatch_kernel(cnt_ref, dest_ref, h_ref, xs_ref, zero_ref, rows_ref, starts_ref, zsem, sem):
    @pl.when(pl.program_id(0) == 0)
    def _():
        end = _bucket_layout(cnt_ref, starts_ref)
        zero_ref[...] = jnp.zeros_like(zero_ref)
        fills = []
        for b in range(N_BUCKETS):
            c = cnt_ref[b]
            whole = lax.shift_right_logical(c, MOE_UNIT_LOG2) * MOE_UNIT
            fills.append(((c & (MOE_UNIT - 1)) != 0, starts_ref[b] + whole))
        for k in range(N_BUCKETS):
            fills.append((end + k * MOE_UNIT < xs_ref.shape[0] // ROW_SUB, end + k * MOE_UNIT))

        def fill(row):
            return pltpu.make_async_copy(zero_ref, _row_block(xs_ref, row, MOE_UNIT), zsem)

        for on, row in fills:
            pl.when(on)(lambda row=row: fill(row).start())
        for on, row in fills:
            pl.when(on)(lambda row=row: fill(row).wait())

    _store_row_major(rows_ref, h_ref[...])

    def issue(k, carry):
        base = pl.multiple_of(k * DMA_UNROLL, DMA_UNROLL)
        for u in range(DMA_UNROLL):
            d = dest_ref[0, 0, base + u]
            pltpu.make_async_copy(_row_block(rows_ref, base + u, 1),
                                  _row_block(xs_ref, d, 1), sem).start(priority=u % 2)
        return carry

    lax.fori_loop(0, T_ROW // DMA_UNROLL, issue, 0)
    pltpu.make_async_copy(rows_ref, _row_block(xs_ref, 0, T_ROW), sem).wait()


def _dispatch(h, dest3, cnt):
    n = h.shape[0]
    return pl.pallas_call(
        _dispatch_kernel,
        grid=(n // T_ROW,),
        in_specs=[
            pl.BlockSpec(memory_space=pltpu.SMEM),
            pl.BlockSpec((1, 1, T_ROW), lambda t: (t, 0, 0), memory_space=pltpu.SMEM),
            pl.BlockSpec((T_ROW, D_MODEL), lambda t: (t, 0)),
        ],
        out_specs=pl.BlockSpec(memory_space=pl.ANY),
        out_shape=jax.ShapeDtypeStruct((_sorted_rows(n) * ROW_SUB, LANES), F32),
        scratch_shapes=[
            pltpu.VMEM((MOE_UNIT * ROW_SUB, LANES), F32),
            pltpu.VMEM((T_ROW * ROW_SUB, LANES), F32),
            pltpu.SMEM((N_BUCKETS,), jnp.int32),
            pltpu.SemaphoreType.DMA(()),
            pltpu.SemaphoreType.DMA(()),
        ],
        compiler_params=_params(("arbitrary",)),
        name="moe_dispatch",
    )(cnt, dest3, h)


def _experts_kernel(bg_ref, bea_ref, beb_ref,
                    cnt_ref, xs_ref, gffn_ref, wr_ref, br_ref,
                    wga_ref, wua_ref, wda_ref, wgb_ref, wub_ref, wdb_ref,
                    ys_ref,
                    wgab_ref, wuab_ref, wdab_ref, wgbb_ref, wubb_ref, wdbb_ref,
                    xbuf_ref, ybuf_ref, isem, osem,
                    bg0_ref, bnt_ref, trow_ref, tunits_ref, ntot_ref):
    b = pl.program_id(0)
    pair = b % N_PAIRS

    @pl.when(b == 0)
    def _():
        def per_bucket(q, carry):
            row, g = carry
            units = _units(cnt_ref[q])
            n_full = lax.shift_right_logical(units, 1)
            n_tiles = n_full + (units & 1)
            bg0_ref[q] = g
            bnt_ref[q] = n_tiles

            def per_tile(j, c):
                trow_ref[g + j] = row + j * T_MOE
                tunits_ref[g + j] = jnp.where(j < n_full, T_MOE // MOE_UNIT, 1)
                return c

            lax.fori_loop(0, n_tiles, per_tile, 0)
            return row + units * MOE_UNIT, g + n_tiles

        _, n_total = lax.fori_loop(0, N_BUCKETS, per_bucket, (jnp.int32(0), jnp.int32(0)))
        ntot_ref[0] = n_total

    def cast_a():
        _cast_weight(wgab_ref, wga_ref)
        _cast_weight(wuab_ref, wua_ref)
        _cast_weight(wdab_ref, wda_ref)

    def cast_b():
        _cast_weight(wgbb_ref, wgb_ref)
        _cast_weight(wubb_ref, wub_ref)
        _cast_weight(wdbb_ref, wdb_ref)

    a_changes = [q for q in range(N_PAIRS) if q == 0 or PAIRS[q][0] != PAIRS[q - 1][0]]
    b_changes = [q for q in range(N_PAIRS) if q == 0 or PAIRS[q][1] != PAIRS[q - 1][1]]
    pl.when(functools.reduce(jnp.logical_or, [pair == q for q in a_changes]))(cast_a)
    pl.when(functools.reduce(jnp.logical_or, [pair == q for q in b_changes]))(cast_b)

    g0 = bg0_ref[b]
    ntot = ntot_ref[0]
    grp = bg_ref[b]
    ea = bea_ref[b]
    eb = beb_ref[b]

    def compute(x32):
        x = _rms(x32, gffn_ref[...]).astype(BF16)
        lg = _dot(x, wr_ref[...]) + br_ref[...]
        lane = lax.broadcasted_iota(jnp.int32, lg.shape, 1)
        glog = jnp.where(lane < MOE_GROUPS, lg, -jnp.inf)
        pe = jnp.exp(glog - jnp.max(glog, axis=-1, keepdims=True))
        g_w = (jnp.sum(jnp.where(lane == grp, pe, 0.0), axis=-1, keepdims=True)
               / jnp.sum(pe, axis=-1, keepdims=True))
        la = jnp.sum(jnp.where(lane == MOE_GROUPS + ea, lg, 0.0), axis=-1, keepdims=True)
        lb = jnp.sum(jnp.where(lane == MOE_GROUPS + eb, lg, 0.0), axis=-1, keepdims=True)
        m = jnp.maximum(la, lb)
        pa = jnp.exp(la - m)
        pb = jnp.exp(lb - m)
        ca = pa / (pa + pb) * g_w
        cb = pb / (pa + pb) * g_w

        def expert(wg_ref, wu_ref, c):
            return (jax.nn.silu(_dot(x, wg_ref[...])) * _dot(x, wu_ref[...]) * c).astype(BF16)

        return (_dot(expert(wgab_ref, wuab_ref, ca), wdab_ref[...])
                + _dot(expert(wgbb_ref, wubb_ref, cb), wdbb_ref[...]))

    def load(g, rows):
        return pltpu.make_async_copy(_row_block(xs_ref, trow_ref[g], rows),
                                     _row_block(xbuf_ref.at[g % 2], 0, rows), isem.at[g % 2])

    def store(g, rows):
        return pltpu.make_async_copy(_row_block(ybuf_ref.at[g % 2], 0, rows),
                                     _row_block(ys_ref, trow_ref[g], rows), osem.at[g % 2])

    def by_size(g, fn):
        pl.when(tunits_ref[g] == T_MOE // MOE_UNIT)(lambda: fn(T_MOE))
        pl.when(tunits_ref[g] == 1)(lambda: fn(MOE_UNIT))

    @pl.when((b == 0) & (ntot > 0))
    def _():
        by_size(0, lambda rows: load(0, rows).start())

    def body(j, carry):
        g = g0 + j

        @pl.when(g + 1 < ntot)
        def _():
            by_size(g + 1, lambda rows: load(g + 1, rows).start())

        @pl.when(g >= 2)
        def _():
            by_size(g - 2, lambda rows: store(g - 2, rows).wait())

        def run(rows):
            load(g, rows).wait()
            y = compute(_load_row_major(xbuf_ref.at[g % 2], rows))
            _store_row_major(ybuf_ref.at[g % 2], y)
            store(g, rows).start()

        by_size(g, run)
        return carry

    lax.fori_loop(0, bnt_ref[b], body, 0)

    @pl.when(b == N_BUCKETS - 1)
    def _():
        for back in (2, 1):
            @pl.when(ntot >= back)
            def _():
                by_size(ntot - back, lambda rows: store(ntot - back, rows).wait())


def _experts(xs, i, cnt, gffn, wr, br, wg, wu, wd):
    e0 = i * MOE_EXPERTS
    max_tiles = xs.shape[0] // ROW_SUB // T_MOE + N_BUCKETS
    bg = jnp.asarray([q // N_PAIRS for q in range(N_BUCKETS)], jnp.int32)
    bea = jnp.asarray([(q // N_PAIRS) * MOE_EPG + PAIRS[q % N_PAIRS][0] for q in range(N_BUCKETS)],
                      jnp.int32)
    beb = jnp.asarray([(q // N_PAIRS) * MOE_EPG + PAIRS[q % N_PAIRS][1] for q in range(N_BUCKETS)],
                      jnp.int32)
    ea_map = lambda b, g, ea, eb: (e0 + ea[b], 0, 0)
    eb_map = lambda b, g, ea, eb: (e0 + eb[b], 0, 0)
    up_spec = lambda m: pl.BlockSpec((None, D_MODEL, MOE_FF), m)
    down_spec = lambda m: pl.BlockSpec((None, MOE_FF, D_MODEL), m)
    grid_spec = pltpu.PrefetchScalarGridSpec(
        num_scalar_prefetch=3,
        grid=(N_BUCKETS,),
        in_specs=[
            pl.BlockSpec(memory_space=pltpu.SMEM),
            pl.BlockSpec(memory_space=pl.ANY),
            pl.BlockSpec((None, 1, D_MODEL), lambda b, *_: (i, 0, 0)),
            pl.BlockSpec((None, D_MODEL, LANES), lambda b, *_: (i, 0, 0)),
            pl.BlockSpec((None, 1, LANES), lambda b, *_: (i, 0, 0)),
            up_spec(ea_map), up_spec(ea_map), down_spec(ea_map),
            up_spec(eb_map), up_spec(eb_map), down_spec(eb_map),
        ],
        out_specs=pl.BlockSpec(memory_space=pl.ANY),
        scratch_shapes=[
            pltpu.VMEM((D_MODEL, MOE_FF), BF16), pltpu.VMEM((D_MODEL, MOE_FF), BF16),
            pltpu.VMEM((MOE_FF, D_MODEL), BF16),
            pltpu.VMEM((D_MODEL, MOE_FF), BF16), pltpu.VMEM((D_MODEL, MOE_FF), BF16),
            pltpu.VMEM((MOE_FF, D_MODEL), BF16),
            pltpu.VMEM((2, T_MOE * ROW_SUB, LANES), F32),
            pltpu.VMEM((2, T_MOE * ROW_SUB, LANES), F32),
            pltpu.SemaphoreType.DMA((2,)),
            pltpu.SemaphoreType.DMA((2,)),
            pltpu.SMEM((N_BUCKETS,), jnp.int32), pltpu.SMEM((N_BUCKETS,), jnp.int32),
            pltpu.SMEM((max_tiles,), jnp.int32), pltpu.SMEM((max_tiles,), jnp.int32),
            pltpu.SMEM((1,), jnp.int32),
        ],
    )
    return pl.pallas_call(
        _experts_kernel,
        grid_spec=grid_spec,
        out_shape=jax.ShapeDtypeStruct(xs.shape, F32),
        input_output_aliases={4: 0},
        compiler_params=_params(("arbitrary",)),
        name="moe_experts",
    )(bg, bea, beb, cnt, xs, gffn, wr, br, wg, wu, wd, wg, wu, wd)


def _final_kernel(dcur_ref, dnext_ref, h_ref, p_ref, ys_ref, gple_ref, wup_ref, wgate_ref,
                  gfin_ref, o_ref, wupb_ref, wgateb_ref, ybuf_ref, sem):
    t = pl.program_id(0)
    _gather_start(t, dcur_ref, dnext_ref, ys_ref, ybuf_ref, sem)(0, 1)

    @pl.when(t == 0)
    def _():
        _cast_weight(wupb_ref, wup_ref)
        _cast_weight(wgateb_ref, wgate_ref)

    h2 = h_ref[...] + _gather_finish(t, ys_ref, ybuf_ref, sem)
    o_ref[...] = _rms(_ple(h2, p_ref, gple_ref, wupb_ref, wgateb_ref), gfin_ref[...])
    _gather_drain(t, pl.num_programs(0), ys_ref, ybuf_ref, sem)


def _final_layer(h, dest3, p2, ys, i, gple, wup, wgate, gfin):
    n = h.shape[0]
    nt = n // T_ROW
    return pl.pallas_call(
        _final_kernel,
        grid=(nt,),
        in_specs=_finish_specs(i, nt) + [pl.BlockSpec((1, D_MODEL), lambda t: (0, 0))],
        out_specs=pl.BlockSpec((T_ROW, D_MODEL), lambda t: (t, 0)),
        out_shape=jax.ShapeDtypeStruct((n, D_MODEL), F32),
        scratch_shapes=_FINISH_SCRATCH,
        compiler_params=_params(("arbitrary",)),
        name="final_combine_ple",
    )(dest3, dest3, h, p2, ys, gple, wup, wgate, gfin)


def _sorted_rows(n):
    return n + N_BUCKETS * MOE_UNIT


def kernel(x, p, gla_w_in, gla_w_gate_up, gla_b_gate, gla_g_head, gla_w_out, sgu_w_in, sgu_ln_g,
           sgu_ln_b, sgu_w_s, sgu_b_s, sgu_w_out, norm_mix_g, norm_ffn_g, norm_ple_g,
           moe_w_route_group, moe_b_route_group, moe_w_route_expert, moe_b_route_expert,
           moe_w_gate, moe_w_up, moe_w_down, ple_w_up, ple_w_gate, final_norm_g):
    batch, seq, d = x.shape
    assert d == D_MODEL and seq % T_MIX == 0
    n = batch * seq
    assert n % T_ROUTE == 0
    depth = p.shape[0]
    assert depth == DEPTH
    na, nb = gla_w_in.shape[0], sgu_w_in.shape[0]

    gla_win = jnp.pad(gla_w_in, ((0, 0), (0, 0), (0, GLA_A_PAD - GLA_RANK)))
    gla_wgu = jnp.pad(gla_w_gate_up, ((0, 0), (0, GLA_A_PAD - GLA_RANK), (0, 0)))
    gla_bg = gla_b_gate.reshape(na, 1, GLA_DK)
    gla_gh = gla_g_head.reshape(na, 1, GLA_DV)
    sgu_lng = sgu_ln_g.reshape(nb, 1, SGU_WIDTH)
    sgu_lnb = sgu_ln_b.reshape(nb, 1, SGU_WIDTH)
    sgu_bs = jnp.repeat(jnp.transpose(sgu_b_s, (0, 2, 1)), SGU_GC, axis=-1)
    g_mix = norm_mix_g.reshape(depth, 1, D_MODEL)
    g_ffn = norm_ffn_g.reshape(depth, 1, D_MODEL)
    g_ple = norm_ple_g.reshape(depth, 1, D_MODEL)
    g_fin = final_norm_g.reshape(1, D_MODEL)
    w_route = jnp.concatenate(
        [moe_w_route_group, moe_w_route_expert.reshape(depth, D_MODEL, MOE_EXPERTS)], axis=-1)
    b_route = jnp.concatenate(
        [moe_b_route_group, moe_b_route_expert.reshape(depth, MOE_EXPERTS)], axis=-1)
    n_logits = MOE_GROUPS + MOE_EXPERTS
    wr = jnp.pad(w_route, ((0, 0), (0, 0), (0, LANES - n_logits))).astype(BF16)
    br = jnp.pad(b_route, ((0, 0), (0, LANES - n_logits))).reshape(depth, 1, LANES)
    wrt = jnp.pad(jnp.transpose(w_route, (0, 2, 1)),
                  ((0, 0), (0, ROUTE_ROWS - n_logits), (0, 0))).astype(BF16)
    brt = jnp.broadcast_to(
        jnp.pad(b_route, ((0, 0), (0, ROUTE_ROWS - n_logits)))[:, :, None],
        (depth, ROUTE_ROWS, T_ROUTE))
    triu = jnp.triu(jnp.ones((T_ROUTE, T_ROUTE), BF16), k=1)
    wg = moe_w_gate.reshape(depth * MOE_EXPERTS, D_MODEL, MOE_FF)
    wu = moe_w_up.reshape(depth * MOE_EXPERTS, D_MODEL, MOE_FF)
    wd = moe_w_down.reshape(depth * MOE_EXPERTS, MOE_FF, D_MODEL)
    p2 = p.reshape(depth * n, PLE_DIM)

    ns = seq // T_MIX
    h = x.reshape(n, D_MODEL)
    finish = None
    for i in range(depth):
        if i % 2 == 0:
            kind, params = "gla", (g_mix, gla_win, gla_wgu, gla_bg, gla_gh, gla_w_out)
        else:
            kind, params = "sgu", (g_mix, sgu_w_in, sgu_lng, sgu_lnb, sgu_w_s, sgu_bs, sgu_w_out)
        h = _mixer_layer(kind, i, ns, h, finish, params)
        route, counts = _router(h, i, g_ffn, wrt, brt, triu)
        cnt = counts[:N_BUCKETS, 0]
        padded = (cnt + (MOE_UNIT - 1)) // MOE_UNIT * MOE_UNIT
        starts = jnp.cumsum(padded) - padded
        dest3 = (starts[route[0]] + route[1]).reshape(n // T_ROW, 1, T_ROW)
        xs = _dispatch(h, dest3, cnt)
        ys = _experts(xs, i, cnt, g_ffn, wr, br, wg, wu, wd)
        finish = (dest3, p2, ys, (g_ple, ple_w_up, ple_w_gate))
    h = _final_layer(h, dest3, p2, ys, depth - 1, g_ple, ple_w_up, ple_w_gate, g_fin)
    return h.reshape(batch, seq, D_MODEL)
```

```python
import functools

import jax
import jax.numpy as jnp
from jax import lax
from jax.experimental import pallas as pl
from jax.experimental.pallas import tpu as pltpu

F32 = jnp.float32
BF16 = jnp.bfloat16

D_MODEL = 1024
DEPTH = 4
GLA_HEADS = 4
GLA_DK = 512
GLA_DV = 1024
GLA_HK = GLA_DK // GLA_HEADS
GLA_HV = GLA_DV // GLA_HEADS
GLA_RANK = 16
GLA_TAU = 16.0
GLA_CHUNK = 64
SGU_WIDTH = 2048
SGU_GROUPS = 8
SGU_GC = SGU_WIDTH // SGU_GROUPS
SGU_CHUNK = 128
MOE_GROUPS = 4
MOE_EPG = 4
MOE_EXPERTS = MOE_GROUPS * MOE_EPG
MOE_FF = 512
PLE_DIM = 256
EPS = 1e-6

LANES = 128
ROW_SUB = D_MODEL // LANES
GLA_A_PAD = LANES
GLA_ZW = 2 * GLA_DK + 2 * GLA_DV + GLA_A_PAD
ROUTE_ROWS = 32
ROUTE_OUT_ROWS = 8
PAIRS = ((0, 1), (0, 2), (1, 2), (1, 3), (2, 3), (0, 3))
N_PAIRS = len(PAIRS)
N_BUCKETS = MOE_GROUPS * N_PAIRS

T_MIX = 256
SGU_VBLK = 512
T_ROW = T_MIX
T_MOE = 256
MOE_UNIT = 128
MOE_UNIT_LOG2 = 7
assert 1 << MOE_UNIT_LOG2 == MOE_UNIT and T_MOE == 2 * MOE_UNIT
DMA_UNROLL = 8
CAST_ROWS = 256
STAGE_ROWS = 64
VMEM_LIMIT = 56 * 1024 * 1024


def _rms(x, g):
    ms = jnp.mean(x * x, axis=-1, keepdims=True)
    return x * lax.rsqrt(ms + EPS) * g


def _dot(a, b):
    return jnp.dot(a, b, preferred_element_type=F32)


def _dot_nt(a, b):
    return lax.dot_general(a, b, (((1,), (1,)), ((), ())), preferred_element_type=F32)


def _dot_tn(a, b):
    return lax.dot_general(a, b, (((0,), (0,)), ((), ())), preferred_element_type=F32)


def _params(sem):
    return pltpu.CompilerParams(dimension_semantics=sem, vmem_limit_bytes=VMEM_LIMIT)


def _resident(block_shape, index_map):
    return pl.BlockSpec(block_shape, index_map, pipeline_mode=pl.Buffered(1))


def _cast_weight(dst_ref, src_ref):
    rows = src_ref.shape[0]
    for r in range(0, rows, CAST_ROWS):
        dst_ref[r:r + CAST_ROWS, :] = src_ref[r:r + CAST_ROWS, :].astype(BF16)


def _store_row_major(dst_ref, x):
    m = x.shape[0]
    for c in range(ROW_SUB):
        dst_ref[pl.ds(c, m, stride=ROW_SUB), :] = x[:, c * LANES:(c + 1) * LANES]


def _load_row_major(src_ref, m):
    return jnp.concatenate(
        [src_ref[pl.ds(c, m, stride=ROW_SUB), :] for c in range(ROW_SUB)], axis=-1)


def _row_block(ref, row, rows):
    start = row * ROW_SUB
    if not isinstance(start, int):
        start = pl.multiple_of(start, ROW_SUB)
    return ref.at[pl.ds(start, rows * ROW_SUB), :]


def _stage_weight(dst_ref, w_hbm_ref, stage_ref, sem):
    rows, ch = dst_ref.shape[0], stage_ref.shape[1]

    def chunk(k):
        return pltpu.make_async_copy(w_hbm_ref.at[pl.ds(k * ch, ch), :], stage_ref.at[k % 2],
                                     sem.at[k % 2])

    chunk(0).start()
    for k in range(rows // ch):
        if (k + 1) * ch < rows:
            chunk(k + 1).start()
        chunk(k).wait()
        dst_ref[k * ch:(k + 1) * ch, :] = stage_ref[k % 2].astype(BF16)


def _units(count):
    return lax.shift_right_logical(count + (MOE_UNIT - 1), MOE_UNIT_LOG2)


def _bucket_layout(cnt_ref, starts_ref):
    def body(b, row):
        starts_ref[b] = row
        return row + _units(cnt_ref[b, 0]) * MOE_UNIT
    return lax.fori_loop(0, N_BUCKETS, body, jnp.int32(0))


def _gather_start(t, dcur_ref, dnext_ref, ys_ref, ybuf_ref, sem):
    def row_copy(dref, slot, r):
        return pltpu.make_async_copy(_row_block(ys_ref, dref[0, 0, r], 1),
                                     _row_block(ybuf_ref.at[slot], r, 1), sem.at[slot])

    @pl.when(t == 0)
    def _():
        def body(k, carry):
            base = pl.multiple_of(k * DMA_UNROLL, DMA_UNROLL)
            for u in range(DMA_UNROLL):
                row_copy(dcur_ref, 0, base + u).start()
            return carry
        lax.fori_loop(0, T_ROW // DMA_UNROLL, body, 0)

    def issue_next(k, n):
        for r in range(k * T_ROW // n, (k + 1) * T_ROW // n):
            row_copy(dnext_ref, (t + 1) % 2, r).start()

    return issue_next


def _gather_wait(slot, ys_ref, ybuf_ref, sem):
    pltpu.make_async_copy(_row_block(ys_ref, 0, T_ROW), ybuf_ref.at[slot], sem.at[slot]).wait()


def _gather_finish(t, ys_ref, ybuf_ref, sem):
    _gather_wait(t % 2, ys_ref, ybuf_ref, sem)
    return _load_row_major(ybuf_ref.at[t % 2], T_ROW)


def _gather_drain(t, nt, ys_ref, ybuf_ref, sem):
    @pl.when(t == nt - 1)
    def _():
        _gather_wait((t + 1) % 2, ys_ref, ybuf_ref, sem)


def _ple(h2, p_ref, gple_ref, wupb_ref, wgateb_ref):
    pn = _rms(h2, gple_ref[...]).astype(BF16)
    gate = jax.nn.sigmoid(_dot(pn, wgateb_ref[...]))
    up = _dot(p_ref[...].astype(BF16), wupb_ref[...])
    return h2 + up * gate


def _finish_specs(i, nt):
    return [
        pl.BlockSpec((1, 1, T_ROW), lambda t: (t, 0, 0), memory_space=pltpu.SMEM),
        pl.BlockSpec((1, 1, T_ROW), lambda t: (jnp.minimum(t + 1, nt - 1), 0, 0),
                     memory_space=pltpu.SMEM),
        pl.BlockSpec((T_ROW, D_MODEL), lambda t: (t, 0)),
        pl.BlockSpec((T_ROW, PLE_DIM), lambda t: (i * nt + t, 0)),
        pl.BlockSpec(memory_space=pl.ANY),
        pl.BlockSpec((None, 1, D_MODEL), lambda t: (i, 0, 0)),
        _resident((None, PLE_DIM, D_MODEL), lambda t: (i, 0, 0)),
        _resident((None, D_MODEL, D_MODEL), lambda t: (i, 0, 0)),
    ]


N_FINISH_INPUTS = 8

_FINISH_SCRATCH = [
    pltpu.VMEM((PLE_DIM, D_MODEL), BF16),
    pltpu.VMEM((D_MODEL, D_MODEL), BF16),
    pltpu.VMEM((2, T_ROW * ROW_SUB, LANES), F32),
    pltpu.SemaphoreType.DMA((2,)),
]


def _gla_body(h, t, ns, tick, gmix_ref, wgu_ref, bg_ref, ghead_ref, winb_ref, woutb_ref,
              st_ref, oacc_ref, y_ref):
    @pl.when(t % ns == 0)
    def _():
        st_ref[...] = jnp.zeros_like(st_ref)

    hn = _rms(h, gmix_ref[...]).astype(BF16)
    z = _dot(hn, winb_ref[...])
    a = z[:, 2 * GLA_DK + 2 * GLA_DV:].astype(BF16)
    pre = _dot(a, wgu_ref[...].astype(BF16)) + bg_ref[...]
    la = (jnp.minimum(pre, 0.0) - jnp.log1p(jnp.exp(-jnp.abs(pre)))) * (1.0 / GLA_TAU)

    C = GLA_CHUNK
    ri = lax.broadcasted_iota(jnp.int32, (C, C), 0)
    ci = lax.broadcasted_iota(jnp.int32, (C, C), 1)
    causal = ri >= ci
    tri = causal.astype(BF16)

    for c in range(T_MIX // C):
        rows = slice(c * C, (c + 1) * C)
        la_c = la[rows]
        hi = la_c.astype(BF16)
        lo = (la_c - hi.astype(F32)).astype(BF16)
        bc = _dot(tri, hi) + _dot(tri, lo)
        bend = bc[C - 1:C, :]
        kc = z[rows, GLA_DK:2 * GLA_DK]
        qd = (z[rows, 0:GLA_DK] * (GLA_HK ** -0.5) * jnp.exp(bc)).astype(BF16)
        kd = (kc * jnp.exp(-bc)).astype(BF16)
        ke = (kc * jnp.exp(bend - bc)).astype(BF16)
        vc = z[rows, 2 * GLA_DK:2 * GLA_DK + GLA_DV].astype(BF16)
        dec = jnp.exp(bend)
        for hd in range(GLA_HEADS):
            ks = slice(hd * GLA_HK, (hd + 1) * GLA_HK)
            vs = slice(hd * GLA_HV, (hd + 1) * GLA_HV)
            att = jnp.where(causal, _dot_nt(qd[:, ks], kd[:, ks]), 0.0).astype(BF16)
            st = st_ref[hd]
            o = _dot(att, vc[:, vs]) + _dot_nt(qd[:, ks], st.astype(BF16))
            st_ref[hd] = st * dec[:, ks] + _dot_tn(vc[:, vs], ke[:, ks])
            oacc_ref[rows, vs] = o
            tick(c * GLA_HEADS + hd, (T_MIX // C) * GLA_HEADS)

    r0 = 2 * GLA_DK + GLA_DV
    for hd in range(GLA_HEADS):
        vs = slice(hd * GLA_HV, (hd + 1) * GLA_HV)
        on = _rms(oacc_ref[:, vs], ghead_ref[:, vs])
        r = z[:, r0 + hd * GLA_HV:r0 + (hd + 1) * GLA_HV]
        y_ref[:, vs] = (jax.nn.silu(r) * on).astype(BF16)
    return h + _dot(y_ref[...], woutb_ref[...])


def _sgu_body(h, tick, gmix_ref, lng_ref, lnb_ref, bs_ref, winb_ref, woutb_ref, wsb_ref, us_ref,
              v_ref):
    C = SGU_CHUNK
    hn = _rms(h, gmix_ref[...]).astype(BF16)

    vsum = jnp.zeros((T_MIX, 1), F32)
    for k in range(SGU_WIDTH // SGU_VBLK):
        cols = slice(k * SGU_VBLK, (k + 1) * SGU_VBLK)
        vb = jax.nn.gelu(_dot(hn, winb_ref[:, SGU_WIDTH + k * SGU_VBLK:SGU_WIDTH + (k + 1) * SGU_VBLK]),
                         approximate=True)
        v_ref[:, cols] = vb
        vsum = vsum + jnp.sum(vb, axis=-1, keepdims=True)
    mean = vsum * (1.0 / SGU_WIDTH)
    vsq = jnp.zeros((T_MIX, 1), F32)
    for k in range(SGU_WIDTH // SGU_VBLK):
        cols = slice(k * SGU_VBLK, (k + 1) * SGU_VBLK)
        xc = v_ref[:, cols] - mean
        vsq = vsq + jnp.sum(xc * xc, axis=-1, keepdims=True)
    rstd = lax.rsqrt(vsq * (1.0 / SGU_WIDTH) + EPS)

    for g in range(SGU_GROUPS):
        cols = slice(g * SGU_GC, (g + 1) * SGU_GC)
        u = jax.nn.gelu(_dot(hn, winb_ref[:, cols]), approximate=True)
        vn = (((v_ref[:, cols] - mean) * rstd) * lng_ref[:, cols] + lnb_ref[:, cols]).astype(BF16)
        wc = wsb_ref[g]
        for c in range(T_MIX // C):
            rows = slice(c * C, (c + 1) * C)
            s = _dot(wc, vn[rows]) + bs_ref[:, cols]
            us_ref[rows, cols] = (u[rows] * s).astype(BF16)
        tick(g, SGU_GROUPS)
    return h + _dot(us_ref[...], woutb_ref[...])


def _mixer_kernel(*refs, kind, fused, j, ns):
    refs = list(refs)
    t = pl.program_id(0)
    nt = pl.num_programs(0)
    if fused:
        (dcur_ref, dnext_ref, hprev_ref, p_ref, ys_ref, gple_ref, wup_ref,
         wgate_ref) = refs[:N_FINISH_INPUTS]
        wupb_ref, wgateb_ref, ybuf_ref, gsem = refs[-len(_FINISH_SCRATCH):]
        refs = refs[N_FINISH_INPUTS:-len(_FINISH_SCRATCH)]
        tick = _gather_start(t, dcur_ref, dnext_ref, ys_ref, ybuf_ref, gsem)
    else:
        h_ref, refs = refs[0], refs[1:]
        tick = lambda k, n: None
    cnt_ref, route_ref = refs[-2:]
    refs = refs[:-2]
    if kind == "gla":
        (gmix_ref, win_hbm, wgu_ref, bg_ref, ghead_ref, wout_ref,
         gffn_ref, wrt_ref, brt_ref, triu_ref, o_ref, dest_ref, counts_ref,
         winb_ref, woutb_ref, stage_ref, ssem, st_ref, oacc_ref, y_ref) = refs
    else:
        (gmix_ref, win_hbm, lng_ref, lnb_ref, ws_ref, bs_ref, wout_ref,
         gffn_ref, wrt_ref, brt_ref, triu_ref, o_ref, dest_ref, counts_ref,
         winb_ref, woutb_ref, wsb_ref, stage_ref, ssem, us_ref, v_ref) = refs

    @pl.when(t == 0)
    def _():
        _stage_weight(winb_ref, win_hbm.at[j], stage_ref, ssem)
        _cast_weight(woutb_ref, wout_ref)
        if fused:
            _cast_weight(wupb_ref, wup_ref)
            _cast_weight(wgateb_ref, wgate_ref)
        if kind == "sgu":
            ri = lax.broadcasted_iota(jnp.int32, (SGU_CHUNK, SGU_CHUNK), 0)
            ci = lax.broadcasted_iota(jnp.int32, (SGU_CHUNK, SGU_CHUNK), 1)
            for g in range(SGU_GROUPS):
                wsb_ref[g] = jnp.where(ri >= ci, ws_ref[g], 0.0).astype(BF16)

    if fused:
        h2 = hprev_ref[...] + _gather_finish(t, ys_ref, ybuf_ref, gsem)
        h = _ple(h2, p_ref, gple_ref, wupb_ref, wgateb_ref)
    else:
        h = h_ref[...]
    if kind == "gla":
        out = _gla_body(h, t, ns, tick, gmix_ref, wgu_ref, bg_ref, ghead_ref, winb_ref,
                        woutb_ref, st_ref, oacc_ref, y_ref)
    else:
        out = _sgu_body(h, tick, gmix_ref, lng_ref, lnb_ref, bs_ref, winb_ref, woutb_ref,
                        wsb_ref, us_ref, v_ref)
    o_ref[...] = out
    _route_tile(out, t, gffn_ref, wrt_ref, brt_ref, triu_ref, cnt_ref, route_ref)

    @pl.when(t == nt - 1)
    def _():
        _emit_dest(nt, cnt_ref, route_ref, dest_ref, counts_ref)
    if fused:
        _gather_drain(t, nt, ys_ref, ybuf_ref, gsem)


def _mixer_layer(kind, i, ns, h, finish, params, router):
    n = h.shape[0]
    nt = n // T_MIX
    j = i // 2
    tile = lambda t: (t, 0)
    if kind == "gla":
        zw = GLA_ZW
        mixer_specs = [
            pl.BlockSpec((None, 1, D_MODEL), lambda t: (i, 0, 0)),
            pl.BlockSpec(memory_space=pl.ANY),
            pl.BlockSpec((None, GLA_A_PAD, GLA_DK), lambda t: (j, 0, 0)),
            pl.BlockSpec((None, 1, GLA_DK), lambda t: (j, 0, 0)),
            pl.BlockSpec((None, 1, GLA_DV), lambda t: (j, 0, 0)),
            _resident((None, GLA_DV, D_MODEL), lambda t: (j, 0, 0)),
        ]
        mixer_scratch = [
            pltpu.VMEM((D_MODEL, zw), BF16),
            pltpu.VMEM((GLA_DV, D_MODEL), BF16),
            pltpu.VMEM((2, STAGE_ROWS, zw), F32),
            pltpu.SemaphoreType.DMA((2,)),
            pltpu.VMEM((GLA_HEADS, GLA_HV, GLA_HK), F32),
            pltpu.VMEM((T_MIX, GLA_DV), F32),
            pltpu.VMEM((T_MIX, GLA_DV), BF16),
        ]
    else:
        zw = 2 * SGU_WIDTH
        mixer_specs = [
            pl.BlockSpec((None, 1, D_MODEL), lambda t: (i, 0, 0)),
            pl.BlockSpec(memory_space=pl.ANY),
            pl.BlockSpec((None, 1, SGU_WIDTH), lambda t: (j, 0, 0)),
            pl.BlockSpec((None, 1, SGU_WIDTH), lambda t: (j, 0, 0)),
            _resident((None, SGU_GROUPS, SGU_CHUNK, SGU_CHUNK), lambda t: (j, 0, 0, 0)),
            _resident((None, SGU_CHUNK, SGU_WIDTH), lambda t: (j, 0, 0)),
            _resident((None, SGU_WIDTH, D_MODEL), lambda t: (j, 0, 0)),
        ]
        mixer_scratch = [
            pltpu.VMEM((D_MODEL, zw), BF16),
            pltpu.VMEM((SGU_WIDTH, D_MODEL), BF16),
            pltpu.VMEM((SGU_GROUPS, SGU_CHUNK, SGU_CHUNK), BF16),
            pltpu.VMEM((2, STAGE_ROWS, zw), F32),
            pltpu.SemaphoreType.DMA((2,)),
            pltpu.VMEM((T_MIX, SGU_WIDTH), BF16),
            pltpu.VMEM((T_MIX, SGU_WIDTH), F32),
        ]
    if finish is None:
        head_specs = [pl.BlockSpec((T_MIX, D_MODEL), tile)]
        head_args = (h,)
        tail_scratch = []
    else:
        dest3, p2, ys, (gple, wup, wgate) = finish
        head_specs = _finish_specs(i - 1, nt)
        head_args = (dest3, dest3, h, p2, ys, gple, wup, wgate)
        tail_scratch = _FINISH_SCRATCH
    router_specs = [
        pl.BlockSpec((None, 1, D_MODEL), lambda t: (i, 0, 0)),
        pl.BlockSpec((None, ROUTE_ROWS, D_MODEL), lambda t: (i, 0, 0)),
        pl.BlockSpec((None, ROUTE_ROWS, T_MIX), lambda t: (i, 0, 0)),
        pl.BlockSpec((T_MIX, T_MIX), lambda t: (0, 0)),
    ]
    router_scratch = [
        pltpu.VMEM((ROUTE_ROWS, LANES), F32),
        pltpu.VMEM((nt, ROUTE_OUT_ROWS, T_MIX), jnp.int32),
    ]
    return pl.pallas_call(
        functools.partial(_mixer_kernel, kind=kind, fused=finish is not None, j=j, ns=ns),
        grid=(nt,),
        in_specs=head_specs + mixer_specs + router_specs,
        out_specs=[
            pl.BlockSpec((T_MIX, D_MODEL), tile),
            pl.BlockSpec((nt, 1, T_MIX), lambda t: (0, 0, 0)),
            pl.BlockSpec((ROUTE_ROWS, LANES), lambda t: (0, 0)),
        ],
        out_shape=[
            jax.ShapeDtypeStruct((n, D_MODEL), F32),
            jax.ShapeDtypeStruct((nt, 1, T_MIX), jnp.int32),
            jax.ShapeDtypeStruct((ROUTE_ROWS, LANES), jnp.int32),
        ],
        scratch_shapes=mixer_scratch + router_scratch + tail_scratch,
        compiler_params=_params(("arbitrary",)),
        name=kind + ("_fused" if finish is not None else "") + "_mixer",
    )(*head_args, *params, *router)


def _first_argmax(rows):
    best = rows[0]
    idx = jnp.zeros(best.shape, jnp.int32)
    for j in range(1, len(rows)):
        upd = rows[j] > best
        idx = jnp.where(upd, j, idx)
        best = jnp.where(upd, rows[j], best)
    return idx, best


def _route_tile(h, t, gffn_ref, wrt_ref, brt_ref, triu_ref, cnt_ref, route_ref):
    @pl.when(t == 0)
    def _():
        cnt_ref[...] = jnp.zeros_like(cnt_ref)

    hb = _rms(h, gffn_ref[...]).astype(BF16)
    lg = _dot_nt(wrt_ref[...], hb) + brt_ref[...]
    row = lambda j: lg[j:j + 1, :]
    g, _ = _first_argmax([row(j) for j in range(MOE_GROUPS)])
    ex = []
    for e in range(MOE_EPG):
        v = row(MOE_GROUPS + e)
        for gg in range(1, MOE_GROUPS):
            v = jnp.where(g == gg, row(MOE_GROUPS + gg * MOE_EPG + e), v)
        ex.append(v)
    i1, _ = _first_argmax(ex)
    neg = jnp.full(ex[0].shape, -jnp.inf, F32)
    i2, _ = _first_argmax([jnp.where(i1 == e, neg, ex[e]) for e in range(MOE_EPG)])
    lo = jnp.minimum(i1, i2)
    hi = jnp.maximum(i1, i2)
    pair = jnp.zeros(lo.shape, jnp.int32)
    for p, (a, b) in enumerate(PAIRS):
        pair = jnp.where((lo == a) & (hi == b), p, pair)
    bucket = g * N_PAIRS + pair

    m = bucket.shape[1]
    onehot = (lax.broadcasted_iota(jnp.int32, (ROUTE_ROWS, m), 0) == bucket).astype(F32)
    before = _dot(onehot.astype(BF16), triu_ref[...])
    rank = jnp.sum(onehot * (before + cnt_ref[:, 0:1]), axis=0, keepdims=True).astype(jnp.int32)
    cnt_ref[...] = cnt_ref[...] + jnp.sum(onehot, axis=1, keepdims=True)
    r8 = lax.broadcasted_iota(jnp.int32, (ROUTE_OUT_ROWS, m), 0)
    route_ref[t] = jnp.where(r8 == 0, bucket, jnp.where(r8 == 1, rank, 0))


def _emit_dest(nt, cnt_ref, route_ref, dest_ref, counts_ref):
    cnt = cnt_ref[...].astype(jnp.int32)
    counts_ref[...] = cnt
    units = _units(cnt).astype(F32).astype(BF16)
    ri = lax.broadcasted_iota(jnp.int32, (ROUTE_ROWS, ROUTE_ROWS), 0)
    ci = lax.broadcasted_iota(jnp.int32, (ROUTE_ROWS, ROUTE_ROWS), 1)
    before = _dot((ci < ri).astype(BF16), units)
    start = (before[:, 0:1] * MOE_UNIT).astype(jnp.int32)

    def body(k, carry):
        blk = route_ref[k]
        onehot = lax.broadcasted_iota(jnp.int32, (ROUTE_ROWS, T_MIX), 0) == blk[0:1, :]
        dest_ref[k] = jnp.sum(jnp.where(onehot, start, 0), axis=0, keepdims=True) + blk[1:2, :]
        return carry

    lax.fori_loop(0, nt, body, 0)


def _dispatch_kernel(cnt_ref, dest_ref, h_ref, xs_ref, zero_ref, rows_ref, starts_ref, zsem, sem):
    @pl.when(pl.program_id(0) == 0)
    def _():
        end = _bucket_layout(cnt_ref, starts_ref)
        zero_ref[...] = jnp.zeros_like(zero_ref)
        fills = []
        for b in range(N_BUCKETS):
            c = cnt_ref[b, 0]
            whole = lax.shift_right_logical(c, MOE_UNIT_LOG2) * MOE_UNIT
            fills.append(((c & (MOE_UNIT - 1)) != 0, starts_ref[b] + whole))
        for k in range(N_BUCKETS):
            fills.append((end + k * MOE_UNIT < xs_ref.shape[0] // ROW_SUB, end + k * MOE_UNIT))

        def fill(row):
            return pltpu.make_async_copy(zero_ref, _row_block(xs_ref, row, MOE_UNIT), zsem)

        for on, row in fills:
            pl.when(on)(lambda row=row: fill(row).start())
        for on, row in fills:
            pl.when(on)(lambda row=row: fill(row).wait())

    _store_row_major(rows_ref, h_ref[...])

    def issue(k, carry):
        base = pl.multiple_of(k * DMA_UNROLL, DMA_UNROLL)
        for u in range(DMA_UNROLL):
            d = dest_ref[0, 0, base + u]
            pltpu.make_async_copy(_row_block(rows_ref, base + u, 1),
                                  _row_block(xs_ref, d, 1), sem).start(priority=u % 2)
        return carry

    lax.fori_loop(0, T_ROW // DMA_UNROLL, issue, 0)
    pltpu.make_async_copy(rows_ref, _row_block(xs_ref, 0, T_ROW), sem).wait()


def _dispatch(h, dest3, cnt):
    n = h.shape[0]
    return pl.pallas_call(
        _dispatch_kernel,
        grid=(n // T_ROW,),
        in_specs=[
            pl.BlockSpec(memory_space=pltpu.SMEM),
            pl.BlockSpec((1, 1, T_ROW), lambda t: (t, 0, 0), memory_space=pltpu.SMEM),
            pl.BlockSpec((T_ROW, D_MODEL), lambda t: (t, 0)),
        ],
        out_specs=pl.BlockSpec(memory_space=pl.ANY),
        out_shape=jax.ShapeDtypeStruct((_sorted_rows(n) * ROW_SUB, LANES), F32),
        scratch_shapes=[
            pltpu.VMEM((MOE_UNIT * ROW_SUB, LANES), F32),
            pltpu.VMEM((T_ROW * ROW_SUB, LANES), F32),
            pltpu.SMEM((N_BUCKETS,), jnp.int32),
            pltpu.SemaphoreType.DMA(()),
            pltpu.SemaphoreType.DMA(()),
        ],
        compiler_params=_params(("arbitrary",)),
        name="moe_dispatch",
    )(cnt, dest3, h)


def _experts_kernel(bg_ref, bea_ref, beb_ref,
                    cnt_ref, xs_ref, gffn_ref, wr_ref, br_ref,
                    wga_ref, wua_ref, wda_ref, wgb_ref, wub_ref, wdb_ref,
                    ys_ref,
                    wgab_ref, wuab_ref, wdab_ref, wgbb_ref, wubb_ref, wdbb_ref,
                    xbuf_ref, ybuf_ref, isem, osem,
                    bg0_ref, bnt_ref, trow_ref, tunits_ref, ntot_ref):
    b = pl.program_id(0)
    pair = b % N_PAIRS

    @pl.when(b == 0)
    def _():
        def per_bucket(q, carry):
            row, g = carry
            units = _units(cnt_ref[q, 0])
            n_full = lax.shift_right_logical(units, 1)
            n_tiles = n_full + (units & 1)
            bg0_ref[q] = g
            bnt_ref[q] = n_tiles

            def per_tile(j, c):
                trow_ref[g + j] = row + j * T_MOE
                tunits_ref[g + j] = jnp.where(j < n_full, T_MOE // MOE_UNIT, 1)
                return c

            lax.fori_loop(0, n_tiles, per_tile, 0)
            return row + units * MOE_UNIT, g + n_tiles

        _, n_total = lax.fori_loop(0, N_BUCKETS, per_bucket, (jnp.int32(0), jnp.int32(0)))
        ntot_ref[0] = n_total

    def cast_a():
        _cast_weight(wgab_ref, wga_ref)
        _cast_weight(wuab_ref, wua_ref)
        _cast_weight(wdab_ref, wda_ref)

    def cast_b():
        _cast_weight(wgbb_ref, wgb_ref)
        _cast_weight(wubb_ref, wub_ref)
        _cast_weight(wdbb_ref, wdb_ref)

    a_changes = [q for q in range(N_PAIRS) if q == 0 or PAIRS[q][0] != PAIRS[q - 1][0]]
    b_changes = [q for q in range(N_PAIRS) if q == 0 or PAIRS[q][1] != PAIRS[q - 1][1]]
    pl.when(functools.reduce(jnp.logical_or, [pair == q for q in a_changes]))(cast_a)
    pl.when(functools.reduce(jnp.logical_or, [pair == q for q in b_changes]))(cast_b)

    g0 = bg0_ref[b]
    ntot = ntot_ref[0]
    grp = bg_ref[b]
    ea = bea_ref[b]
    eb = beb_ref[b]

    def compute(x32):
        x = _rms(x32, gffn_ref[...]).astype(BF16)
        lg = _dot(x, wr_ref[...]) + br_ref[...]
        lane = lax.broadcasted_iota(jnp.int32, lg.shape, 1)
        glog = jnp.where(lane < MOE_GROUPS, lg, -jnp.inf)
        pe = jnp.exp(glog - jnp.max(glog, axis=-1, keepdims=True))
        g_w = (jnp.sum(jnp.where(lane == grp, pe, 0.0), axis=-1, keepdims=True)
               / jnp.sum(pe, axis=-1, keepdims=True))
        la = jnp.sum(jnp.where(lane == MOE_GROUPS + ea, lg, 0.0), axis=-1, keepdims=True)
        lb = jnp.sum(jnp.where(lane == MOE_GROUPS + eb, lg, 0.0), axis=-1, keepdims=True)
        m = jnp.maximum(la, lb)
        pa = jnp.exp(la - m)
        pb = jnp.exp(lb - m)
        ca = pa / (pa + pb) * g_w
        cb = pb / (pa + pb) * g_w

        def expert(wg_ref, wu_ref, c):
            return (jax.nn.silu(_dot(x, wg_ref[...])) * _dot(x, wu_ref[...]) * c).astype(BF16)

        return (_dot(expert(wgab_ref, wuab_ref, ca), wdab_ref[...])
                + _dot(expert(wgbb_ref, wubb_ref, cb), wdbb_ref[...]))

    def load(g, rows):
        return pltpu.make_async_copy(_row_block(xs_ref, trow_ref[g], rows),
                                     _row_block(xbuf_ref.at[g % 2], 0, rows), isem.at[g % 2])

    def store(g, rows):
        return pltpu.make_async_copy(_row_block(ybuf_ref.at[g % 2], 0, rows),
                                     _row_block(ys_ref, trow_ref[g], rows), osem.at[g % 2])

    def by_size(g, fn):
        pl.when(tunits_ref[g] == T_MOE // MOE_UNIT)(lambda: fn(T_MOE))
        pl.when(tunits_ref[g] == 1)(lambda: fn(MOE_UNIT))

    @pl.when((b == 0) & (ntot > 0))
    def _():
        by_size(0, lambda rows: load(0, rows).start())

    def body(j, carry):
        g = g0 + j

        @pl.when(g + 1 < ntot)
        def _():
            by_size(g + 1, lambda rows: load(g + 1, rows).start())

        @pl.when(g >= 2)
        def _():
            by_size(g - 2, lambda rows: store(g - 2, rows).wait())

        def run(rows):
            load(g, rows).wait()
            y = compute(_load_row_major(xbuf_ref.at[g % 2], rows))
            _store_row_major(ybuf_ref.at[g % 2], y)
            store(g, rows).start()

        by_size(g, run)
        return carry

    lax.fori_loop(0, bnt_ref[b], body, 0)

    @pl.when(b == N_BUCKETS - 1)
    def _():
        for back in (2, 1):
            @pl.when(ntot >= back)
            def _():
                by_size(ntot - back, lambda rows: store(ntot - back, rows).wait())


def _experts(xs, i, cnt, gffn, wr, br, wg, wu, wd):
    e0 = i * MOE_EXPERTS
    max_tiles = xs.shape[0] // ROW_SUB // T_MOE + N_BUCKETS
    bg = jnp.asarray([q // N_PAIRS for q in range(N_BUCKETS)], jnp.int32)
    bea = jnp.asarray([(q // N_PAIRS) * MOE_EPG + PAIRS[q % N_PAIRS][0] for q in range(N_BUCKETS)],
                      jnp.int32)
    beb = jnp.asarray([(q // N_PAIRS) * MOE_EPG + PAIRS[q % N_PAIRS][1] for q in range(N_BUCKETS)],
                      jnp.int32)
    ea_map = lambda b, g, ea, eb: (e0 + ea[b], 0, 0)
    eb_map = lambda b, g, ea, eb: (e0 + eb[b], 0, 0)
    up_spec = lambda m: pl.BlockSpec((None, D_MODEL, MOE_FF), m)
    down_spec = lambda m: pl.BlockSpec((None, MOE_FF, D_MODEL), m)
    grid_spec = pltpu.PrefetchScalarGridSpec(
        num_scalar_prefetch=3,
        grid=(N_BUCKETS,),
        in_specs=[
            pl.BlockSpec(memory_space=pltpu.SMEM),
            pl.BlockSpec(memory_space=pl.ANY),
            pl.BlockSpec((None, 1, D_MODEL), lambda b, *_: (i, 0, 0)),
            pl.BlockSpec((None, D_MODEL, LANES), lambda b, *_: (i, 0, 0)),
            pl.BlockSpec((None, 1, LANES), lambda b, *_: (i, 0, 0)),
            up_spec(ea_map), up_spec(ea_map), down_spec(ea_map),
            up_spec(eb_map), up_spec(eb_map), down_spec(eb_map),
        ],
        out_specs=pl.BlockSpec(memory_space=pl.ANY),
        scratch_shapes=[
            pltpu.VMEM((D_MODEL, MOE_FF), BF16), pltpu.VMEM((D_MODEL, MOE_FF), BF16),
            pltpu.VMEM((MOE_FF, D_MODEL), BF16),
            pltpu.VMEM((D_MODEL, MOE_FF), BF16), pltpu.VMEM((D_MODEL, MOE_FF), BF16),
            pltpu.VMEM((MOE_FF, D_MODEL), BF16),
            pltpu.VMEM((2, T_MOE * ROW_SUB, LANES), F32),
            pltpu.VMEM((2, T_MOE * ROW_SUB, LANES), F32),
            pltpu.SemaphoreType.DMA((2,)),
            pltpu.SemaphoreType.DMA((2,)),
            pltpu.SMEM((N_BUCKETS,), jnp.int32), pltpu.SMEM((N_BUCKETS,), jnp.int32),
            pltpu.SMEM((max_tiles,), jnp.int32), pltpu.SMEM((max_tiles,), jnp.int32),
            pltpu.SMEM((1,), jnp.int32),
        ],
    )
    return pl.pallas_call(
        _experts_kernel,
        grid_spec=grid_spec,
        out_shape=jax.ShapeDtypeStruct(xs.shape, F32),
        input_output_aliases={4: 0},
        compiler_params=_params(("arbitrary",)),
        name="moe_experts",
    )(bg, bea, beb, cnt, xs, gffn, wr, br, wg, wu, wd, wg, wu, wd)


def _final_kernel(dcur_ref, dnext_ref, h_ref, p_ref, ys_ref, gple_ref, wup_ref, wgate_ref,
                  gfin_ref, o_ref, wupb_ref, wgateb_ref, ybuf_ref, sem):
    t = pl.program_id(0)
    _gather_start(t, dcur_ref, dnext_ref, ys_ref, ybuf_ref, sem)(0, 1)

    @pl.when(t == 0)
    def _():
        _cast_weight(wupb_ref, wup_ref)
        _cast_weight(wgateb_ref, wgate_ref)

    h2 = h_ref[...] + _gather_finish(t, ys_ref, ybuf_ref, sem)
    o_ref[...] = _rms(_ple(h2, p_ref, gple_ref, wupb_ref, wgateb_ref), gfin_ref[...])
    _gather_drain(t, pl.num_programs(0), ys_ref, ybuf_ref, sem)


def _final_layer(h, dest3, p2, ys, i, gple, wup, wgate, gfin):
    n = h.shape[0]
    nt = n // T_ROW
    return pl.pallas_call(
        _final_kernel,
        grid=(nt,),
        in_specs=_finish_specs(i, nt) + [pl.BlockSpec((1, D_MODEL), lambda t: (0, 0))],
        out_specs=pl.BlockSpec((T_ROW, D_MODEL), lambda t: (t, 0)),
        out_shape=jax.ShapeDtypeStruct((n, D_MODEL), F32),
        scratch_shapes=_FINISH_SCRATCH,
        compiler_params=_params(("arbitrary",)),
        name="final_combine_ple",
    )(dest3, dest3, h, p2, ys, gple, wup, wgate, gfin)


def _sorted_rows(n):
    return n + N_BUCKETS * MOE_UNIT


def kernel(x, p, gla_w_in, gla_w_gate_up, gla_b_gate, gla_g_head, gla_w_out, sgu_w_in, sgu_ln_g,
           sgu_ln_b, sgu_w_s, sgu_b_s, sgu_w_out, norm_mix_g, norm_ffn_g, norm_ple_g,
           moe_w_route_group, moe_b_route_group, moe_w_route_expert, moe_b_route_expert,
           moe_w_gate, moe_w_up, moe_w_down, ple_w_up, ple_w_gate, final_norm_g):
    batch, seq, d = x.shape
    assert d == D_MODEL and seq % T_MIX == 0
    n = batch * seq
    depth = p.shape[0]
    assert depth == DEPTH
    na, nb = gla_w_in.shape[0], sgu_w_in.shape[0]

    gla_win = jnp.pad(gla_w_in, ((0, 0), (0, 0), (0, GLA_A_PAD - GLA_RANK)))
    gla_wgu = jnp.pad(gla_w_gate_up, ((0, 0), (0, GLA_A_PAD - GLA_RANK), (0, 0)))
    gla_bg = gla_b_gate.reshape(na, 1, GLA_DK)
    gla_gh = gla_g_head.reshape(na, 1, GLA_DV)
    sgu_lng = sgu_ln_g.reshape(nb, 1, SGU_WIDTH)
    sgu_lnb = sgu_ln_b.reshape(nb, 1, SGU_WIDTH)
    sgu_bs = jnp.repeat(jnp.transpose(sgu_b_s, (0, 2, 1)), SGU_GC, axis=-1)
    g_mix = norm_mix_g.reshape(depth, 1, D_MODEL)
    g_ffn = norm_ffn_g.reshape(depth, 1, D_MODEL)
    g_ple = norm_ple_g.reshape(depth, 1, D_MODEL)
    g_fin = final_norm_g.reshape(1, D_MODEL)
    w_route = jnp.concatenate(
        [moe_w_route_group, moe_w_route_expert.reshape(depth, D_MODEL, MOE_EXPERTS)], axis=-1)
    b_route = jnp.concatenate(
        [moe_b_route_group, moe_b_route_expert.reshape(depth, MOE_EXPERTS)], axis=-1)
    n_logits = MOE_GROUPS + MOE_EXPERTS
    wr = jnp.pad(w_route, ((0, 0), (0, 0), (0, LANES - n_logits))).astype(BF16)
    br = jnp.pad(b_route, ((0, 0), (0, LANES - n_logits))).reshape(depth, 1, LANES)
    wrt = jnp.pad(jnp.transpose(w_route, (0, 2, 1)),
                  ((0, 0), (0, ROUTE_ROWS - n_logits), (0, 0))).astype(BF16)
    brt = jnp.broadcast_to(
        jnp.pad(b_route, ((0, 0), (0, ROUTE_ROWS - n_logits)))[:, :, None],
        (depth, ROUTE_ROWS, T_MIX))
    triu = jnp.triu(jnp.ones((T_MIX, T_MIX), BF16), k=1)
    wg = moe_w_gate.reshape(depth * MOE_EXPERTS, D_MODEL, MOE_FF)
    wu = moe_w_up.reshape(depth * MOE_EXPERTS, D_MODEL, MOE_FF)
    wd = moe_w_down.reshape(depth * MOE_EXPERTS, MOE_FF, D_MODEL)
    p2 = p.reshape(depth * n, PLE_DIM)

    ns = seq // T_MIX
    h = x.reshape(n, D_MODEL)
    finish = None
    for i in range(depth):
        if i % 2 == 0:
            kind, params = "gla", (g_mix, gla_win, gla_wgu, gla_bg, gla_gh, gla_w_out)
        else:
            kind, params = "sgu", (g_mix, sgu_w_in, sgu_lng, sgu_lnb, sgu_w_s, sgu_bs, sgu_w_out)
        h, dest3, cnt = _mixer_layer(kind, i, ns, h, finish, params, (g_ffn, wrt, brt, triu))
        xs = _dispatch(h, dest3, cnt)
        ys = _experts(xs, i, cnt, g_ffn, wr, br, wg, wu, wd)
        finish = (dest3, p2, ys, (g_ple, ple_w_up, ple_w_gate))
    h = _final_layer(h, dest3, p2, ys, depth - 1, g_ple, ple_w_up, ple_w_gate, g_fin)
    return h.reshape(batch, seq, D_MODEL)
```

```python
import functools

import jax
import jax.numpy as jnp
from jax import lax
from jax.experimental import pallas as pl
from jax.experimental.pallas import tpu as pltpu

F32 = jnp.float32
BF16 = jnp.bfloat16

D_MODEL = 1024
DEPTH = 4
GLA_HEADS = 4
GLA_DK = 512
GLA_DV = 1024
GLA_HK = GLA_DK // GLA_HEADS
GLA_HV = GLA_DV // GLA_HEADS
GLA_RANK = 16
GLA_TAU = 16.0
GLA_CHUNK = 64
SGU_WIDTH = 2048
SGU_GROUPS = 8
SGU_GC = SGU_WIDTH // SGU_GROUPS
SGU_CHUNK = 128
MOE_GROUPS = 4
MOE_EPG = 4
MOE_EXPERTS = MOE_GROUPS * MOE_EPG
MOE_FF = 512
PLE_DIM = 256
EPS = 1e-6

LANES = 128
ROW_SUB = D_MODEL // LANES
GLA_A_PAD = LANES
GLA_ZW = 2 * GLA_DK + 2 * GLA_DV + GLA_A_PAD
ROUTE_ROWS = 32
ROUTE_OUT_ROWS = 8
PAIRS = ((0, 1), (0, 2), (1, 2), (1, 3), (2, 3), (0, 3))
N_PAIRS = len(PAIRS)
N_BUCKETS = MOE_GROUPS * N_PAIRS

T_MIX = 256
SGU_VBLK = 512
T_ROW = T_MIX
T_MOE = 256
MOE_UNIT = 128
MOE_UNIT_LOG2 = 7
assert 1 << MOE_UNIT_LOG2 == MOE_UNIT and T_MOE == 2 * MOE_UNIT
DMA_UNROLL = 8
CAST_ROWS = 256
STAGE_ROWS = 64
VMEM_LIMIT = 56 * 1024 * 1024


def _rms(x, g):
    ms = jnp.mean(x * x, axis=-1, keepdims=True)
    return x * lax.rsqrt(ms + EPS) * g


def _dot(a, b):
    return jnp.dot(a, b, preferred_element_type=F32)


def _dot_nt(a, b):
    return lax.dot_general(a, b, (((1,), (1,)), ((), ())), preferred_element_type=F32)


def _dot_tn(a, b):
    return lax.dot_general(a, b, (((0,), (0,)), ((), ())), preferred_element_type=F32)


def _params(sem):
    return pltpu.CompilerParams(dimension_semantics=sem, vmem_limit_bytes=VMEM_LIMIT)


def _resident(block_shape, index_map):
    return pl.BlockSpec(block_shape, index_map, pipeline_mode=pl.Buffered(1))


def _cast_weight(dst_ref, src_ref):
    rows = src_ref.shape[0]
    for r in range(0, rows, CAST_ROWS):
        dst_ref[r:r + CAST_ROWS, :] = src_ref[r:r + CAST_ROWS, :].astype(BF16)


def _store_row_major(dst_ref, x):
    m = x.shape[0]
    for c in range(ROW_SUB):
        dst_ref[pl.ds(c, m, stride=ROW_SUB), :] = x[:, c * LANES:(c + 1) * LANES]


def _load_row_major(src_ref, m):
    return jnp.concatenate(
        [src_ref[pl.ds(c, m, stride=ROW_SUB), :] for c in range(ROW_SUB)], axis=-1)


def _row_block(ref, row, rows):
    start = row * ROW_SUB
    if not isinstance(start, int):
        start = pl.multiple_of(start, ROW_SUB)
    return ref.at[pl.ds(start, rows * ROW_SUB), :]


def _stage_weight(dst_ref, w_hbm_ref, stage_ref, sem):
    rows, ch = dst_ref.shape[0], stage_ref.shape[1]

    def chunk(k):
        return pltpu.make_async_copy(w_hbm_ref.at[pl.ds(k * ch, ch), :], stage_ref.at[k % 2],
                                     sem.at[k % 2])

    chunk(0).start()
    for k in range(rows // ch):
        if (k + 1) * ch < rows:
            chunk(k + 1).start()
        chunk(k).wait()
        dst_ref[k * ch:(k + 1) * ch, :] = stage_ref[k % 2].astype(BF16)


def _units(count):
    return lax.shift_right_logical(count + (MOE_UNIT - 1), MOE_UNIT_LOG2)


def _bucket_layout(cnt_ref, starts_ref):
    def body(b, row):
        starts_ref[b] = row
        return row + _units(cnt_ref[b, 0]) * MOE_UNIT
    return lax.fori_loop(0, N_BUCKETS, body, jnp.int32(0))


def _gather_start(t, dcur_ref, dnext_ref, ys_ref, ybuf_ref, sem):
    def row_copy(dref, slot, r):
        return pltpu.make_async_copy(_row_block(ys_ref, dref[0, 0, r], 1),
                                     _row_block(ybuf_ref.at[slot], r, 1), sem.at[slot])

    @pl.when(t == 0)
    def _():
        def body(k, carry):
            base = pl.multiple_of(k * DMA_UNROLL, DMA_UNROLL)
            for u in range(DMA_UNROLL):
                row_copy(dcur_ref, 0, base + u).start()
            return carry
        lax.fori_loop(0, T_ROW // DMA_UNROLL, body, 0)

    def issue_next(k, n):
        for r in range(k * T_ROW // n, (k + 1) * T_ROW // n):
            row_copy(dnext_ref, (t + 1) % 2, r).start()

    return issue_next


def _gather_wait(slot, ys_ref, ybuf_ref, sem):
    pltpu.make_async_copy(_row_block(ys_ref, 0, T_ROW), ybuf_ref.at[slot], sem.at[slot]).wait()


def _gather_finish(t, ys_ref, ybuf_ref, sem):
    _gather_wait(t % 2, ys_ref, ybuf_ref, sem)
    return _load_row_major(ybuf_ref.at[t % 2], T_ROW)


def _gather_drain(t, nt, ys_ref, ybuf_ref, sem):
    @pl.when(t == nt - 1)
    def _():
        _gather_wait((t + 1) % 2, ys_ref, ybuf_ref, sem)


def _ple(h2, p_ref, gple_ref, wupb_ref, wgateb_ref):
    pn = _rms(h2, gple_ref[...]).astype(BF16)
    gate = jax.nn.sigmoid(_dot(pn, wgateb_ref[...]))
    up = _dot(p_ref[...].astype(BF16), wupb_ref[...])
    return h2 + up * gate


def _finish_specs(i, nt):
    return [
        pl.BlockSpec((1, 1, T_ROW), lambda t: (t, 0, 0), memory_space=pltpu.SMEM),
        pl.BlockSpec((1, 1, T_ROW), lambda t: (jnp.minimum(t + 1, nt - 1), 0, 0),
                     memory_space=pltpu.SMEM),
        pl.BlockSpec((T_ROW, D_MODEL), lambda t: (t, 0)),
        pl.BlockSpec((T_ROW, PLE_DIM), lambda t: (i * nt + t, 0)),
        pl.BlockSpec(memory_space=pl.ANY),
        pl.BlockSpec((None, 1, D_MODEL), lambda t: (i, 0, 0)),
        _resident((None, PLE_DIM, D_MODEL), lambda t: (i, 0, 0)),
        _resident((None, D_MODEL, D_MODEL), lambda t: (i, 0, 0)),
    ]


N_FINISH_INPUTS = 8

_FINISH_SCRATCH = [
    pltpu.VMEM((PLE_DIM, D_MODEL), BF16),
    pltpu.VMEM((D_MODEL, D_MODEL), BF16),
    pltpu.VMEM((2, T_ROW * ROW_SUB, LANES), F32),
    pltpu.SemaphoreType.DMA((2,)),
]


def _gla_body(h, tick, gmix_ref, wgu_ref, bg_ref, ghead_ref, winb_ref, woutb_ref,
              st_ref, oacc_ref, y_ref):
    hn = _rms(h, gmix_ref[...]).astype(BF16)
    z = _dot(hn, winb_ref[...])
    a = z[:, 2 * GLA_DK + 2 * GLA_DV:].astype(BF16)
    pre = _dot(a, wgu_ref[...].astype(BF16)) + bg_ref[...]
    la = (jnp.minimum(pre, 0.0) - jnp.log1p(jnp.exp(-jnp.abs(pre)))) * (1.0 / GLA_TAU)

    C = GLA_CHUNK
    ri = lax.broadcasted_iota(jnp.int32, (C, C), 0)
    ci = lax.broadcasted_iota(jnp.int32, (C, C), 1)
    causal = ri >= ci
    tri = causal.astype(BF16)

    for c in range(T_MIX // C):
        rows = slice(c * C, (c + 1) * C)
        la_c = la[rows]
        hi = la_c.astype(BF16)
        lo = (la_c - hi.astype(F32)).astype(BF16)
        bc = _dot(tri, hi) + _dot(tri, lo)
        bend = bc[C - 1:C, :]
        kc = z[rows, GLA_DK:2 * GLA_DK]
        qd = (z[rows, 0:GLA_DK] * (GLA_HK ** -0.5) * jnp.exp(bc)).astype(BF16)
        kd = (kc * jnp.exp(-bc)).astype(BF16)
        ke = (kc * jnp.exp(bend - bc)).astype(BF16)
        vc = z[rows, 2 * GLA_DK:2 * GLA_DK + GLA_DV].astype(BF16)
        dec = jnp.exp(bend)
        for hd in range(GLA_HEADS):
            ks = slice(hd * GLA_HK, (hd + 1) * GLA_HK)
            vs = slice(hd * GLA_HV, (hd + 1) * GLA_HV)
            att = jnp.where(causal, _dot_nt(qd[:, ks], kd[:, ks]), 0.0).astype(BF16)
            st = st_ref[hd]
            o = _dot(att, vc[:, vs]) + _dot_nt(qd[:, ks], st.astype(BF16))
            st_ref[hd] = st * dec[:, ks] + _dot_tn(vc[:, vs], ke[:, ks])
            oacc_ref[rows, vs] = o
            tick(c * GLA_HEADS + hd, (T_MIX // C) * GLA_HEADS)

    r0 = 2 * GLA_DK + GLA_DV
    for hd in range(GLA_HEADS):
        vs = slice(hd * GLA_HV, (hd + 1) * GLA_HV)
        on = _rms(oacc_ref[:, vs], ghead_ref[:, vs])
        r = z[:, r0 + hd * GLA_HV:r0 + (hd + 1) * GLA_HV]
        y_ref[:, vs] = (jax.nn.silu(r) * on).astype(BF16)
    return h + _dot(y_ref[...], woutb_ref[...])


def _sgu_body(h, tick, gmix_ref, lng_ref, lnb_ref, bs_ref, winb_ref, woutb_ref, wsb_ref, us_ref,
              v_ref):
    C = SGU_CHUNK
    hn = _rms(h, gmix_ref[...]).astype(BF16)

    vsum = jnp.zeros((T_MIX, 1), F32)
    for k in range(SGU_WIDTH // SGU_VBLK):
        cols = slice(k * SGU_VBLK, (k + 1) * SGU_VBLK)
        vb = jax.nn.gelu(_dot(hn, winb_ref[:, SGU_WIDTH + k * SGU_VBLK:SGU_WIDTH + (k + 1) * SGU_VBLK]),
                         approximate=True)
        v_ref[:, cols] = vb
        vsum = vsum + jnp.sum(vb, axis=-1, keepdims=True)
    mean = vsum * (1.0 / SGU_WIDTH)
    vsq = jnp.zeros((T_MIX, 1), F32)
    for k in range(SGU_WIDTH // SGU_VBLK):
        cols = slice(k * SGU_VBLK, (k + 1) * SGU_VBLK)
        xc = v_ref[:, cols] - mean
        vsq = vsq + jnp.sum(xc * xc, axis=-1, keepdims=True)
    rstd = lax.rsqrt(vsq * (1.0 / SGU_WIDTH) + EPS)

    for g in range(SGU_GROUPS):
        cols = slice(g * SGU_GC, (g + 1) * SGU_GC)
        u = jax.nn.gelu(_dot(hn, winb_ref[:, cols]), approximate=True)
        vn = (((v_ref[:, cols] - mean) * rstd) * lng_ref[:, cols] + lnb_ref[:, cols]).astype(BF16)
        wc = wsb_ref[g]
        for c in range(T_MIX // C):
            rows = slice(c * C, (c + 1) * C)
            s = _dot(wc, vn[rows]) + bs_ref[:, cols]
            us_ref[rows, cols] = (u[rows] * s).astype(BF16)
        tick(g, SGU_GROUPS)
    return h + _dot(us_ref[...], woutb_ref[...])


def _mixer_kernel(*refs, kind, fused, j, ns):
    refs = list(refs)
    t = pl.program_id(0)
    nt = pl.num_programs(0)
    if fused:
        (dcur_ref, dnext_ref, hprev_ref, p_ref, ys_ref, gple_ref, wup_ref,
         wgate_ref) = refs[:N_FINISH_INPUTS]
        wupb_ref, wgateb_ref, ybuf_ref, gsem = refs[-len(_FINISH_SCRATCH):]
        refs = refs[N_FINISH_INPUTS:-len(_FINISH_SCRATCH)]
        tick = _gather_start(t, dcur_ref, dnext_ref, ys_ref, ybuf_ref, gsem)
    else:
        h_ref, refs = refs[0], refs[1:]
        tick = lambda k, n: None
    cnt_ref, route_ref, hb_ref = refs[-3:]
    refs = refs[:-3]
    if kind == "gla":
        (gmix_ref, win_hbm, wgu_ref, bg_ref, ghead_ref, wout_ref,
         gffn_ref, wrt_ref, brt_ref, triu_ref, o_ref, dest_ref, counts_ref,
         winb_ref, woutb_ref, stage_ref, ssem, st_ref, oacc_ref, y_ref) = refs
    else:
        (gmix_ref, win_hbm, lng_ref, lnb_ref, ws_ref, bs_ref, wout_ref,
         gffn_ref, wrt_ref, brt_ref, triu_ref, o_ref, dest_ref, counts_ref,
         winb_ref, woutb_ref, wsb_ref, stage_ref, ssem, us_ref, v_ref) = refs

    @pl.when(t == 0)
    def _():
        _stage_weight(winb_ref, win_hbm.at[j], stage_ref, ssem)
        _cast_weight(woutb_ref, wout_ref)
        if fused:
            _cast_weight(wupb_ref, wup_ref)
            _cast_weight(wgateb_ref, wgate_ref)
        if kind == "sgu":
            ri = lax.broadcasted_iota(jnp.int32, (SGU_CHUNK, SGU_CHUNK), 0)
            ci = lax.broadcasted_iota(jnp.int32, (SGU_CHUNK, SGU_CHUNK), 1)
            for g in range(SGU_GROUPS):
                wsb_ref[g] = jnp.where(ri >= ci, ws_ref[g], 0.0).astype(BF16)
        cnt_ref[...] = jnp.zeros_like(cnt_ref)
        hb_ref[...] = jnp.zeros_like(hb_ref)

    if kind == "gla":
        @pl.when(t % ns == 0)
        def _():
            st_ref[...] = jnp.zeros_like(st_ref)

    if fused:
        h2 = hprev_ref[...] + _gather_finish(t, ys_ref, ybuf_ref, gsem)
        h = _ple(h2, p_ref, gple_ref, wupb_ref, wgateb_ref)
    else:
        h = h_ref[...]
    _route_tile(hb_ref[...], jnp.maximum(t - 1, 0), t > 0, wrt_ref, brt_ref, triu_ref, cnt_ref,
                route_ref)
    if kind == "gla":
        out = _gla_body(h, tick, gmix_ref, wgu_ref, bg_ref, ghead_ref, winb_ref,
                        woutb_ref, st_ref, oacc_ref, y_ref)
    else:
        out = _sgu_body(h, tick, gmix_ref, lng_ref, lnb_ref, bs_ref, winb_ref, woutb_ref,
                        wsb_ref, us_ref, v_ref)
    o_ref[...] = out
    hb_ref[...] = _rms(out, gffn_ref[...]).astype(BF16)

    @pl.when(t == nt - 1)
    def _():
        _route_tile(hb_ref[...], t, t >= 0, wrt_ref, brt_ref, triu_ref, cnt_ref, route_ref)
        _emit_dest(nt, cnt_ref, route_ref, dest_ref, counts_ref)
    if fused:
        _gather_drain(t, nt, ys_ref, ybuf_ref, gsem)


def _mixer_layer(kind, i, ns, h, finish, params, router):
    n = h.shape[0]
    nt = n // T_MIX
    j = i // 2
    tile = lambda t: (t, 0)
    if kind == "gla":
        zw = GLA_ZW
        mixer_specs = [
            pl.BlockSpec((None, 1, D_MODEL), lambda t: (i, 0, 0)),
            pl.BlockSpec(memory_space=pl.ANY),
            pl.BlockSpec((None, GLA_A_PAD, GLA_DK), lambda t: (j, 0, 0)),
            pl.BlockSpec((None, 1, GLA_DK), lambda t: (j, 0, 0)),
            pl.BlockSpec((None, 1, GLA_DV), lambda t: (j, 0, 0)),
            _resident((None, GLA_DV, D_MODEL), lambda t: (j, 0, 0)),
        ]
        mixer_scratch = [
            pltpu.VMEM((D_MODEL, zw), BF16),
            pltpu.VMEM((GLA_DV, D_MODEL), BF16),
            pltpu.VMEM((2, STAGE_ROWS, zw), F32),
            pltpu.SemaphoreType.DMA((2,)),
            pltpu.VMEM((GLA_HEADS, GLA_HV, GLA_HK), F32),
            pltpu.VMEM((T_MIX, GLA_DV), F32),
            pltpu.VMEM((T_MIX, GLA_DV), BF16),
        ]
    else:
        zw = 2 * SGU_WIDTH
        mixer_specs = [
            pl.BlockSpec((None, 1, D_MODEL), lambda t: (i, 0, 0)),
            pl.BlockSpec(memory_space=pl.ANY),
            pl.BlockSpec((None, 1, SGU_WIDTH), lambda t: (j, 0, 0)),
            pl.BlockSpec((None, 1, SGU_WIDTH), lambda t: (j, 0, 0)),
            _resident((None, SGU_GROUPS, SGU_CHUNK, SGU_CHUNK), lambda t: (j, 0, 0, 0)),
            _resident((None, SGU_CHUNK, SGU_WIDTH), lambda t: (j, 0, 0)),
            _resident((None, SGU_WIDTH, D_MODEL), lambda t: (j, 0, 0)),
        ]
        mixer_scratch = [
            pltpu.VMEM((D_MODEL, zw), BF16),
            pltpu.VMEM((SGU_WIDTH, D_MODEL), BF16),
            pltpu.VMEM((SGU_GROUPS, SGU_CHUNK, SGU_CHUNK), BF16),
            pltpu.VMEM((2, STAGE_ROWS, zw), F32),
            pltpu.SemaphoreType.DMA((2,)),
            pltpu.VMEM((T_MIX, SGU_WIDTH), BF16),
            pltpu.VMEM((T_MIX, SGU_WIDTH), F32),
        ]
    if finish is None:
        head_specs = [pl.BlockSpec((T_MIX, D_MODEL), tile)]
        head_args = (h,)
        tail_scratch = []
    else:
        dest3, p2, ys, (gple, wup, wgate) = finish
        head_specs = _finish_specs(i - 1, nt)
        head_args = (dest3, dest3, h, p2, ys, gple, wup, wgate)
        tail_scratch = _FINISH_SCRATCH
    router_specs = [
        pl.BlockSpec((None, 1, D_MODEL), lambda t: (i, 0, 0)),
        pl.BlockSpec((None, ROUTE_ROWS, D_MODEL), lambda t: (i, 0, 0)),
        pl.BlockSpec((None, ROUTE_ROWS, T_MIX), lambda t: (i, 0, 0)),
        pl.BlockSpec((T_MIX, T_MIX), lambda t: (0, 0)),
    ]
    router_scratch = [
        pltpu.VMEM((ROUTE_ROWS, LANES), F32),
        pltpu.VMEM((nt, ROUTE_OUT_ROWS, T_MIX), jnp.int32),
        pltpu.VMEM((T_MIX, D_MODEL), BF16),
    ]
    return pl.pallas_call(
        functools.partial(_mixer_kernel, kind=kind, fused=finish is not None, j=j, ns=ns),
        grid=(nt,),
        in_specs=head_specs + mixer_specs + router_specs,
        out_specs=[
            pl.BlockSpec((T_MIX, D_MODEL), tile),
            pl.BlockSpec((nt, 1, T_MIX), lambda t: (0, 0, 0)),
            pl.BlockSpec((ROUTE_ROWS, LANES), lambda t: (0, 0)),
        ],
        out_shape=[
            jax.ShapeDtypeStruct((n, D_MODEL), F32),
            jax.ShapeDtypeStruct((nt, 1, T_MIX), jnp.int32),
            jax.ShapeDtypeStruct((ROUTE_ROWS, LANES), jnp.int32),
        ],
        scratch_shapes=mixer_scratch + router_scratch + tail_scratch,
        compiler_params=_params(("arbitrary",)),
        name=kind + ("_fused" if finish is not None else "") + "_mixer",
    )(*head_args, *params, *router)


def _first_argmax(rows):
    best = rows[0]
    idx = jnp.zeros(best.shape, jnp.int32)
    for j in range(1, len(rows)):
        upd = rows[j] > best
        idx = jnp.where(upd, j, idx)
        best = jnp.where(upd, rows[j], best)
    return idx, best


def _route_tile(hb, k, live, wrt_ref, brt_ref, triu_ref, cnt_ref, route_ref):
    lg = _dot_nt(wrt_ref[...], hb) + brt_ref[...]
    row = lambda j: lg[j:j + 1, :]
    g, _ = _first_argmax([row(j) for j in range(MOE_GROUPS)])
    ex = []
    for e in range(MOE_EPG):
        v = row(MOE_GROUPS + e)
        for gg in range(1, MOE_GROUPS):
            v = jnp.where(g == gg, row(MOE_GROUPS + gg * MOE_EPG + e), v)
        ex.append(v)
    i1, _ = _first_argmax(ex)
    neg = jnp.full(ex[0].shape, -jnp.inf, F32)
    i2, _ = _first_argmax([jnp.where(i1 == e, neg, ex[e]) for e in range(MOE_EPG)])
    lo = jnp.minimum(i1, i2)
    hi = jnp.maximum(i1, i2)
    pair = jnp.zeros(lo.shape, jnp.int32)
    for p, (a, b) in enumerate(PAIRS):
        pair = jnp.where((lo == a) & (hi == b), p, pair)
    bucket = g * N_PAIRS + pair

    m = bucket.shape[1]
    onehot = (lax.broadcasted_iota(jnp.int32, (ROUTE_ROWS, m), 0) == bucket).astype(F32)
    before = _dot(onehot.astype(BF16), triu_ref[...])
    rank = jnp.sum(onehot * (before + cnt_ref[:, 0:1]), axis=0, keepdims=True).astype(jnp.int32)
    cnt_ref[...] = cnt_ref[...] + jnp.sum(onehot, axis=1, keepdims=True) * live.astype(F32)
    r8 = lax.broadcasted_iota(jnp.int32, (ROUTE_OUT_ROWS, m), 0)
    route_ref[k] = jnp.where(r8 == 0, bucket, jnp.where(r8 == 1, rank, 0))


def _emit_dest(nt, cnt_ref, route_ref, dest_ref, counts_ref):
    cnt = cnt_ref[...].astype(jnp.int32)
    counts_ref[...] = cnt
    units = _units(cnt).astype(F32).astype(BF16)
    ri = lax.broadcasted_iota(jnp.int32, (ROUTE_ROWS, ROUTE_ROWS), 0)
    ci = lax.broadcasted_iota(jnp.int32, (ROUTE_ROWS, ROUTE_ROWS), 1)
    before = _dot((ci < ri).astype(BF16), units)
    start = (before[:, 0:1] * MOE_UNIT).astype(jnp.int32)

    def body(k, carry):
        blk = route_ref[k]
        onehot = lax.broadcasted_iota(jnp.int32, (ROUTE_ROWS, T_MIX), 0) == blk[0:1, :]
        dest_ref[k] = jnp.sum(jnp.where(onehot, start, 0), axis=0, keepdims=True) + blk[1:2, :]
        return carry

    lax.fori_loop(0, nt, body, 0)


def _dispatch_kernel(cnt_ref, dest_ref, h_ref, xs_ref, zero_ref, rows_ref, starts_ref, zsem, sem):
    @pl.when(pl.program_id(0) == 0)
    def _():
        end = _bucket_layout(cnt_ref, starts_ref)
        zero_ref[...] = jnp.zeros_like(zero_ref)
        fills = []
        for b in range(N_BUCKETS):
            c = cnt_ref[b, 0]
            whole = lax.shift_right_logical(c, MOE_UNIT_LOG2) * MOE_UNIT
            fills.append(((c & (MOE_UNIT - 1)) != 0, starts_ref[b] + whole))
        for k in range(N_BUCKETS):
            fills.append((end + k * MOE_UNIT < xs_ref.shape[0] // ROW_SUB, end + k * MOE_UNIT))

        def fill(row):
            return pltpu.make_async_copy(zero_ref, _row_block(xs_ref, row, MOE_UNIT), zsem)

        for on, row in fills:
            pl.when(on)(lambda row=row: fill(row).start())
        for on, row in fills:
            pl.when(on)(lambda row=row: fill(row).wait())

    _store_row_major(rows_ref, h_ref[...])

    def issue(k, carry):
        base = pl.multiple_of(k * DMA_UNROLL, DMA_UNROLL)
        for u in range(DMA_UNROLL):
            d = dest_ref[0, 0, base + u]
            pltpu.make_async_copy(_row_block(rows_ref, base + u, 1),
                                  _row_block(xs_ref, d, 1), sem).start(priority=u % 2)
        return carry

    lax.fori_loop(0, T_ROW // DMA_UNROLL, issue, 0)
    pltpu.make_async_copy(rows_ref, _row_block(xs_ref, 0, T_ROW), sem).wait()


def _dispatch(h, dest3, cnt):
    n = h.shape[0]
    return pl.pallas_call(
        _dispatch_kernel,
        grid=(n // T_ROW,),
        in_specs=[
            pl.BlockSpec(memory_space=pltpu.SMEM),
            pl.BlockSpec((1, 1, T_ROW), lambda t: (t, 0, 0), memory_space=pltpu.SMEM),
            pl.BlockSpec((T_ROW, D_MODEL), lambda t: (t, 0)),
        ],
        out_specs=pl.BlockSpec(memory_space=pl.ANY),
        out_shape=jax.ShapeDtypeStruct((_sorted_rows(n) * ROW_SUB, LANES), F32),
        scratch_shapes=[
            pltpu.VMEM((MOE_UNIT * ROW_SUB, LANES), F32),
            pltpu.VMEM((T_ROW * ROW_SUB, LANES), F32),
            pltpu.SMEM((N_BUCKETS,), jnp.int32),
            pltpu.SemaphoreType.DMA(()),
            pltpu.SemaphoreType.DMA(()),
        ],
        compiler_params=_params(("arbitrary",)),
        name="moe_dispatch",
    )(cnt, dest3, h)


def _experts_kernel(bg_ref, bea_ref, beb_ref,
                    cnt_ref, xs_ref, gffn_ref, wr_ref, br_ref,
                    wga_ref, wua_ref, wda_ref, wgb_ref, wub_ref, wdb_ref,
                    ys_ref,
                    wgab_ref, wuab_ref, wdab_ref, wgbb_ref, wubb_ref, wdbb_ref,
                    xbuf_ref, ybuf_ref, isem, osem,
                    bg0_ref, bnt_ref, trow_ref, tunits_ref, ntot_ref):
    b = pl.program_id(0)
    pair = b % N_PAIRS

    @pl.when(b == 0)
    def _():
        def per_bucket(q, carry):
            row, g = carry
            units = _units(cnt_ref[q, 0])
            n_full = lax.shift_right_logical(units, 1)
            n_tiles = n_full + (units & 1)
            bg0_ref[q] = g
            bnt_ref[q] = n_tiles

            def per_tile(j, c):
                trow_ref[g + j] = row + j * T_MOE
                tunits_ref[g + j] = jnp.where(j < n_full, T_MOE // MOE_UNIT, 1)
                return c

            lax.fori_loop(0, n_tiles, per_tile, 0)
            return row + units * MOE_UNIT, g + n_tiles

        _, n_total = lax.fori_loop(0, N_BUCKETS, per_bucket, (jnp.int32(0), jnp.int32(0)))
        ntot_ref[0] = n_total

    def cast_a():
        _cast_weight(wgab_ref, wga_ref)
        _cast_weight(wuab_ref, wua_ref)
        _cast_weight(wdab_ref, wda_ref)

    def cast_b():
        _cast_weight(wgbb_ref, wgb_ref)
        _cast_weight(wubb_ref, wub_ref)
        _cast_weight(wdbb_ref, wdb_ref)

    a_changes = [q for q in range(N_PAIRS) if q == 0 or PAIRS[q][0] != PAIRS[q - 1][0]]
    b_changes = [q for q in range(N_PAIRS) if q == 0 or PAIRS[q][1] != PAIRS[q - 1][1]]
    pl.when(functools.reduce(jnp.logical_or, [pair == q for q in a_changes]))(cast_a)
    pl.when(functools.reduce(jnp.logical_or, [pair == q for q in b_changes]))(cast_b)

    g0 = bg0_ref[b]
    ntot = ntot_ref[0]
    grp = bg_ref[b]
    ea = bea_ref[b]
    eb = beb_ref[b]

    def compute(x32):
        x = _rms(x32, gffn_ref[...]).astype(BF16)
        lg = _dot(x, wr_ref[...]) + br_ref[...]
        lane = lax.broadcasted_iota(jnp.int32, lg.shape, 1)
        glog = jnp.where(lane < MOE_GROUPS, lg, -jnp.inf)
        pe = jnp.exp(glog - jnp.max(glog, axis=-1, keepdims=True))
        g_w = (jnp.sum(jnp.where(lane == grp, pe, 0.0), axis=-1, keepdims=True)
               / jnp.sum(pe, axis=-1, keepdims=True))
        la = jnp.sum(jnp.where(lane == MOE_GROUPS + ea, lg, 0.0), axis=-1, keepdims=True)
        lb = jnp.sum(jnp.where(lane == MOE_GROUPS + eb, lg, 0.0), axis=-1, keepdims=True)
        m = jnp.maximum(la, lb)
        pa = jnp.exp(la - m)
        pb = jnp.exp(lb - m)
        ca = pa / (pa + pb) * g_w
        cb = pb / (pa + pb) * g_w

        def expert(wg_ref, wu_ref, c):
            return (jax.nn.silu(_dot(x, wg_ref[...])) * _dot(x, wu_ref[...]) * c).astype(BF16)

        return (_dot(expert(wgab_ref, wuab_ref, ca), wdab_ref[...])
                + _dot(expert(wgbb_ref, wubb_ref, cb), wdbb_ref[...]))

    def load(g, rows):
        return pltpu.make_async_copy(_row_block(xs_ref, trow_ref[g], rows),
                                     _row_block(xbuf_ref.at[g % 2], 0, rows), isem.at[g % 2])

    def store(g, rows):
        return pltpu.make_async_copy(_row_block(ybuf_ref.at[g % 2], 0, rows),
                                     _row_block(ys_ref, trow_ref[g], rows), osem.at[g % 2])

    def by_size(g, fn):
        pl.when(tunits_ref[g] == T_MOE // MOE_UNIT)(lambda: fn(T_MOE))
        pl.when(tunits_ref[g] == 1)(lambda: fn(MOE_UNIT))

    @pl.when((b == 0) & (ntot > 0))
    def _():
        by_size(0, lambda rows: load(0, rows).start())

    def body(j, carry):
        g = g0 + j

        @pl.when(g + 1 < ntot)
        def _():
            by_size(g + 1, lambda rows: load(g + 1, rows).start())

        @pl.when(g >= 2)
        def _():
            by_size(g - 2, lambda rows: store(g - 2, rows).wait())

        def run(rows):
            load(g, rows).wait()
            y = compute(_load_row_major(xbuf_ref.at[g % 2], rows))
            _store_row_major(ybuf_ref.at[g % 2], y)
            store(g, rows).start()

        by_size(g, run)
        return carry

    lax.fori_loop(0, bnt_ref[b], body, 0)

    @pl.when(b == N_BUCKETS - 1)
    def _():
        for back in (2, 1):
            @pl.when(ntot >= back)
            def _():
                by_size(ntot - back, lambda rows: store(ntot - back, rows).wait())


def _experts(xs, i, cnt, gffn, wr, br, wg, wu, wd):
    e0 = i * MOE_EXPERTS
    max_tiles = xs.shape[0] // ROW_SUB // T_MOE + N_BUCKETS
    bg = jnp.asarray([q // N_PAIRS for q in range(N_BUCKETS)], jnp.int32)
    bea = jnp.asarray([(q // N_PAIRS) * MOE_EPG + PAIRS[q % N_PAIRS][0] for q in range(N_BUCKETS)],
                      jnp.int32)
    beb = jnp.asarray([(q // N_PAIRS) * MOE_EPG + PAIRS[q % N_PAIRS][1] for q in range(N_BUCKETS)],
                      jnp.int32)
    ea_map = lambda b, g, ea, eb: (e0 + ea[b], 0, 0)
    eb_map = lambda b, g, ea, eb: (e0 + eb[b], 0, 0)
    up_spec = lambda m: pl.BlockSpec((None, D_MODEL, MOE_FF), m)
    down_spec = lambda m: pl.BlockSpec((None, MOE_FF, D_MODEL), m)
    grid_spec = pltpu.PrefetchScalarGridSpec(
        num_scalar_prefetch=3,
        grid=(N_BUCKETS,),
        in_specs=[
            pl.BlockSpec(memory_space=pltpu.SMEM),
            pl.BlockSpec(memory_space=pl.ANY),
            pl.BlockSpec((None, 1, D_MODEL), lambda b, *_: (i, 0, 0)),
            pl.BlockSpec((None, D_MODEL, LANES), lambda b, *_: (i, 0, 0)),
            pl.BlockSpec((None, 1, LANES), lambda b, *_: (i, 0, 0)),
            up_spec(ea_map), up_spec(ea_map), down_spec(ea_map),
            up_spec(eb_map), up_spec(eb_map), down_spec(eb_map),
        ],
        out_specs=pl.BlockSpec(memory_space=pl.ANY),
        scratch_shapes=[
            pltpu.VMEM((D_MODEL, MOE_FF), BF16), pltpu.VMEM((D_MODEL, MOE_FF), BF16),
            pltpu.VMEM((MOE_FF, D_MODEL), BF16),
            pltpu.VMEM((D_MODEL, MOE_FF), BF16), pltpu.VMEM((D_MODEL, MOE_FF), BF16),
            pltpu.VMEM((MOE_FF, D_MODEL), BF16),
            pltpu.VMEM((2, T_MOE * ROW_SUB, LANES), F32),
            pltpu.VMEM((2, T_MOE * ROW_SUB, LANES), F32),
            pltpu.SemaphoreType.DMA((2,)),
            pltpu.SemaphoreType.DMA((2,)),
            pltpu.SMEM((N_BUCKETS,), jnp.int32), pltpu.SMEM((N_BUCKETS,), jnp.int32),
            pltpu.SMEM((max_tiles,), jnp.int32), pltpu.SMEM((max_tiles,), jnp.int32),
            pltpu.SMEM((1,), jnp.int32),
        ],
    )
    return pl.pallas_call(
        _experts_kernel,
        grid_spec=grid_spec,
        out_shape=jax.ShapeDtypeStruct(xs.shape, F32),
        input_output_aliases={4: 0},
        compiler_params=_params(("arbitrary",)),
        name="moe_experts",
    )(bg, bea, beb, cnt, xs, gffn, wr, br, wg, wu, wd, wg, wu, wd)


def _final_kernel(dcur_ref, dnext_ref, h_ref, p_ref, ys_ref, gple_ref, wup_ref, wgate_ref,
                  gfin_ref, o_ref, wupb_ref, wgateb_ref, ybuf_ref, sem):
    t = pl.program_id(0)
    _gather_start(t, dcur_ref, dnext_ref, ys_ref, ybuf_ref, sem)(0, 1)

    @pl.when(t == 0)
    def _():
        _cast_weight(wupb_ref, wup_ref)
        _cast_weight(wgateb_ref, wgate_ref)

    h2 = h_ref[...] + _gather_finish(t, ys_ref, ybuf_ref, sem)
    o_ref[...] = _rms(_ple(h2, p_ref, gple_ref, wupb_ref, wgateb_ref), gfin_ref[...])
    _gather_drain(t, pl.num_programs(0), ys_ref, ybuf_ref, sem)


def _final_layer(h, dest3, p2, ys, i, gple, wup, wgate, gfin):
    n = h.shape[0]
    nt = n // T_ROW
    return pl.pallas_call(
        _final_kernel,
        grid=(nt,),
        in_specs=_finish_specs(i, nt) + [pl.BlockSpec((1, D_MODEL), lambda t: (0, 0))],
        out_specs=pl.BlockSpec((T_ROW, D_MODEL), lambda t: (t, 0)),
        out_shape=jax.ShapeDtypeStruct((n, D_MODEL), F32),
        scratch_shapes=_FINISH_SCRATCH,
        compiler_params=_params(("arbitrary",)),
        name="final_combine_ple",
    )(dest3, dest3, h, p2, ys, gple, wup, wgate, gfin)


def _sorted_rows(n):
    return n + N_BUCKETS * MOE_UNIT


def kernel(x, p, gla_w_in, gla_w_gate_up, gla_b_gate, gla_g_head, gla_w_out, sgu_w_in, sgu_ln_g,
           sgu_ln_b, sgu_w_s, sgu_b_s, sgu_w_out, norm_mix_g, norm_ffn_g, norm_ple_g,
           moe_w_route_group, moe_b_route_group, moe_w_route_expert, moe_b_route_expert,
           moe_w_gate, moe_w_up, moe_w_down, ple_w_up, ple_w_gate, final_norm_g):
    batch, seq, d = x.shape
    assert d == D_MODEL and seq % T_MIX == 0
    n = batch * seq
    depth = p.shape[0]
    assert depth == DEPTH
    na, nb = gla_w_in.shape[0], sgu_w_in.shape[0]

    gla_win = jnp.pad(gla_w_in, ((0, 0), (0, 0), (0, GLA_A_PAD - GLA_RANK)))
    gla_wgu = jnp.pad(gla_w_gate_up, ((0, 0), (0, GLA_A_PAD - GLA_RANK), (0, 0)))
    gla_bg = gla_b_gate.reshape(na, 1, GLA_DK)
    gla_gh = gla_g_head.reshape(na, 1, GLA_DV)
    sgu_lng = sgu_ln_g.reshape(nb, 1, SGU_WIDTH)
    sgu_lnb = sgu_ln_b.reshape(nb, 1, SGU_WIDTH)
    sgu_bs = jnp.repeat(jnp.transpose(sgu_b_s, (0, 2, 1)), SGU_GC, axis=-1)
    g_mix = norm_mix_g.reshape(depth, 1, D_MODEL)
    g_ffn = norm_ffn_g.reshape(depth, 1, D_MODEL)
    g_ple = norm_ple_g.reshape(depth, 1, D_MODEL)
    g_fin = final_norm_g.reshape(1, D_MODEL)
    w_route = jnp.concatenate(
        [moe_w_route_group, moe_w_route_expert.reshape(depth, D_MODEL, MOE_EXPERTS)], axis=-1)
    b_route = jnp.concatenate(
        [moe_b_route_group, moe_b_route_expert.reshape(depth, MOE_EXPERTS)], axis=-1)
    n_logits = MOE_GROUPS + MOE_EXPERTS
    wr = jnp.pad(w_route, ((0, 0), (0, 0), (0, LANES - n_logits))).astype(BF16)
    br = jnp.pad(b_route, ((0, 0), (0, LANES - n_logits))).reshape(depth, 1, LANES)
    wrt = jnp.pad(jnp.transpose(w_route, (0, 2, 1)),
                  ((0, 0), (0, ROUTE_ROWS - n_logits), (0, 0))).astype(BF16)
    brt = jnp.broadcast_to(
        jnp.pad(b_route, ((0, 0), (0, ROUTE_ROWS - n_logits)))[:, :, None],
        (depth, ROUTE_ROWS, T_MIX))
    triu = jnp.triu(jnp.ones((T_MIX, T_MIX), BF16), k=1)
    wg = moe_w_gate.reshape(depth * MOE_EXPERTS, D_MODEL, MOE_FF)
    wu = moe_w_up.reshape(depth * MOE_EXPERTS, D_MODEL, MOE_FF)
    wd = moe_w_down.reshape(depth * MOE_EXPERTS, MOE_FF, D_MODEL)
    p2 = p.reshape(depth * n, PLE_DIM)

    ns = seq // T_MIX
    h = x.reshape(n, D_MODEL)
    finish = None
    for i in range(depth):
        if i % 2 == 0:
            kind, params = "gla", (g_mix, gla_win, gla_wgu, gla_bg, gla_gh, gla_w_out)
        else:
            kind, params = "sgu", (g_mix, sgu_w_in, sgu_lng, sgu_lnb, sgu_w_s, sgu_bs, sgu_w_out)
        h, dest3, cnt = _mixer_layer(kind, i, ns, h, finish, params, (g_ffn, wrt, brt, triu))
        xs = _dispatch(h, dest3, cnt)
        ys = _experts(xs, i, cnt, g_ffn, wr, br, wg, wu, wd)
        finish = (dest3, p2, ys, (g_ple, ple_w_up, ple_w_gate))
    h = _final_layer(h, dest3, p2, ys, depth - 1, g_ple, ple_w_up, ple_w_gate, g_fin)
    return h.reshape(batch, seq, D_MODEL)
```

```python
import functools

import jax
import jax.numpy as jnp
from jax import lax
from jax.experimental import pallas as pl
from jax.experimental.pallas import tpu as pltpu

F32 = jnp.float32
BF16 = jnp.bfloat16

D_MODEL = 1024
DEPTH = 4
GLA_HEADS = 4
GLA_DK = 512
GLA_DV = 1024
GLA_HK = GLA_DK // GLA_HEADS
GLA_HV = GLA_DV // GLA_HEADS
GLA_RANK = 16
GLA_TAU = 16.0
GLA_CHUNK = 64
SGU_WIDTH = 2048
SGU_GROUPS = 8
SGU_GC = SGU_WIDTH // SGU_GROUPS
SGU_CHUNK = 128
MOE_GROUPS = 4
MOE_EPG = 4
MOE_EXPERTS = MOE_GROUPS * MOE_EPG
MOE_FF = 512
PLE_DIM = 256
EPS = 1e-6

LANES = 128
ROW_SUB = D_MODEL // LANES
GLA_A_PAD = LANES
GLA_ZW = 2 * GLA_DK + 2 * GLA_DV + GLA_A_PAD
ROUTE_ROWS = 32
ROUTE_OUT_ROWS = 8
PAIRS = ((0, 1), (0, 2), (1, 2), (1, 3), (2, 3), (0, 3))
N_PAIRS = len(PAIRS)
N_BUCKETS = MOE_GROUPS * N_PAIRS

T_MIX = 256
SGU_VBLK = 512
T_ROW = T_MIX
T_MOE = 256
MOE_UNIT = 128
MOE_UNIT_LOG2 = 7
assert 1 << MOE_UNIT_LOG2 == MOE_UNIT and T_MOE == 2 * MOE_UNIT
DMA_UNROLL = 8
CAST_ROWS = 256
STAGE_ROWS = 64
VMEM_LIMIT = 56 * 1024 * 1024


def _rms(x, g):
    ms = jnp.mean(x * x, axis=-1, keepdims=True)
    return x * lax.rsqrt(ms + EPS) * g


def _dot(a, b):
    return jnp.dot(a, b, preferred_element_type=F32)


def _dot_nt(a, b):
    return lax.dot_general(a, b, (((1,), (1,)), ((), ())), preferred_element_type=F32)


def _dot_tn(a, b):
    return lax.dot_general(a, b, (((0,), (0,)), ((), ())), preferred_element_type=F32)


def _params(sem):
    return pltpu.CompilerParams(dimension_semantics=sem, vmem_limit_bytes=VMEM_LIMIT)


def _resident(block_shape, index_map):
    return pl.BlockSpec(block_shape, index_map, pipeline_mode=pl.Buffered(1))


def _cast_weight(dst_ref, src_ref):
    rows = src_ref.shape[0]
    for r in range(0, rows, CAST_ROWS):
        dst_ref[r:r + CAST_ROWS, :] = src_ref[r:r + CAST_ROWS, :].astype(BF16)


def _store_row_major(dst_ref, x):
    m = x.shape[0]
    for c in range(ROW_SUB):
        dst_ref[pl.ds(c, m, stride=ROW_SUB), :] = x[:, c * LANES:(c + 1) * LANES]


def _load_row_major(src_ref, m):
    return jnp.concatenate(
        [src_ref[pl.ds(c, m, stride=ROW_SUB), :] for c in range(ROW_SUB)], axis=-1)


def _row_block(ref, row, rows):
    start = row * ROW_SUB
    if not isinstance(start, int):
        start = pl.multiple_of(start, ROW_SUB)
    return ref.at[pl.ds(start, rows * ROW_SUB), :]


def _stage_weight(dst_ref, w_hbm_ref, stage_ref, sem):
    rows, ch = dst_ref.shape[0], stage_ref.shape[1]

    def chunk(k):
        return pltpu.make_async_copy(w_hbm_ref.at[pl.ds(k * ch, ch), :], stage_ref.at[k % 2],
                                     sem.at[k % 2])

    chunk(0).start()
    for k in range(rows // ch):
        if (k + 1) * ch < rows:
            chunk(k + 1).start()
        chunk(k).wait()
        dst_ref[k * ch:(k + 1) * ch, :] = stage_ref[k % 2].astype(BF16)


def _units(count):
    return lax.shift_right_logical(count + (MOE_UNIT - 1), MOE_UNIT_LOG2)


def _bucket_layout(cnt_ref, starts_ref):
    def body(b, row):
        starts_ref[b] = row
        return row + _units(cnt_ref[b, 0]) * MOE_UNIT
    return lax.fori_loop(0, N_BUCKETS, body, jnp.int32(0))


def _gather_start(t, dcur_ref, dnext_ref, ys_ref, ybuf_ref, sem):
    def row_copy(dref, slot, r):
        return pltpu.make_async_copy(_row_block(ys_ref, dref[0, 0, r], 1),
                                     _row_block(ybuf_ref.at[slot], r, 1), sem.at[slot])

    @pl.when(t == 0)
    def _():
        def body(k, carry):
            base = pl.multiple_of(k * DMA_UNROLL, DMA_UNROLL)
            for u in range(DMA_UNROLL):
                row_copy(dcur_ref, 0, base + u).start()
            return carry
        lax.fori_loop(0, T_ROW // DMA_UNROLL, body, 0)

    def issue_next(k, n):
        for r in range(k * T_ROW // n, (k + 1) * T_ROW // n):
            row_copy(dnext_ref, (t + 1) % 2, r).start()

    return issue_next


def _gather_wait(slot, ys_ref, ybuf_ref, sem):
    pltpu.make_async_copy(_row_block(ys_ref, 0, T_ROW), ybuf_ref.at[slot], sem.at[slot]).wait()


def _gather_finish(t, ys_ref, ybuf_ref, sem):
    _gather_wait(t % 2, ys_ref, ybuf_ref, sem)
    return _load_row_major(ybuf_ref.at[t % 2], T_ROW)


def _gather_drain(t, nt, ys_ref, ybuf_ref, sem):
    @pl.when(t == nt - 1)
    def _():
        _gather_wait((t + 1) % 2, ys_ref, ybuf_ref, sem)


def _ple(h2, p_ref, gple_ref, wupb_ref, wgateb_ref):
    pn = _rms(h2, gple_ref[...]).astype(BF16)
    gate = jax.nn.sigmoid(_dot(pn, wgateb_ref[...]))
    up = _dot(p_ref[...].astype(BF16), wupb_ref[...])
    return h2 + up * gate


def _finish_specs(i, nt):
    return [
        pl.BlockSpec((1, 1, T_ROW), lambda t: (t, 0, 0), memory_space=pltpu.SMEM),
        pl.BlockSpec((1, 1, T_ROW), lambda t: (jnp.minimum(t + 1, nt - 1), 0, 0),
                     memory_space=pltpu.SMEM),
        pl.BlockSpec((T_ROW, D_MODEL), lambda t: (t, 0)),
        pl.BlockSpec((T_ROW, PLE_DIM), lambda t: (i * nt + t, 0)),
        pl.BlockSpec(memory_space=pl.ANY),
        pl.BlockSpec((None, 1, D_MODEL), lambda t: (i, 0, 0)),
        _resident((None, PLE_DIM, D_MODEL), lambda t: (i, 0, 0)),
        _resident((None, D_MODEL, D_MODEL), lambda t: (i, 0, 0)),
    ]


N_FINISH_INPUTS = 8

_FINISH_SCRATCH = [
    pltpu.VMEM((PLE_DIM, D_MODEL), BF16),
    pltpu.VMEM((D_MODEL, D_MODEL), BF16),
    pltpu.VMEM((2, T_ROW * ROW_SUB, LANES), F32),
    pltpu.SemaphoreType.DMA((2,)),
]


def _gla_body(h, tick, gmix_ref, wgu_ref, bg_ref, ghead_ref, winb_ref, woutb_ref,
              st_ref, y_ref):
    hn = _rms(h, gmix_ref[...]).astype(BF16)
    r0 = 2 * GLA_DK + GLA_DV
    a = _dot(hn, winb_ref[:, r0 + GLA_DV:]).astype(BF16)
    tick(0, 4)
    pre = _dot(a, wgu_ref[...].astype(BF16)) + bg_ref[...]
    la = (jnp.minimum(pre, 0.0) - jnp.log1p(jnp.exp(-jnp.abs(pre)))) * (1.0 / GLA_TAU)
    zqk = _dot(hn, winb_ref[:, 0:2 * GLA_DK])
    tick(1, 4)
    zv = _dot(hn, winb_ref[:, 2 * GLA_DK:r0])
    tick(2, 4)
    zr = _dot(hn, winb_ref[:, r0:r0 + GLA_DV])
    tick(3, 4)

    C = GLA_CHUNK
    ri = lax.broadcasted_iota(jnp.int32, (C, C), 0)
    ci = lax.broadcasted_iota(jnp.int32, (C, C), 1)
    causal = ri >= ci
    tri = causal.astype(BF16)

    for c in range(T_MIX // C):
        rows = slice(c * C, (c + 1) * C)
        la_c = la[rows]
        hi = la_c.astype(BF16)
        lo = (la_c - hi.astype(F32)).astype(BF16)
        bc = _dot(tri, hi) + _dot(tri, lo)
        bend = bc[C - 1:C, :]
        kc = zqk[rows, GLA_DK:2 * GLA_DK]
        qd = (zqk[rows, 0:GLA_DK] * (GLA_HK ** -0.5) * jnp.exp(bc)).astype(BF16)
        kd = (kc * jnp.exp(-bc)).astype(BF16)
        ke = (kc * jnp.exp(bend - bc)).astype(BF16)
        vc = zv[rows].astype(BF16)
        dec = jnp.exp(bend)
        for hd in range(GLA_HEADS):
            ks = slice(hd * GLA_HK, (hd + 1) * GLA_HK)
            vs = slice(hd * GLA_HV, (hd + 1) * GLA_HV)
            att = jnp.where(causal, _dot_nt(qd[:, ks], kd[:, ks]), 0.0).astype(BF16)
            st = st_ref[hd]
            o = _dot(att, vc[:, vs]) + _dot_nt(qd[:, ks], st.astype(BF16))
            st_ref[hd] = st * dec[:, ks] + _dot_tn(vc[:, vs], ke[:, ks])
            y_ref[rows, vs] = (jax.nn.silu(zr[rows, vs]) * _rms(o, ghead_ref[:, vs])).astype(BF16)
    return h + _dot(y_ref[...], woutb_ref[...])


def _sgu_body(h, tick, gmix_ref, lng_ref, lnb_ref, bs_ref, winb_ref, woutb_ref, wsb_ref,
              us_ref, v_ref):
    C = SGU_CHUNK
    hn = _rms(h, gmix_ref[...]).astype(BF16)

    vsum = jnp.zeros((T_MIX, 1), F32)
    for k in range(SGU_WIDTH // SGU_VBLK):
        cols = slice(k * SGU_VBLK, (k + 1) * SGU_VBLK)
        vb = jax.nn.gelu(_dot(hn, winb_ref[:, SGU_WIDTH + k * SGU_VBLK:SGU_WIDTH + (k + 1) * SGU_VBLK]),
                         approximate=True)
        v_ref[:, cols] = vb
        vsum = vsum + jnp.sum(vb, axis=-1, keepdims=True)
        tick(k, SGU_WIDTH // SGU_VBLK)
    mean = vsum * (1.0 / SGU_WIDTH)
    vsq = jnp.zeros((T_MIX, 1), F32)
    for k in range(SGU_WIDTH // SGU_VBLK):
        cols = slice(k * SGU_VBLK, (k + 1) * SGU_VBLK)
        xc = v_ref[:, cols] - mean
        vsq = vsq + jnp.sum(xc * xc, axis=-1, keepdims=True)
    rstd = lax.rsqrt(vsq * (1.0 / SGU_WIDTH) + EPS)

    for g in range(SGU_GROUPS):
        cols = slice(g * SGU_GC, (g + 1) * SGU_GC)
        u = jax.nn.gelu(_dot(hn, winb_ref[:, cols]), approximate=True)
        vn = (((v_ref[:, cols] - mean) * rstd) * lng_ref[:, cols] + lnb_ref[:, cols]).astype(BF16)
        wc = wsb_ref[g]
        for c in range(T_MIX // C):
            rows = slice(c * C, (c + 1) * C)
            s = _dot(wc, vn[rows]) + bs_ref[:, cols]
            us_ref[rows, cols] = (u[rows] * s).astype(BF16)
    return h + _dot(us_ref[...], woutb_ref[...])


def _mixer_kernel(*refs, kind, fused, j, ns):
    refs = list(refs)
    t = pl.program_id(0)
    nt = pl.num_programs(0)
    if fused:
        (dcur_ref, dnext_ref, hprev_ref, p_ref, ys_ref, gple_ref, wup_ref,
         wgate_ref) = refs[:N_FINISH_INPUTS]
        wupb_ref, wgateb_ref, ybuf_ref, gsem = refs[-len(_FINISH_SCRATCH):]
        refs = refs[N_FINISH_INPUTS:-len(_FINISH_SCRATCH)]
        tick = _gather_start(t, dcur_ref, dnext_ref, ys_ref, ybuf_ref, gsem)
    else:
        h_ref, refs = refs[0], refs[1:]
        tick = lambda k, n: None
    cnt_ref, route_ref = refs[-2:]
    refs = refs[:-2]
    if kind == "gla":
        (gmix_ref, win_hbm, wgu_ref, bg_ref, ghead_ref, wout_ref,
         gffn_ref, wrt_ref, brt_ref, triu_ref, o_ref, dest_ref, counts_ref,
         winb_ref, woutb_ref, stage_ref, ssem, st_ref, y_ref) = refs
    else:
        (gmix_ref, win_hbm, lng_ref, lnb_ref, ws_ref, bs_ref, wout_ref,
         gffn_ref, wrt_ref, brt_ref, triu_ref, o_ref, dest_ref, counts_ref,
         winb_ref, woutb_ref, wsb_ref, stage_ref, ssem, us_ref, v_ref) = refs

    @pl.when(t == 0)
    def _():
        _stage_weight(winb_ref, win_hbm.at[j], stage_ref, ssem)
        _cast_weight(woutb_ref, wout_ref)
        if fused:
            _cast_weight(wupb_ref, wup_ref)
            _cast_weight(wgateb_ref, wgate_ref)
        if kind == "sgu":
            ri = lax.broadcasted_iota(jnp.int32, (SGU_CHUNK, SGU_CHUNK), 0)
            ci = lax.broadcasted_iota(jnp.int32, (SGU_CHUNK, SGU_CHUNK), 1)
            for g in range(SGU_GROUPS):
                wsb_ref[g] = jnp.where(ri >= ci, ws_ref[g], 0.0).astype(BF16)

    if kind == "gla":
        @pl.when(t % ns == 0)
        def _():
            st_ref[...] = jnp.zeros_like(st_ref)

    if fused:
        h2 = hprev_ref[...] + _gather_finish(t, ys_ref, ybuf_ref, gsem)
        h = _ple(h2, p_ref, gple_ref, wupb_ref, wgateb_ref)
    else:
        h = h_ref[...]

    if kind == "gla":
        out = _gla_body(h, tick, gmix_ref, wgu_ref, bg_ref, ghead_ref, winb_ref,
                        woutb_ref, st_ref, y_ref)
    else:
        out = _sgu_body(h, tick, gmix_ref, lng_ref, lnb_ref, bs_ref, winb_ref,
                        woutb_ref, wsb_ref, us_ref, v_ref)
    o_ref[...] = out
    _route_tile(out, t, gffn_ref, wrt_ref, brt_ref, triu_ref, cnt_ref, route_ref)

    @pl.when(t == nt - 1)
    def _():
        _emit_dest(nt, cnt_ref, route_ref, dest_ref, counts_ref)
    if fused:
        _gather_drain(t, nt, ys_ref, ybuf_ref, gsem)


def _mixer_layer(kind, i, ns, h, finish, params, router):
    n = h.shape[0]
    nt = n // T_MIX
    j = i // 2
    tile = lambda t: (t, 0)
    if kind == "gla":
        zw = GLA_ZW
        mixer_specs = [
            pl.BlockSpec((None, 1, D_MODEL), lambda t: (i, 0, 0)),
            pl.BlockSpec(memory_space=pl.ANY),
            pl.BlockSpec((None, GLA_A_PAD, GLA_DK), lambda t: (j, 0, 0)),
            pl.BlockSpec((None, 1, GLA_DK), lambda t: (j, 0, 0)),
            pl.BlockSpec((None, 1, GLA_DV), lambda t: (j, 0, 0)),
            _resident((None, GLA_DV, D_MODEL), lambda t: (j, 0, 0)),
        ]
        mixer_scratch = [
            pltpu.VMEM((D_MODEL, zw), BF16),
            pltpu.VMEM((GLA_DV, D_MODEL), BF16),
            pltpu.VMEM((2, STAGE_ROWS, zw), F32),
            pltpu.SemaphoreType.DMA((2,)),
            pltpu.VMEM((GLA_HEADS, GLA_HV, GLA_HK), F32),
            pltpu.VMEM((T_MIX, GLA_DV), BF16),
        ]
    else:
        zw = 2 * SGU_WIDTH
        mixer_specs = [
            pl.BlockSpec((None, 1, D_MODEL), lambda t: (i, 0, 0)),
            pl.BlockSpec(memory_space=pl.ANY),
            pl.BlockSpec((None, 1, SGU_WIDTH), lambda t: (j, 0, 0)),
            pl.BlockSpec((None, 1, SGU_WIDTH), lambda t: (j, 0, 0)),
            _resident((None, SGU_GROUPS, SGU_CHUNK, SGU_CHUNK), lambda t: (j, 0, 0, 0)),
            _resident((None, SGU_CHUNK, SGU_WIDTH), lambda t: (j, 0, 0)),
            _resident((None, SGU_WIDTH, D_MODEL), lambda t: (j, 0, 0)),
        ]
        mixer_scratch = [
            pltpu.VMEM((D_MODEL, zw), BF16),
            pltpu.VMEM((SGU_WIDTH, D_MODEL), BF16),
            pltpu.VMEM((SGU_GROUPS, SGU_CHUNK, SGU_CHUNK), BF16),
            pltpu.VMEM((2, STAGE_ROWS, zw), F32),
            pltpu.SemaphoreType.DMA((2,)),
            pltpu.VMEM((T_MIX, SGU_WIDTH), BF16),
            pltpu.VMEM((T_MIX, SGU_WIDTH), F32),
        ]
    if finish is None:
        head_specs = [pl.BlockSpec((T_MIX, D_MODEL), tile)]
        head_args = (h,)
        tail_scratch = []
    else:
        dest3, p2, ys, (gple, wup, wgate) = finish
        head_specs = _finish_specs(i - 1, nt)
        head_args = (dest3, dest3, h, p2, ys, gple, wup, wgate)
        tail_scratch = _FINISH_SCRATCH
    router_specs = [
        pl.BlockSpec((None, 1, D_MODEL), lambda t: (i, 0, 0)),
        pl.BlockSpec((None, ROUTE_ROWS, D_MODEL), lambda t: (i, 0, 0)),
        pl.BlockSpec((None, ROUTE_ROWS, T_MIX), lambda t: (i, 0, 0)),
        pl.BlockSpec((T_MIX, T_MIX), lambda t: (0, 0)),
    ]
    router_scratch = [
        pltpu.VMEM((ROUTE_ROWS, LANES), F32),
        pltpu.VMEM((nt, ROUTE_OUT_ROWS, T_MIX), jnp.int32),
    ]
    return pl.pallas_call(
        functools.partial(_mixer_kernel, kind=kind, fused=finish is not None, j=j, ns=ns),
        grid=(nt,),
        in_specs=head_specs + mixer_specs + router_specs,
        out_specs=[
            pl.BlockSpec((T_MIX, D_MODEL), tile),
            pl.BlockSpec((nt, 1, T_MIX), lambda t: (0, 0, 0)),
            pl.BlockSpec((ROUTE_ROWS, LANES), lambda t: (0, 0)),
        ],
        out_shape=[
            jax.ShapeDtypeStruct((n, D_MODEL), F32),
            jax.ShapeDtypeStruct((nt, 1, T_MIX), jnp.int32),
            jax.ShapeDtypeStruct((ROUTE_ROWS, LANES), jnp.int32),
        ],
        scratch_shapes=mixer_scratch + router_scratch + tail_scratch,
        compiler_params=_params(("arbitrary",)),
        name=kind + ("_fused" if finish is not None else "") + "_mixer",
    )(*head_args, *params, *router)


def _first_argmax(rows):
    best = rows[0]
    idx = jnp.zeros(best.shape, jnp.int32)
    for j in range(1, len(rows)):
        upd = rows[j] > best
        idx = jnp.where(upd, j, idx)
        best = jnp.where(upd, rows[j], best)
    return idx, best


def _route_tile(h, t, gffn_ref, wrt_ref, brt_ref, triu_ref, cnt_ref, route_ref):
    @pl.when(t == 0)
    def _():
        cnt_ref[...] = jnp.zeros_like(cnt_ref)

    hb = _rms(h, gffn_ref[...]).astype(BF16)
    lg = _dot_nt(wrt_ref[...], hb) + brt_ref[...]
    row = lambda j: lg[j:j + 1, :]
    g, _ = _first_argmax([row(j) for j in range(MOE_GROUPS)])
    ex = []
    for e in range(MOE_EPG):
        v = row(MOE_GROUPS + e)
        for gg in range(1, MOE_GROUPS):
            v = jnp.where(g == gg, row(MOE_GROUPS + gg * MOE_EPG + e), v)
        ex.append(v)
    i1, _ = _first_argmax(ex)
    neg = jnp.full(ex[0].shape, -jnp.inf, F32)
    i2, _ = _first_argmax([jnp.where(i1 == e, neg, ex[e]) for e in range(MOE_EPG)])
    lo = jnp.minimum(i1, i2)
    hi = jnp.maximum(i1, i2)
    pair = jnp.zeros(lo.shape, jnp.int32)
    for p, (a, b) in enumerate(PAIRS):
        pair = jnp.where((lo == a) & (hi == b), p, pair)
    bucket = g * N_PAIRS + pair

    m = bucket.shape[1]
    onehot = (lax.broadcasted_iota(jnp.int32, (ROUTE_ROWS, m), 0) == bucket).astype(F32)
    before = _dot(onehot.astype(BF16), triu_ref[...])
    rank = jnp.sum(onehot * (before + cnt_ref[:, 0:1]), axis=0, keepdims=True).astype(jnp.int32)
    cnt_ref[...] = cnt_ref[...] + jnp.sum(onehot, axis=1, keepdims=True)
    r8 = lax.broadcasted_iota(jnp.int32, (ROUTE_OUT_ROWS, m), 0)
    route_ref[t] = jnp.where(r8 == 0, bucket, jnp.where(r8 == 1, rank, 0))


def _emit_dest(nt, cnt_ref, route_ref, dest_ref, counts_ref):
    cnt = cnt_ref[...].astype(jnp.int32)
    counts_ref[...] = cnt
    units = _units(cnt).astype(F32).astype(BF16)
    ri = lax.broadcasted_iota(jnp.int32, (ROUTE_ROWS, ROUTE_ROWS), 0)
    ci = lax.broadcasted_iota(jnp.int32, (ROUTE_ROWS, ROUTE_ROWS), 1)
    before = _dot((ci < ri).astype(BF16), units)
    start = (before[:, 0:1] * MOE_UNIT).astype(jnp.int32)

    def body(k, carry):
        blk = route_ref[k]
        onehot = lax.broadcasted_iota(jnp.int32, (ROUTE_ROWS, T_MIX), 0) == blk[0:1, :]
        dest_ref[k] = jnp.sum(jnp.where(onehot, start, 0), axis=0, keepdims=True) + blk[1:2, :]
        return carry

    lax.fori_loop(0, nt, body, 0)


def _dispatch_kernel(cnt_ref, dest_ref, h_ref, xs_ref, zero_ref, rows_ref, starts_ref, zsem, sem):
    @pl.when(pl.program_id(0) == 0)
    def _():
        end = _bucket_layout(cnt_ref, starts_ref)
        zero_ref[...] = jnp.zeros_like(zero_ref)
        fills = []
        for b in range(N_BUCKETS):
            c = cnt_ref[b, 0]
            whole = lax.shift_right_logical(c, MOE_UNIT_LOG2) * MOE_UNIT
            fills.append(((c & (MOE_UNIT - 1)) != 0, starts_ref[b] + whole))
        for k in range(N_BUCKETS):
            fills.append((end + k * MOE_UNIT < xs_ref.shape[0] // ROW_SUB, end + k * MOE_UNIT))

        def fill(row):
            return pltpu.make_async_copy(zero_ref, _row_block(xs_ref, row, MOE_UNIT), zsem)

        for on, row in fills:
            pl.when(on)(lambda row=row: fill(row).start())
        for on, row in fills:
            pl.when(on)(lambda row=row: fill(row).wait())

    _store_row_major(rows_ref, h_ref[...])

    def issue(k, carry):
        base = pl.multiple_of(k * DMA_UNROLL, DMA_UNROLL)
        for u in range(DMA_UNROLL):
            d = dest_ref[0, 0, base + u]
            pltpu.make_async_copy(_row_block(rows_ref, base + u, 1),
                                  _row_block(xs_ref, d, 1), sem).start(priority=u % 2)
        return carry

    lax.fori_loop(0, T_ROW // DMA_UNROLL, issue, 0)
    pltpu.make_async_copy(rows_ref, _row_block(xs_ref, 0, T_ROW), sem).wait()


def _dispatch(h, dest3, cnt):
    n = h.shape[0]
    return pl.pallas_call(
        _dispatch_kernel,
        grid=(n // T_ROW,),
        in_specs=[
            pl.BlockSpec(memory_space=pltpu.SMEM),
            pl.BlockSpec((1, 1, T_ROW), lambda t: (t, 0, 0), memory_space=pltpu.SMEM),
            pl.BlockSpec((T_ROW, D_MODEL), lambda t: (t, 0)),
        ],
        out_specs=pl.BlockSpec(memory_space=pl.ANY),
        out_shape=jax.ShapeDtypeStruct((_sorted_rows(n) * ROW_SUB, LANES), F32),
        scratch_shapes=[
            pltpu.VMEM((MOE_UNIT * ROW_SUB, LANES), F32),
            pltpu.VMEM((T_ROW * ROW_SUB, LANES), F32),
            pltpu.SMEM((N_BUCKETS,), jnp.int32),
            pltpu.SemaphoreType.DMA(()),
            pltpu.SemaphoreType.DMA(()),
        ],
        compiler_params=_params(("arbitrary",)),
        name="moe_dispatch",
    )(cnt, dest3, h)


def _experts_kernel(bg_ref, bea_ref, beb_ref,
                    cnt_ref, xs_ref, gffn_ref, wr_ref, br_ref,
                    wga_ref, wua_ref, wda_ref, wgb_ref, wub_ref, wdb_ref,
                    ys_ref,
                    wgab_ref, wuab_ref, wdab_ref, wgbb_ref, wubb_ref, wdbb_ref,
                    xbuf_ref, ybuf_ref, isem, osem,
                    bg0_ref, bnt_ref, trow_ref, tunits_ref, ntot_ref):
    b = pl.program_id(0)
    pair = b % N_PAIRS

    @pl.when(b == 0)
    def _():
        def per_bucket(q, carry):
            row, g = carry
            units = _units(cnt_ref[q, 0])
            n_full = lax.shift_right_logical(units, 1)
            n_tiles = n_full + (units & 1)
            bg0_ref[q] = g
            bnt_ref[q] = n_tiles

            def per_tile(j, c):
                trow_ref[g + j] = row + j * T_MOE
                tunits_ref[g + j] = jnp.where(j < n_full, T_MOE // MOE_UNIT, 1)
                return c

            lax.fori_loop(0, n_tiles, per_tile, 0)
            return row + units * MOE_UNIT, g + n_tiles

        _, n_total = lax.fori_loop(0, N_BUCKETS, per_bucket, (jnp.int32(0), jnp.int32(0)))
        ntot_ref[0] = n_total

    def cast_a():
        _cast_weight(wgab_ref, wga_ref)
        _cast_weight(wuab_ref, wua_ref)
        _cast_weight(wdab_ref, wda_ref)

    def cast_b():
        _cast_weight(wgbb_ref, wgb_ref)
        _cast_weight(wubb_ref, wub_ref)
        _cast_weight(wdbb_ref, wdb_ref)

    a_changes = [q for q in range(N_PAIRS) if q == 0 or PAIRS[q][0] != PAIRS[q - 1][0]]
    b_changes = [q for q in range(N_PAIRS) if q == 0 or PAIRS[q][1] != PAIRS[q - 1][1]]
    pl.when(functools.reduce(jnp.logical_or, [pair == q for q in a_changes]))(cast_a)
    pl.when(functools.reduce(jnp.logical_or, [pair == q for q in b_changes]))(cast_b)

    g0 = bg0_ref[b]
    ntot = ntot_ref[0]
    grp = bg_ref[b]
    ea = bea_ref[b]
    eb = beb_ref[b]

    def compute(x32):
        x = _rms(x32, gffn_ref[...]).astype(BF16)
        lg = _dot(x, wr_ref[...]) + br_ref[...]
        lane = lax.broadcasted_iota(jnp.int32, lg.shape, 1)
        glog = jnp.where(lane < MOE_GROUPS, lg, -jnp.inf)
        pe = jnp.exp(glog - jnp.max(glog, axis=-1, keepdims=True))
        g_w = (jnp.sum(jnp.where(lane == grp, pe, 0.0), axis=-1, keepdims=True)
               / jnp.sum(pe, axis=-1, keepdims=True))
        la = jnp.sum(jnp.where(lane == MOE_GROUPS + ea, lg, 0.0), axis=-1, keepdims=True)
        lb = jnp.sum(jnp.where(lane == MOE_GROUPS + eb, lg, 0.0), axis=-1, keepdims=True)
        m = jnp.maximum(la, lb)
        pa = jnp.exp(la - m)
        pb = jnp.exp(lb - m)
        ca = pa / (pa + pb) * g_w
        cb = pb / (pa + pb) * g_w

        def expert(wg_ref, wu_ref, c):
            return (jax.nn.silu(_dot(x, wg_ref[...])) * _dot(x, wu_ref[...]) * c).astype(BF16)

        return (_dot(expert(wgab_ref, wuab_ref, ca), wdab_ref[...])
                + _dot(expert(wgbb_ref, wubb_ref, cb), wdbb_ref[...]))

    def load(g, rows):
        return pltpu.make_async_copy(_row_block(xs_ref, trow_ref[g], rows),
                                     _row_block(xbuf_ref.at[g % 2], 0, rows), isem.at[g % 2])

    def store(g, rows):
        return pltpu.make_async_copy(_row_block(ybuf_ref.at[g % 2], 0, rows),
                                     _row_block(ys_ref, trow_ref[g], rows), osem.at[g % 2])

    def by_size(g, fn):
        pl.when(tunits_ref[g] == T_MOE // MOE_UNIT)(lambda: fn(T_MOE))
        pl.when(tunits_ref[g] == 1)(lambda: fn(MOE_UNIT))

    @pl.when((b == 0) & (ntot > 0))
    def _():
        by_size(0, lambda rows: load(0, rows).start())

    def body(j, carry):
        g = g0 + j

        @pl.when(g + 1 < ntot)
        def _():
            by_size(g + 1, lambda rows: load(g + 1, rows).start())

        @pl.when(g >= 2)
        def _():
            by_size(g - 2, lambda rows: store(g - 2, rows).wait())

        def run(rows):
            load(g, rows).wait()
            y = compute(_load_row_major(xbuf_ref.at[g % 2], rows))
            _store_row_major(ybuf_ref.at[g % 2], y)
            store(g, rows).start()

        by_size(g, run)
        return carry

    lax.fori_loop(0, bnt_ref[b], body, 0)

    @pl.when(b == N_BUCKETS - 1)
    def _():
        for back in (2, 1):
            @pl.when(ntot >= back)
            def _():
                by_size(ntot - back, lambda rows: store(ntot - back, rows).wait())


def _experts(xs, i, cnt, gffn, wr, br, wg, wu, wd):
    e0 = i * MOE_EXPERTS
    max_tiles = xs.shape[0] // ROW_SUB // T_MOE + N_BUCKETS
    bg = jnp.asarray([q // N_PAIRS for q in range(N_BUCKETS)], jnp.int32)
    bea = jnp.asarray([(q // N_PAIRS) * MOE_EPG + PAIRS[q % N_PAIRS][0] for q in range(N_BUCKETS)],
                      jnp.int32)
    beb = jnp.asarray([(q // N_PAIRS) * MOE_EPG + PAIRS[q % N_PAIRS][1] for q in range(N_BUCKETS)],
                      jnp.int32)
    ea_map = lambda b, g, ea, eb: (e0 + ea[b], 0, 0)
    eb_map = lambda b, g, ea, eb: (e0 + eb[b], 0, 0)
    up_spec = lambda m: pl.BlockSpec((None, D_MODEL, MOE_FF), m)
    down_spec = lambda m: pl.BlockSpec((None, MOE_FF, D_MODEL), m)
    grid_spec = pltpu.PrefetchScalarGridSpec(
        num_scalar_prefetch=3,
        grid=(N_BUCKETS,),
        in_specs=[
            pl.BlockSpec(memory_space=pltpu.SMEM),
            pl.BlockSpec(memory_space=pl.ANY),
            pl.BlockSpec((None, 1, D_MODEL), lambda b, *_: (i, 0, 0)),
            pl.BlockSpec((None, D_MODEL, LANES), lambda b, *_: (i, 0, 0)),
            pl.BlockSpec((None, 1, LANES), lambda b, *_: (i, 0, 0)),
            up_spec(ea_map), up_spec(ea_map), down_spec(ea_map),
            up_spec(eb_map), up_spec(eb_map), down_spec(eb_map),
        ],
        out_specs=pl.BlockSpec(memory_space=pl.ANY),
        scratch_shapes=[
            pltpu.VMEM((D_MODEL, MOE_FF), BF16), pltpu.VMEM((D_MODEL, MOE_FF), BF16),
            pltpu.VMEM((MOE_FF, D_MODEL), BF16),
            pltpu.VMEM((D_MODEL, MOE_FF), BF16), pltpu.VMEM((D_MODEL, MOE_FF), BF16),
            pltpu.VMEM((MOE_FF, D_MODEL), BF16),
            pltpu.VMEM((2, T_MOE * ROW_SUB, LANES), F32),
            pltpu.VMEM((2, T_MOE * ROW_SUB, LANES), F32),
            pltpu.SemaphoreType.DMA((2,)),
            pltpu.SemaphoreType.DMA((2,)),
            pltpu.SMEM((N_BUCKETS,), jnp.int32), pltpu.SMEM((N_BUCKETS,), jnp.int32),
            pltpu.SMEM((max_tiles,), jnp.int32), pltpu.SMEM((max_tiles,), jnp.int32),
            pltpu.SMEM((1,), jnp.int32),
        ],
    )
    return pl.pallas_call(
        _experts_kernel,
        grid_spec=grid_spec,
        out_shape=jax.ShapeDtypeStruct(xs.shape, F32),
        input_output_aliases={4: 0},
        compiler_params=_params(("arbitrary",)),
        name="moe_experts",
    )(bg, bea, beb, cnt, xs, gffn, wr, br, wg, wu, wd, wg, wu, wd)


def _final_kernel(dcur_ref, dnext_ref, h_ref, p_ref, ys_ref, gple_ref, wup_ref, wgate_ref,
                  gfin_ref, o_ref, wupb_ref, wgateb_ref, ybuf_ref, sem):
    t = pl.program_id(0)
    _gather_start(t, dcur_ref, dnext_ref, ys_ref, ybuf_ref, sem)(0, 1)

    @pl.when(t == 0)
    def _():
        _cast_weight(wupb_ref, wup_ref)
        _cast_weight(wgateb_ref, wgate_ref)

    h2 = h_ref[...] + _gather_finish(t, ys_ref, ybuf_ref, sem)
    o_ref[...] = _rms(_ple(h2, p_ref, gple_ref, wupb_ref, wgateb_ref), gfin_ref[...])
    _gather_drain(t, pl.num_programs(0), ys_ref, ybuf_ref, sem)


def _final_layer(h, dest3, p2, ys, i, gple, wup, wgate, gfin):
    n = h.shape[0]
    nt = n // T_ROW
    return pl.pallas_call(
        _final_kernel,
        grid=(nt,),
        in_specs=_finish_specs(i, nt) + [pl.BlockSpec((1, D_MODEL), lambda t: (0, 0))],
        out_specs=pl.BlockSpec((T_ROW, D_MODEL), lambda t: (t, 0)),
        out_shape=jax.ShapeDtypeStruct((n, D_MODEL), F32),
        scratch_shapes=_FINISH_SCRATCH,
        compiler_params=_params(("arbitrary",)),
        name="final_combine_ple",
    )(dest3, dest3, h, p2, ys, gple, wup, wgate, gfin)


def _sorted_rows(n):
    return n + N_BUCKETS * MOE_UNIT


def kernel(x, p, gla_w_in, gla_w_gate_up, gla_b_gate, gla_g_head, gla_w_out, sgu_w_in, sgu_ln_g,
           sgu_ln_b, sgu_w_s, sgu_b_s, sgu_w_out, norm_mix_g, norm_ffn_g, norm_ple_g,
           moe_w_route_group, moe_b_route_group, moe_w_route_expert, moe_b_route_expert,
           moe_w_gate, moe_w_up, moe_w_down, ple_w_up, ple_w_gate, final_norm_g):
    batch, seq, d = x.shape
    assert d == D_MODEL and seq % T_MIX == 0
    n = batch * seq
    depth = p.shape[0]
    assert depth == DEPTH
    na, nb = gla_w_in.shape[0], sgu_w_in.shape[0]

    gla_win = jnp.pad(gla_w_in, ((0, 0), (0, 0), (0, GLA_A_PAD - GLA_RANK)))
    gla_wgu = jnp.pad(gla_w_gate_up, ((0, 0), (0, GLA_A_PAD - GLA_RANK), (0, 0)))
    gla_bg = gla_b_gate.reshape(na, 1, GLA_DK)
    gla_gh = gla_g_head.reshape(na, 1, GLA_DV)
    sgu_lng = sgu_ln_g.reshape(nb, 1, SGU_WIDTH)
    sgu_lnb = sgu_ln_b.reshape(nb, 1, SGU_WIDTH)
    sgu_bs = jnp.repeat(jnp.transpose(sgu_b_s, (0, 2, 1)), SGU_GC, axis=-1)
    g_mix = norm_mix_g.reshape(depth, 1, D_MODEL)
    g_ffn = norm_ffn_g.reshape(depth, 1, D_MODEL)
    g_ple = norm_ple_g.reshape(depth, 1, D_MODEL)
    g_fin = final_norm_g.reshape(1, D_MODEL)
    w_route = jnp.concatenate(
        [moe_w_route_group, moe_w_route_expert.reshape(depth, D_MODEL, MOE_EXPERTS)], axis=-1)
    b_route = jnp.concatenate(
        [moe_b_route_group, moe_b_route_expert.reshape(depth, MOE_EXPERTS)], axis=-1)
    n_logits = MOE_GROUPS + MOE_EXPERTS
    wr = jnp.pad(w_route, ((0, 0), (0, 0), (0, LANES - n_logits))).astype(BF16)
    br = jnp.pad(b_route, ((0, 0), (0, LANES - n_logits))).reshape(depth, 1, LANES)
    wrt = jnp.pad(jnp.transpose(w_route, (0, 2, 1)),
                  ((0, 0), (0, ROUTE_ROWS - n_logits), (0, 0))).astype(BF16)
    brt = jnp.broadcast_to(
        jnp.pad(b_route, ((0, 0), (0, ROUTE_ROWS - n_logits)))[:, :, None],
        (depth, ROUTE_ROWS, T_MIX))
    triu = jnp.triu(jnp.ones((T_MIX, T_MIX), BF16), k=1)
    wg = moe_w_gate.reshape(depth * MOE_EXPERTS, D_MODEL, MOE_FF)
    wu = moe_w_up.reshape(depth * MOE_EXPERTS, D_MODEL, MOE_FF)
    wd = moe_w_down.reshape(depth * MOE_EXPERTS, MOE_FF, D_MODEL)
    p2 = p.reshape(depth * n, PLE_DIM)

    ns = seq // T_MIX
    h = x.reshape(n, D_MODEL)
    finish = None
    for i in range(depth):
        if i % 2 == 0:
            kind, params = "gla", (g_mix, gla_win, gla_wgu, gla_bg, gla_gh, gla_w_out)
        else:
            kind, params = "sgu", (g_mix, sgu_w_in, sgu_lng, sgu_lnb, sgu_w_s, sgu_bs, sgu_w_out)
        h, dest3, cnt = _mixer_layer(kind, i, ns, h, finish, params, (g_ffn, wrt, brt, triu))
        xs = _dispatch(h, dest3, cnt)
        ys = _experts(xs, i, cnt, g_ffn, wr, br, wg, wu, wd)
        finish = (dest3, p2, ys, (g_ple, ple_w_up, ple_w_gate))
    h = _final_layer(h, dest3, p2, ys, depth - 1, g_ple, ple_w_up, ple_w_gate, g_fin)
    return h.reshape(batch, seq, D_MODEL)
```

```python
import functools

import jax
import jax.numpy as jnp
from jax import lax
from jax.experimental import pallas as pl
from jax.experimental.pallas import tpu as pltpu

F32 = jnp.float32
BF16 = jnp.bfloat16

D_MODEL = 1024
DEPTH = 4
GLA_HEADS = 4
GLA_DK = 512
GLA_DV = 1024
GLA_HK = GLA_DK // GLA_HEADS
GLA_HV = GLA_DV // GLA_HEADS
GLA_RANK = 16
GLA_TAU = 16.0
GLA_CHUNK = 64
SGU_WIDTH = 2048
SGU_GROUPS = 8
SGU_GC = SGU_WIDTH // SGU_GROUPS
SGU_CHUNK = 128
MOE_GROUPS = 4
MOE_EPG = 4
MOE_EXPERTS = MOE_GROUPS * MOE_EPG
MOE_FF = 512
PLE_DIM = 256
EPS = 1e-6

LANES = 128
ROW_SUB = D_MODEL // LANES
GLA_A_PAD = LANES
GLA_ZW = 2 * GLA_DK + 2 * GLA_DV + GLA_A_PAD
ROUTE_ROWS = 32
ROUTE_OUT_ROWS = 8
PAIRS = ((0, 1), (0, 2), (1, 2), (1, 3), (2, 3), (0, 3))
N_PAIRS = len(PAIRS)
N_BUCKETS = MOE_GROUPS * N_PAIRS

T_MIX = 256
SGU_VBLK = 512
T_ROW = T_MIX
T_MOE = 256
X_SLOTS = 3
MOE_UNIT = 128
MOE_UNIT_LOG2 = 7
assert 1 << MOE_UNIT_LOG2 == MOE_UNIT and T_MOE == 2 * MOE_UNIT
DMA_UNROLL = 8
CAST_ROWS = 256
STAGE_ROWS = 64
VMEM_LIMIT = 56 * 1024 * 1024


def _rms(x, g):
    ms = jnp.mean(x * x, axis=-1, keepdims=True)
    return x * lax.rsqrt(ms + EPS) * g


def _dot(a, b):
    return jnp.dot(a, b, preferred_element_type=F32)


def _dot_nt(a, b):
    return lax.dot_general(a, b, (((1,), (1,)), ((), ())), preferred_element_type=F32)


def _dot_tn(a, b):
    return lax.dot_general(a, b, (((0,), (0,)), ((), ())), preferred_element_type=F32)


def _params(sem):
    return pltpu.CompilerParams(dimension_semantics=sem, vmem_limit_bytes=VMEM_LIMIT)


def _resident(block_shape, index_map):
    return pl.BlockSpec(block_shape, index_map, pipeline_mode=pl.Buffered(1))


def _cast_weight(dst_ref, src_ref):
    rows = src_ref.shape[0]
    for r in range(0, rows, CAST_ROWS):
        dst_ref[r:r + CAST_ROWS, :] = src_ref[r:r + CAST_ROWS, :].astype(BF16)


def _store_row_major(dst_ref, x):
    m = x.shape[0]
    for c in range(ROW_SUB):
        dst_ref[pl.ds(c, m, stride=ROW_SUB), :] = x[:, c * LANES:(c + 1) * LANES]


def _load_row_major(src_ref, m):
    return jnp.concatenate(
        [src_ref[pl.ds(c, m, stride=ROW_SUB), :] for c in range(ROW_SUB)], axis=-1)


def _row_block(ref, row, rows):
    start = row * ROW_SUB
    if not isinstance(start, int):
        start = pl.multiple_of(start, ROW_SUB)
    return ref.at[pl.ds(start, rows * ROW_SUB), :]


def _stage_weight(dst_ref, w_hbm_ref, stage_ref, sem):
    rows, ch = dst_ref.shape[0], stage_ref.shape[1]

    def chunk(k):
        return pltpu.make_async_copy(w_hbm_ref.at[pl.ds(k * ch, ch), :], stage_ref.at[k % 2],
                                     sem.at[k % 2])

    chunk(0).start()
    for k in range(rows // ch):
        if (k + 1) * ch < rows:
            chunk(k + 1).start()
        chunk(k).wait()
        dst_ref[k * ch:(k + 1) * ch, :] = stage_ref[k % 2].astype(BF16)


def _units(count):
    return lax.shift_right_logical(count + (MOE_UNIT - 1), MOE_UNIT_LOG2)


def _bucket_layout(cnt_ref, starts_ref):
    def body(b, row):
        starts_ref[b] = row
        return row + _units(cnt_ref[b, 0]) * MOE_UNIT
    return lax.fori_loop(0, N_BUCKETS, body, jnp.int32(0))


def _gather_start(t, dcur_ref, dnext_ref, ys_ref, ybuf_ref, sem):
    def row_copy(dref, slot, r):
        return pltpu.make_async_copy(_row_block(ys_ref, dref[0, 0, r], 1),
                                     _row_block(ybuf_ref.at[slot], r, 1), sem.at[slot])

    @pl.when(t == 0)
    def _():
        def body(k, carry):
            base = pl.multiple_of(k * DMA_UNROLL, DMA_UNROLL)
            for u in range(DMA_UNROLL):
                row_copy(dcur_ref, 0, base + u).start()
            return carry
        lax.fori_loop(0, T_ROW // DMA_UNROLL, body, 0)

    def issue_next(k, n):
        for r in range(k * T_ROW // n, (k + 1) * T_ROW // n):
            row_copy(dnext_ref, (t + 1) % 2, r).start()

    return issue_next


def _gather_wait(slot, ys_ref, ybuf_ref, sem):
    pltpu.make_async_copy(_row_block(ys_ref, 0, T_ROW), ybuf_ref.at[slot], sem.at[slot]).wait()


def _gather_finish(t, ys_ref, ybuf_ref, sem):
    _gather_wait(t % 2, ys_ref, ybuf_ref, sem)
    return _load_row_major(ybuf_ref.at[t % 2], T_ROW)


def _gather_drain(t, nt, ys_ref, ybuf_ref, sem):
    @pl.when(t == nt - 1)
    def _():
        _gather_wait((t + 1) % 2, ys_ref, ybuf_ref, sem)


def _ple(h2, p_ref, gple_ref, wupb_ref, wgateb_ref):
    pn = _rms(h2, gple_ref[...]).astype(BF16)
    gate = jax.nn.sigmoid(_dot(pn, wgateb_ref[...]))
    up = _dot(p_ref[...].astype(BF16), wupb_ref[...])
    return h2 + up * gate


def _finish_specs(i, nt):
    return [
        pl.BlockSpec((1, 1, T_ROW), lambda t: (t, 0, 0), memory_space=pltpu.SMEM),
        pl.BlockSpec((1, 1, T_ROW), lambda t: (jnp.minimum(t + 1, nt - 1), 0, 0),
                     memory_space=pltpu.SMEM),
        pl.BlockSpec((T_ROW, D_MODEL), lambda t: (t, 0)),
        pl.BlockSpec((T_ROW, PLE_DIM), lambda t: (i * nt + t, 0)),
        pl.BlockSpec(memory_space=pl.ANY),
        pl.BlockSpec((None, 1, D_MODEL), lambda t: (i, 0, 0)),
        _resident((None, PLE_DIM, D_MODEL), lambda t: (i, 0, 0)),
        _resident((None, D_MODEL, D_MODEL), lambda t: (i, 0, 0)),
    ]


N_FINISH_INPUTS = 8

_FINISH_SCRATCH = [
    pltpu.VMEM((PLE_DIM, D_MODEL), BF16),
    pltpu.VMEM((D_MODEL, D_MODEL), BF16),
    pltpu.VMEM((2, T_ROW * ROW_SUB, LANES), F32),
    pltpu.SemaphoreType.DMA((2,)),
]


def _gla_body(h, tick, gmix_ref, wgu_ref, bg_ref, ghead_ref, winb_ref, woutb_ref,
              st_ref, y_ref):
    hn = _rms(h, gmix_ref[...]).astype(BF16)
    r0 = 2 * GLA_DK + GLA_DV
    a = _dot(hn, winb_ref[:, r0 + GLA_DV:]).astype(BF16)
    tick(0, 4)
    pre = _dot(a, wgu_ref[...].astype(BF16)) + bg_ref[...]
    la = (jnp.minimum(pre, 0.0) - jnp.log1p(jnp.exp(-jnp.abs(pre)))) * (1.0 / GLA_TAU)
    zqk = _dot(hn, winb_ref[:, 0:2 * GLA_DK])
    tick(1, 4)
    zv = _dot(hn, winb_ref[:, 2 * GLA_DK:r0])
    tick(2, 4)
    zr = _dot(hn, winb_ref[:, r0:r0 + GLA_DV])
    tick(3, 4)

    C = GLA_CHUNK
    ri = lax.broadcasted_iota(jnp.int32, (C, C), 0)
    ci = lax.broadcasted_iota(jnp.int32, (C, C), 1)
    causal = ri >= ci
    tri = causal.astype(BF16)

    for c in range(T_MIX // C):
        rows = slice(c * C, (c + 1) * C)
        la_c = la[rows]
        hi = la_c.astype(BF16)
        lo = (la_c - hi.astype(F32)).astype(BF16)
        bc = _dot(tri, hi) + _dot(tri, lo)
        bend = bc[C - 1:C, :]
        kc = zqk[rows, GLA_DK:2 * GLA_DK]
        qd = (zqk[rows, 0:GLA_DK] * (GLA_HK ** -0.5) * jnp.exp(bc)).astype(BF16)
        kd = (kc * jnp.exp(-bc)).astype(BF16)
        ke = (kc * jnp.exp(bend - bc)).astype(BF16)
        vc = zv[rows].astype(BF16)
        dec = jnp.exp(bend)
        for hd in range(GLA_HEADS):
            ks = slice(hd * GLA_HK, (hd + 1) * GLA_HK)
            vs = slice(hd * GLA_HV, (hd + 1) * GLA_HV)
            att = jnp.where(causal, _dot_nt(qd[:, ks], kd[:, ks]), 0.0).astype(BF16)
            st = st_ref[hd]
            o = _dot(att, vc[:, vs]) + _dot_nt(qd[:, ks], st.astype(BF16))
            st_ref[hd] = st * dec[:, ks] + _dot_tn(vc[:, vs], ke[:, ks])
            y_ref[rows, vs] = (jax.nn.silu(zr[rows, vs]) * _rms(o, ghead_ref[:, vs])).astype(BF16)
    return h + _dot(y_ref[...], woutb_ref[...])


def _sgu_body(h, tick, gmix_ref, lng_ref, lnb_ref, bs_ref, winb_ref, woutb_ref, wsb_ref,
              us_ref, v_ref):
    C = SGU_CHUNK
    hn = _rms(h, gmix_ref[...]).astype(BF16)

    vsum = jnp.zeros((T_MIX, 1), F32)
    for k in range(SGU_WIDTH // SGU_VBLK):
        cols = slice(k * SGU_VBLK, (k + 1) * SGU_VBLK)
        vb = jax.nn.gelu(_dot(hn, winb_ref[:, SGU_WIDTH + k * SGU_VBLK:SGU_WIDTH + (k + 1) * SGU_VBLK]),
                         approximate=True)
        v_ref[:, cols] = vb
        vsum = vsum + jnp.sum(vb, axis=-1, keepdims=True)
        tick(k, SGU_WIDTH // SGU_VBLK)
    mean = vsum * (1.0 / SGU_WIDTH)
    vsq = jnp.zeros((T_MIX, 1), F32)
    for k in range(SGU_WIDTH // SGU_VBLK):
        cols = slice(k * SGU_VBLK, (k + 1) * SGU_VBLK)
        xc = v_ref[:, cols] - mean
        vsq = vsq + jnp.sum(xc * xc, axis=-1, keepdims=True)
    rstd = lax.rsqrt(vsq * (1.0 / SGU_WIDTH) + EPS)

    for g in range(SGU_GROUPS):
        cols = slice(g * SGU_GC, (g + 1) * SGU_GC)
        u = jax.nn.gelu(_dot(hn, winb_ref[:, cols]), approximate=True)
        vn = (((v_ref[:, cols] - mean) * rstd) * lng_ref[:, cols] + lnb_ref[:, cols]).astype(BF16)
        wc = wsb_ref[g]
        for c in range(T_MIX // C):
            rows = slice(c * C, (c + 1) * C)
            s = _dot(wc, vn[rows]) + bs_ref[:, cols]
            us_ref[rows, cols] = (u[rows] * s).astype(BF16)
    return h + _dot(us_ref[...], woutb_ref[...])


def _mixer_kernel(*refs, kind, fused, j, ns):
    refs = list(refs)
    t = pl.program_id(0)
    nt = pl.num_programs(0)
    if fused:
        (dcur_ref, dnext_ref, hprev_ref, p_ref, ys_ref, gple_ref, wup_ref,
         wgate_ref) = refs[:N_FINISH_INPUTS]
        wupb_ref, wgateb_ref, ybuf_ref, gsem = refs[-len(_FINISH_SCRATCH):]
        refs = refs[N_FINISH_INPUTS:-len(_FINISH_SCRATCH)]
        tick = _gather_start(t, dcur_ref, dnext_ref, ys_ref, ybuf_ref, gsem)
    else:
        h_ref, refs = refs[0], refs[1:]
        tick = lambda k, n: None
    cnt_ref, route_ref = refs[-2:]
    refs = refs[:-2]
    if kind == "gla":
        (gmix_ref, win_hbm, wgu_ref, bg_ref, ghead_ref, wout_ref,
         gffn_ref, wrt_ref, brt_ref, triu_ref, o_ref, dest_ref, counts_ref,
         winb_ref, woutb_ref, stage_ref, ssem, st_ref, y_ref) = refs
    else:
        (gmix_ref, win_hbm, lng_ref, lnb_ref, ws_ref, bs_ref, wout_ref,
         gffn_ref, wrt_ref, brt_ref, triu_ref, o_ref, dest_ref, counts_ref,
         winb_ref, woutb_ref, wsb_ref, stage_ref, ssem, us_ref, v_ref) = refs

    @pl.when(t == 0)
    def _():
        _stage_weight(winb_ref, win_hbm.at[j], stage_ref, ssem)
        _cast_weight(woutb_ref, wout_ref)
        if fused:
            _cast_weight(wupb_ref, wup_ref)
            _cast_weight(wgateb_ref, wgate_ref)
        if kind == "sgu":
            ri = lax.broadcasted_iota(jnp.int32, (SGU_CHUNK, SGU_CHUNK), 0)
            ci = lax.broadcasted_iota(jnp.int32, (SGU_CHUNK, SGU_CHUNK), 1)
            for g in range(SGU_GROUPS):
                wsb_ref[g] = jnp.where(ri >= ci, ws_ref[g], 0.0).astype(BF16)

    if kind == "gla":
        @pl.when(t % ns == 0)
        def _():
            st_ref[...] = jnp.zeros_like(st_ref)

    if fused:
        h2 = hprev_ref[...] + _gather_finish(t, ys_ref, ybuf_ref, gsem)
        h = _ple(h2, p_ref, gple_ref, wupb_ref, wgateb_ref)
    else:
        h = h_ref[...]

    if kind == "gla":
        out = _gla_body(h, tick, gmix_ref, wgu_ref, bg_ref, ghead_ref, winb_ref,
                        woutb_ref, st_ref, y_ref)
    else:
        out = _sgu_body(h, tick, gmix_ref, lng_ref, lnb_ref, bs_ref, winb_ref,
                        woutb_ref, wsb_ref, us_ref, v_ref)
    o_ref[...] = out
    _route_tile(out, t, gffn_ref, wrt_ref, brt_ref, triu_ref, cnt_ref, route_ref)

    @pl.when(t == nt - 1)
    def _():
        _emit_dest(nt, cnt_ref, route_ref, dest_ref, counts_ref)
    if fused:
        _gather_drain(t, nt, ys_ref, ybuf_ref, gsem)


def _mixer_layer(kind, i, ns, h, finish, params, router):
    n = h.shape[0]
    nt = n // T_MIX
    j = i // 2
    tile = lambda t: (t, 0)
    if kind == "gla":
        zw = GLA_ZW
        mixer_specs = [
            pl.BlockSpec((None, 1, D_MODEL), lambda t: (i, 0, 0)),
            pl.BlockSpec(memory_space=pl.ANY),
            pl.BlockSpec((None, GLA_A_PAD, GLA_DK), lambda t: (j, 0, 0)),
            pl.BlockSpec((None, 1, GLA_DK), lambda t: (j, 0, 0)),
            pl.BlockSpec((None, 1, GLA_DV), lambda t: (j, 0, 0)),
            _resident((None, GLA_DV, D_MODEL), lambda t: (j, 0, 0)),
        ]
        mixer_scratch = [
            pltpu.VMEM((D_MODEL, zw), BF16),
            pltpu.VMEM((GLA_DV, D_MODEL), BF16),
            pltpu.VMEM((2, STAGE_ROWS, zw), F32),
            pltpu.SemaphoreType.DMA((2,)),
            pltpu.VMEM((GLA_HEADS, GLA_HV, GLA_HK), F32),
            pltpu.VMEM((T_MIX, GLA_DV), BF16),
        ]
    else:
        zw = 2 * SGU_WIDTH
        mixer_specs = [
            pl.BlockSpec((None, 1, D_MODEL), lambda t: (i, 0, 0)),
            pl.BlockSpec(memory_space=pl.ANY),
            pl.BlockSpec((None, 1, SGU_WIDTH), lambda t: (j, 0, 0)),
            pl.BlockSpec((None, 1, SGU_WIDTH), lambda t: (j, 0, 0)),
            _resident((None, SGU_GROUPS, SGU_CHUNK, SGU_CHUNK), lambda t: (j, 0, 0, 0)),
            _resident((None, SGU_CHUNK, SGU_WIDTH), lambda t: (j, 0, 0)),
            _resident((None, SGU_WIDTH, D_MODEL), lambda t: (j, 0, 0)),
        ]
        mixer_scratch = [
            pltpu.VMEM((D_MODEL, zw), BF16),
            pltpu.VMEM((SGU_WIDTH, D_MODEL), BF16),
            pltpu.VMEM((SGU_GROUPS, SGU_CHUNK, SGU_CHUNK), BF16),
            pltpu.VMEM((2, STAGE_ROWS, zw), F32),
            pltpu.SemaphoreType.DMA((2,)),
            pltpu.VMEM((T_MIX, SGU_WIDTH), BF16),
            pltpu.VMEM((T_MIX, SGU_WIDTH), F32),
        ]
    if finish is None:
        head_specs = [pl.BlockSpec((T_MIX, D_MODEL), tile)]
        head_args = (h,)
        tail_scratch = []
    else:
        dest3, p2, ys, (gple, wup, wgate) = finish
        head_specs = _finish_specs(i - 1, nt)
        head_args = (dest3, dest3, h, p2, ys, gple, wup, wgate)
        tail_scratch = _FINISH_SCRATCH
    router_specs = [
        pl.BlockSpec((None, 1, D_MODEL), lambda t: (i, 0, 0)),
        pl.BlockSpec((None, ROUTE_ROWS, D_MODEL), lambda t: (i, 0, 0)),
        pl.BlockSpec((None, ROUTE_ROWS, T_MIX), lambda t: (i, 0, 0)),
        pl.BlockSpec((T_MIX, T_MIX), lambda t: (0, 0)),
    ]
    router_scratch = [
        pltpu.VMEM((ROUTE_ROWS, LANES), F32),
        pltpu.VMEM((nt, ROUTE_OUT_ROWS, T_MIX), jnp.int32),
    ]
    return pl.pallas_call(
        functools.partial(_mixer_kernel, kind=kind, fused=finish is not None, j=j, ns=ns),
        grid=(nt,),
        in_specs=head_specs + mixer_specs + router_specs,
        out_specs=[
            pl.BlockSpec((T_MIX, D_MODEL), tile),
            pl.BlockSpec((nt, 1, T_MIX), lambda t: (0, 0, 0)),
            pl.BlockSpec((ROUTE_ROWS, LANES), lambda t: (0, 0)),
        ],
        out_shape=[
            jax.ShapeDtypeStruct((n, D_MODEL), F32),
            jax.ShapeDtypeStruct((nt, 1, T_MIX), jnp.int32),
            jax.ShapeDtypeStruct((ROUTE_ROWS, LANES), jnp.int32),
        ],
        scratch_shapes=mixer_scratch + router_scratch + tail_scratch,
        compiler_params=_params(("arbitrary",)),
        name=kind + ("_fused" if finish is not None else "") + "_mixer",
    )(*head_args, *params, *router)


def _first_argmax(rows):
    best = rows[0]
    idx = jnp.zeros(best.shape, jnp.int32)
    for j in range(1, len(rows)):
        upd = rows[j] > best
        idx = jnp.where(upd, j, idx)
        best = jnp.where(upd, rows[j], best)
    return idx, best


def _route_tile(h, t, gffn_ref, wrt_ref, brt_ref, triu_ref, cnt_ref, route_ref):
    @pl.when(t == 0)
    def _():
        cnt_ref[...] = jnp.zeros_like(cnt_ref)

    hb = _rms(h, gffn_ref[...]).astype(BF16)
    lg = _dot_nt(wrt_ref[...], hb) + brt_ref[...]
    row = lambda j: lg[j:j + 1, :]
    g, _ = _first_argmax([row(j) for j in range(MOE_GROUPS)])
    ex = []
    for e in range(MOE_EPG):
        v = row(MOE_GROUPS + e)
        for gg in range(1, MOE_GROUPS):
            v = jnp.where(g == gg, row(MOE_GROUPS + gg * MOE_EPG + e), v)
        ex.append(v)
    i1, _ = _first_argmax(ex)
    neg = jnp.full(ex[0].shape, -jnp.inf, F32)
    i2, _ = _first_argmax([jnp.where(i1 == e, neg, ex[e]) for e in range(MOE_EPG)])
    lo = jnp.minimum(i1, i2)
    hi = jnp.maximum(i1, i2)
    pair = jnp.zeros(lo.shape, jnp.int32)
    for p, (a, b) in enumerate(PAIRS):
        pair = jnp.where((lo == a) & (hi == b), p, pair)
    bucket = g * N_PAIRS + pair

    m = bucket.shape[1]
    onehot = (lax.broadcasted_iota(jnp.int32, (ROUTE_ROWS, m), 0) == bucket).astype(F32)
    before = _dot(onehot.astype(BF16), triu_ref[...])
    rank = jnp.sum(onehot * (before + cnt_ref[:, 0:1]), axis=0, keepdims=True).astype(jnp.int32)
    cnt_ref[...] = cnt_ref[...] + jnp.sum(onehot, axis=1, keepdims=True)
    r8 = lax.broadcasted_iota(jnp.int32, (ROUTE_OUT_ROWS, m), 0)
    route_ref[t] = jnp.where(r8 == 0, bucket, jnp.where(r8 == 1, rank, 0))


def _emit_dest(nt, cnt_ref, route_ref, dest_ref, counts_ref):
    cnt = cnt_ref[...].astype(jnp.int32)
    counts_ref[...] = cnt
    units = _units(cnt).astype(F32).astype(BF16)
    ri = lax.broadcasted_iota(jnp.int32, (ROUTE_ROWS, ROUTE_ROWS), 0)
    ci = lax.broadcasted_iota(jnp.int32, (ROUTE_ROWS, ROUTE_ROWS), 1)
    before = _dot((ci < ri).astype(BF16), units)
    start = (before[:, 0:1] * MOE_UNIT).astype(jnp.int32)

    def body(k, carry):
        blk = route_ref[k]
        onehot = lax.broadcasted_iota(jnp.int32, (ROUTE_ROWS, T_MIX), 0) == blk[0:1, :]
        dest_ref[k] = jnp.sum(jnp.where(onehot, start, 0), axis=0, keepdims=True) + blk[1:2, :]
        return carry

    lax.fori_loop(0, nt, body, 0)


def _dispatch_kernel(cnt_ref, dest_ref, h_ref, xs_ref, zero_ref, rows_ref, starts_ref, zsem, sem):
    @pl.when(pl.program_id(0) == 0)
    def _():
        end = _bucket_layout(cnt_ref, starts_ref)
        zero_ref[...] = jnp.zeros_like(zero_ref)
        fills = []
        for b in range(N_BUCKETS):
            c = cnt_ref[b, 0]
            whole = lax.shift_right_logical(c, MOE_UNIT_LOG2) * MOE_UNIT
            fills.append(((c & (MOE_UNIT - 1)) != 0, starts_ref[b] + whole))
        for k in range(N_BUCKETS):
            fills.append((end + k * MOE_UNIT < xs_ref.shape[0] // ROW_SUB, end + k * MOE_UNIT))

        def fill(row):
            return pltpu.make_async_copy(zero_ref, _row_block(xs_ref, row, MOE_UNIT), zsem)

        for on, row in fills:
            pl.when(on)(lambda row=row: fill(row).start())
        for on, row in fills:
            pl.when(on)(lambda row=row: fill(row).wait())

    t = pl.program_id(0)
    nt = pl.num_programs(0)
    slot = t % 2

    def drain(s):
        pltpu.make_async_copy(rows_ref.at[s], _row_block(xs_ref, 0, T_ROW), sem.at[s]).wait()

    pl.when(t >= 2)(lambda: drain(slot))
    _store_row_major(rows_ref.at[slot], h_ref[...])

    def issue(k, carry):
        base = pl.multiple_of(k * DMA_UNROLL, DMA_UNROLL)
        for u in range(DMA_UNROLL):
            d = dest_ref[0, 0, base + u]
            pltpu.make_async_copy(_row_block(rows_ref.at[slot], base + u, 1),
                                  _row_block(xs_ref, d, 1), sem.at[slot]).start(priority=u % 2)
        return carry

    lax.fori_loop(0, T_ROW // DMA_UNROLL, issue, 0)

    @pl.when(t == nt - 1)
    def _():
        pl.when(nt >= 2)(lambda: drain(1 - slot))
        drain(slot)


def _dispatch(h, dest3, cnt):
    n = h.shape[0]
    return pl.pallas_call(
        _dispatch_kernel,
        grid=(n // T_ROW,),
        in_specs=[
            pl.BlockSpec(memory_space=pltpu.SMEM),
            pl.BlockSpec((1, 1, T_ROW), lambda t: (t, 0, 0), memory_space=pltpu.SMEM),
            pl.BlockSpec((T_ROW, D_MODEL), lambda t: (t, 0)),
        ],
        out_specs=pl.BlockSpec(memory_space=pl.ANY),
        out_shape=jax.ShapeDtypeStruct((_sorted_rows(n) * ROW_SUB, LANES), F32),
        scratch_shapes=[
            pltpu.VMEM((MOE_UNIT * ROW_SUB, LANES), F32),
            pltpu.VMEM((2, T_ROW * ROW_SUB, LANES), F32),
            pltpu.SMEM((N_BUCKETS,), jnp.int32),
            pltpu.SemaphoreType.DMA(()),
            pltpu.SemaphoreType.DMA((2,)),
        ],
        compiler_params=_params(("arbitrary",)),
        name="moe_dispatch",
    )(cnt, dest3, h)


def _experts_kernel(bg_ref, bea_ref, beb_ref,
                    cnt_ref, xs_ref, gffn_ref, wr_ref, br_ref,
                    wga_ref, wua_ref, wda_ref, wgb_ref, wub_ref, wdb_ref,
                    ys_ref,
                    wgab_ref, wuab_ref, wdab_ref, wgbb_ref, wubb_ref, wdbb_ref,
                    xbuf_ref, ybuf_ref, isem, osem,
                    bg0_ref, bnt_ref, trow_ref, tunits_ref, ntot_ref):
    b = pl.program_id(0)
    pair = b % N_PAIRS

    @pl.when(b == 0)
    def _():
        def per_bucket(q, carry):
            row, g = carry
            units = _units(cnt_ref[q, 0])
            n_full = lax.shift_right_logical(units, 1)
            n_tiles = n_full + (units & 1)
            bg0_ref[q] = g
            bnt_ref[q] = n_tiles

            def per_tile(j, c):
                trow_ref[g + j] = row + j * T_MOE
                tunits_ref[g + j] = jnp.where(j < n_full, T_MOE // MOE_UNIT, 1)
                return c

            lax.fori_loop(0, n_tiles, per_tile, 0)
            return row + units * MOE_UNIT, g + n_tiles

        _, n_total = lax.fori_loop(0, N_BUCKETS, per_bucket, (jnp.int32(0), jnp.int32(0)))
        ntot_ref[0] = n_total

    def cast_a():
        _cast_weight(wgab_ref, wga_ref)
        _cast_weight(wuab_ref, wua_ref)
        _cast_weight(wdab_ref, wda_ref)

    def cast_b():
        _cast_weight(wgbb_ref, wgb_ref)
        _cast_weight(wubb_ref, wub_ref)
        _cast_weight(wdbb_ref, wdb_ref)

    a_changes = [q for q in range(N_PAIRS) if q == 0 or PAIRS[q][0] != PAIRS[q - 1][0]]
    b_changes = [q for q in range(N_PAIRS) if q == 0 or PAIRS[q][1] != PAIRS[q - 1][1]]
    pl.when(functools.reduce(jnp.logical_or, [pair == q for q in a_changes]))(cast_a)
    pl.when(functools.reduce(jnp.logical_or, [pair == q for q in b_changes]))(cast_b)

    g0 = bg0_ref[b]
    ntot = ntot_ref[0]
    grp = bg_ref[b]
    ea = bea_ref[b]
    eb = beb_ref[b]

    def compute(x32):
        x = _rms(x32, gffn_ref[...]).astype(BF16)
        lg = _dot(x, wr_ref[...]) + br_ref[...]
        lane = lax.broadcasted_iota(jnp.int32, lg.shape, 1)
        glog = jnp.where(lane < MOE_GROUPS, lg, -jnp.inf)
        pe = jnp.exp(glog - jnp.max(glog, axis=-1, keepdims=True))
        g_w = (jnp.sum(jnp.where(lane == grp, pe, 0.0), axis=-1, keepdims=True)
               / jnp.sum(pe, axis=-1, keepdims=True))
        la = jnp.sum(jnp.where(lane == MOE_GROUPS + ea, lg, 0.0), axis=-1, keepdims=True)
        lb = jnp.sum(jnp.where(lane == MOE_GROUPS + eb, lg, 0.0), axis=-1, keepdims=True)
        m = jnp.maximum(la, lb)
        pa = jnp.exp(la - m)
        pb = jnp.exp(lb - m)
        ca = pa / (pa + pb) * g_w
        cb = pb / (pa + pb) * g_w

        def expert(wg_ref, wu_ref, c):
            return (jax.nn.silu(_dot(x, wg_ref[...])) * _dot(x, wu_ref[...]) * c).astype(BF16)

        return (_dot(expert(wgab_ref, wuab_ref, ca), wdab_ref[...])
                + _dot(expert(wgbb_ref, wubb_ref, cb), wdbb_ref[...]))

    def load(g, rows):
        slot = lax.rem(g, X_SLOTS)
        return pltpu.make_async_copy(_row_block(xs_ref, trow_ref[g], rows),
                                     _row_block(xbuf_ref.at[slot], 0, rows), isem.at[slot])

    def store(g, rows):
        return pltpu.make_async_copy(_row_block(ybuf_ref.at[g % 2], 0, rows),
                                     _row_block(ys_ref, trow_ref[g], rows), osem.at[g % 2])

    def by_size(g, fn):
        pl.when(tunits_ref[g] == T_MOE // MOE_UNIT)(lambda: fn(T_MOE))
        pl.when(tunits_ref[g] == 1)(lambda: fn(MOE_UNIT))

    @pl.when(b == 0)
    def _():
        for ahead in range(X_SLOTS - 1):
            @pl.when(ahead < ntot)
            def _():
                by_size(ahead, lambda rows: load(ahead, rows).start())

    def body(j, carry):
        g = g0 + j
        nxt = g + (X_SLOTS - 1)

        @pl.when(nxt < ntot)
        def _():
            by_size(nxt, lambda rows: load(nxt, rows).start())

        @pl.when(g >= 2)
        def _():
            by_size(g - 2, lambda rows: store(g - 2, rows).wait())

        def run(rows):
            load(g, rows).wait()
            y = compute(_load_row_major(xbuf_ref.at[lax.rem(g, X_SLOTS)], rows))
            _store_row_major(ybuf_ref.at[g % 2], y)
            store(g, rows).start()

        by_size(g, run)
        return carry

    lax.fori_loop(0, bnt_ref[b], body, 0)

    @pl.when(b == N_BUCKETS - 1)
    def _():
        for back in (2, 1):
            @pl.when(ntot >= back)
            def _():
                by_size(ntot - back, lambda rows: store(ntot - back, rows).wait())


def _experts(xs, i, cnt, gffn, wr, br, wg, wu, wd):
    e0 = i * MOE_EXPERTS
    max_tiles = xs.shape[0] // ROW_SUB // T_MOE + N_BUCKETS
    bg = jnp.asarray([q // N_PAIRS for q in range(N_BUCKETS)], jnp.int32)
    bea = jnp.asarray([(q // N_PAIRS) * MOE_EPG + PAIRS[q % N_PAIRS][0] for q in range(N_BUCKETS)],
                      jnp.int32)
    beb = jnp.asarray([(q // N_PAIRS) * MOE_EPG + PAIRS[q % N_PAIRS][1] for q in range(N_BUCKETS)],
                      jnp.int32)
    ea_map = lambda b, g, ea, eb: (e0 + ea[b], 0, 0)
    eb_map = lambda b, g, ea, eb: (e0 + eb[b], 0, 0)
    up_spec = lambda m: pl.BlockSpec((None, D_MODEL, MOE_FF), m)
    down_spec = lambda m: pl.BlockSpec((None, MOE_FF, D_MODEL), m)
    grid_spec = pltpu.PrefetchScalarGridSpec(
        num_scalar_prefetch=3,
        grid=(N_BUCKETS,),
        in_specs=[
            pl.BlockSpec(memory_space=pltpu.SMEM),
            pl.BlockSpec(memory_space=pl.ANY),
            pl.BlockSpec((None, 1, D_MODEL), lambda b, *_: (i, 0, 0)),
            pl.BlockSpec((None, D_MODEL, LANES), lambda b, *_: (i, 0, 0)),
            pl.BlockSpec((None, 1, LANES), lambda b, *_: (i, 0, 0)),
            up_spec(ea_map), up_spec(ea_map), down_spec(ea_map),
            up_spec(eb_map), up_spec(eb_map), down_spec(eb_map),
        ],
        out_specs=pl.BlockSpec(memory_space=pl.ANY),
        scratch_shapes=[
            pltpu.VMEM((D_MODEL, MOE_FF), BF16), pltpu.VMEM((D_MODEL, MOE_FF), BF16),
            pltpu.VMEM((MOE_FF, D_MODEL), BF16),
            pltpu.VMEM((D_MODEL, MOE_FF), BF16), pltpu.VMEM((D_MODEL, MOE_FF), BF16),
            pltpu.VMEM((MOE_FF, D_MODEL), BF16),
            pltpu.VMEM((X_SLOTS, T_MOE * ROW_SUB, LANES), F32),
            pltpu.VMEM((2, T_MOE * ROW_SUB, LANES), F32),
            pltpu.SemaphoreType.DMA((X_SLOTS,)),
            pltpu.SemaphoreType.DMA((2,)),
            pltpu.SMEM((N_BUCKETS,), jnp.int32), pltpu.SMEM((N_BUCKETS,), jnp.int32),
            pltpu.SMEM((max_tiles,), jnp.int32), pltpu.SMEM((max_tiles,), jnp.int32),
            pltpu.SMEM((1,), jnp.int32),
        ],
    )
    return pl.pallas_call(
        _experts_kernel,
        grid_spec=grid_spec,
        out_shape=jax.ShapeDtypeStruct(xs.shape, F32),
        input_output_aliases={4: 0},
        compiler_params=_params(("arbitrary",)),
        name="moe_experts",
    )(bg, bea, beb, cnt, xs, gffn, wr, br, wg, wu, wd, wg, wu, wd)


def _final_kernel(dcur_ref, dnext_ref, h_ref, p_ref, ys_ref, gple_ref, wup_ref, wgate_ref,
                  gfin_ref, o_ref, wupb_ref, wgateb_ref, ybuf_ref, sem):
    t = pl.program_id(0)
    _gather_start(t, dcur_ref, dnext_ref, ys_ref, ybuf_ref, sem)(0, 1)

    @pl.when(t == 0)
    def _():
        _cast_weight(wupb_ref, wup_ref)
        _cast_weight(wgateb_ref, wgate_ref)

    h2 = h_ref[...] + _gather_finish(t, ys_ref, ybuf_ref, sem)
    o_ref[...] = _rms(_ple(h2, p_ref, gple_ref, wupb_ref, wgateb_ref), gfin_ref[...])
    _gather_drain(t, pl.num_programs(0), ys_ref, ybuf_ref, sem)


def _final_layer(h, dest3, p2, ys, i, gple, wup, wgate, gfin):
    n = h.shape[0]
    nt = n // T_ROW
    return pl.pallas_call(
        _final_kernel,
        grid=(nt,),
        in_specs=_finish_specs(i, nt) + [pl.BlockSpec((1, D_MODEL), lambda t: (0, 0))],
        out_specs=pl.BlockSpec((T_ROW, D_MODEL), lambda t: (t, 0)),
        out_shape=jax.ShapeDtypeStruct((n, D_MODEL), F32),
        scratch_shapes=_FINISH_SCRATCH,
        compiler_params=_params(("arbitrary",)),
        name="final_combine_ple",
    )(dest3, dest3, h, p2, ys, gple, wup, wgate, gfin)


def _sorted_rows(n):
    return n + N_BUCKETS * MOE_UNIT


def kernel(x, p, gla_w_in, gla_w_gate_up, gla_b_gate, gla_g_head, gla_w_out, sgu_w_in, sgu_ln_g,
           sgu_ln_b, sgu_w_s, sgu_b_s, sgu_w_out, norm_mix_g, norm_ffn_g, norm_ple_g,
           moe_w_route_group, moe_b_route_group, moe_w_route_expert, moe_b_route_expert,
           moe_w_gate, moe_w_up, moe_w_down, ple_w_up, ple_w_gate, final_norm_g):
    batch, seq, d = x.shape
    assert d == D_MODEL and seq % T_MIX == 0
    n = batch * seq
    depth = p.shape[0]
    assert depth == DEPTH
    na, nb = gla_w_in.shape[0], sgu_w_in.shape[0]

    gla_win = jnp.pad(gla_w_in, ((0, 0), (0, 0), (0, GLA_A_PAD - GLA_RANK)))
    gla_wgu = jnp.pad(gla_w_gate_up, ((0, 0), (0, GLA_A_PAD - GLA_RANK), (0, 0)))
    gla_bg = gla_b_gate.reshape(na, 1, GLA_DK)
    gla_gh = gla_g_head.reshape(na, 1, GLA_DV)
    sgu_lng = sgu_ln_g.reshape(nb, 1, SGU_WIDTH)
    sgu_lnb = sgu_ln_b.reshape(nb, 1, SGU_WIDTH)
    sgu_bs = jnp.repeat(jnp.transpose(sgu_b_s, (0, 2, 1)), SGU_GC, axis=-1)
    g_mix = norm_mix_g.reshape(depth, 1, D_MODEL)
    g_ffn = norm_ffn_g.reshape(depth, 1, D_MODEL)
    g_ple = norm_ple_g.reshape(depth, 1, D_MODEL)
    g_fin = final_norm_g.reshape(1, D_MODEL)
    w_route = jnp.concatenate(
        [moe_w_route_group, moe_w_route_expert.reshape(depth, D_MODEL, MOE_EXPERTS)], axis=-1)
    b_route = jnp.concatenate(
        [moe_b_route_group, moe_b_route_expert.reshape(depth, MOE_EXPERTS)], axis=-1)
    n_logits = MOE_GROUPS + MOE_EXPERTS
    wr = jnp.pad(w_route, ((0, 0), (0, 0), (0, LANES - n_logits))).astype(BF16)
    br = jnp.pad(b_route, ((0, 0), (0, LANES - n_logits))).reshape(depth, 1, LANES)
    wrt = jnp.pad(jnp.transpose(w_route, (0, 2, 1)),
                  ((0, 0), (0, ROUTE_ROWS - n_logits), (0, 0))).astype(BF16)
    brt = jnp.broadcast_to(
        jnp.pad(b_route, ((0, 0), (0, ROUTE_ROWS - n_logits)))[:, :, None],
        (depth, ROUTE_ROWS, T_MIX))
    triu = jnp.triu(jnp.ones((T_MIX, T_MIX), BF16), k=1)
    wg = moe_w_gate.reshape(depth * MOE_EXPERTS, D_MODEL, MOE_FF)
    wu = moe_w_up.reshape(depth * MOE_EXPERTS, D_MODEL, MOE_FF)
    wd = moe_w_down.reshape(depth * MOE_EXPERTS, MOE_FF, D_MODEL)
    p2 = p.reshape(depth * n, PLE_DIM)

    ns = seq // T_MIX
    h = x.reshape(n, D_MODEL)
    finish = None
    for i in range(depth):
        if i % 2 == 0:
            kind, params = "gla", (g_mix, gla_win, gla_wgu, gla_bg, gla_gh, gla_w_out)
        else:
            kind, params = "sgu", (g_mix, sgu_w_in, sgu_lng, sgu_lnb, sgu_w_s, sgu_bs, sgu_w_out)
        h, dest3, cnt = _mixer_layer(kind, i, ns, h, finish, params, (g_ffn, wrt, brt, triu))
        xs = _dispatch(h, dest3, cnt)
        ys = _experts(xs, i, cnt, g_ffn, wr, br, wg, wu, wd)
        finish = (dest3, p2, ys, (g_ple, ple_w_up, ple_w_gate))
    h = _final_layer(h, dest3, p2, ys, depth - 1, g_ple, ple_w_up, ple_w_gate, g_fin)
    return h.reshape(batch, seq, D_MODEL)
```

```python
import functools

import jax
import jax.numpy as jnp
from jax import lax
from jax.experimental import pallas as pl
from jax.experimental.pallas import tpu as pltpu

F32 = jnp.float32
BF16 = jnp.bfloat16

D_MODEL = 1024
DEPTH = 4
GLA_HEADS = 4
GLA_DK = 512
GLA_DV = 1024
GLA_HK = GLA_DK // GLA_HEADS
GLA_HV = GLA_DV // GLA_HEADS
GLA_RANK = 16
GLA_TAU = 16.0
GLA_CHUNK = 64
SGU_WIDTH = 2048
SGU_GROUPS = 8
SGU_GC = SGU_WIDTH // SGU_GROUPS
SGU_CHUNK = 128
MOE_GROUPS = 4
MOE_EPG = 4
MOE_EXPERTS = MOE_GROUPS * MOE_EPG
MOE_FF = 512
PLE_DIM = 256
EPS = 1e-6

LANES = 128
ROW_SUB = D_MODEL // LANES
GLA_A_PAD = LANES
GLA_ZW = 2 * GLA_DK + 2 * GLA_DV + GLA_A_PAD
ROUTE_ROWS = 32
ROUTE_OUT_ROWS = 8
PAIRS = ((0, 1), (0, 2), (1, 2), (1, 3), (2, 3), (0, 3))
N_PAIRS = len(PAIRS)
N_BUCKETS = MOE_GROUPS * N_PAIRS

T_MIX = 256
SGU_VBLK = 512
T_ROW = T_MIX
T_MOE = 256
X_SLOTS = 4
MOE_UNIT = 128
MOE_UNIT_LOG2 = 7
assert 1 << MOE_UNIT_LOG2 == MOE_UNIT and T_MOE == 2 * MOE_UNIT
DMA_UNROLL = 8
CAST_ROWS = 256
STAGE_ROWS = 64
VMEM_LIMIT = 56 * 1024 * 1024


def _rms(x, g):
    ms = jnp.mean(x * x, axis=-1, keepdims=True)
    return x * lax.rsqrt(ms + EPS) * g


def _dot(a, b):
    return jnp.dot(a, b, preferred_element_type=F32)


def _dot_nt(a, b):
    return lax.dot_general(a, b, (((1,), (1,)), ((), ())), preferred_element_type=F32)


def _dot_tn(a, b):
    return lax.dot_general(a, b, (((0,), (0,)), ((), ())), preferred_element_type=F32)


def _params(sem):
    return pltpu.CompilerParams(dimension_semantics=sem, vmem_limit_bytes=VMEM_LIMIT)


def _resident(block_shape, index_map):
    return pl.BlockSpec(block_shape, index_map, pipeline_mode=pl.Buffered(1))


def _cast_weight(dst_ref, src_ref):
    rows = src_ref.shape[0]
    for r in range(0, rows, CAST_ROWS):
        dst_ref[r:r + CAST_ROWS, :] = src_ref[r:r + CAST_ROWS, :].astype(BF16)


def _store_row_major(dst_ref, x):
    m = x.shape[0]
    for c in range(ROW_SUB):
        dst_ref[pl.ds(c, m, stride=ROW_SUB), :] = x[:, c * LANES:(c + 1) * LANES]


def _load_row_major(src_ref, m):
    return jnp.concatenate(
        [src_ref[pl.ds(c, m, stride=ROW_SUB), :] for c in range(ROW_SUB)], axis=-1)


def _row_block(ref, row, rows):
    start = row * ROW_SUB
    if not isinstance(start, int):
        start = pl.multiple_of(start, ROW_SUB)
    return ref.at[pl.ds(start, rows * ROW_SUB), :]


def _stage_weight(dst_ref, w_hbm_ref, stage_ref, sem):
    rows, ch = dst_ref.shape[0], stage_ref.shape[1]

    def chunk(k):
        return pltpu.make_async_copy(w_hbm_ref.at[pl.ds(k * ch, ch), :], stage_ref.at[k % 2],
                                     sem.at[k % 2])

    chunk(0).start()
    for k in range(rows // ch):
        if (k + 1) * ch < rows:
            chunk(k + 1).start()
        chunk(k).wait()
        dst_ref[k * ch:(k + 1) * ch, :] = stage_ref[k % 2].astype(BF16)


def _units(count):
    return lax.shift_right_logical(count + (MOE_UNIT - 1), MOE_UNIT_LOG2)


def _bucket_layout(cnt_ref, starts_ref):
    def body(b, row):
        starts_ref[b] = row
        return row + _units(cnt_ref[b, 0]) * MOE_UNIT
    return lax.fori_loop(0, N_BUCKETS, body, jnp.int32(0))


def _gather_start(t, dcur_ref, dnext_ref, ys_ref, ybuf_ref, sem):
    def row_copy(dref, slot, r):
        return pltpu.make_async_copy(_row_block(ys_ref, dref[0, 0, r], 1),
                                     _row_block(ybuf_ref.at[slot], r, 1), sem.at[slot])

    @pl.when(t == 0)
    def _():
        def body(k, carry):
            base = pl.multiple_of(k * DMA_UNROLL, DMA_UNROLL)
            for u in range(DMA_UNROLL):
                row_copy(dcur_ref, 0, base + u).start()
            return carry
        lax.fori_loop(0, T_ROW // DMA_UNROLL, body, 0)

    def issue_next(k, n):
        for r in range(k * T_ROW // n, (k + 1) * T_ROW // n):
            row_copy(dnext_ref, (t + 1) % 2, r).start()

    return issue_next


def _gather_wait(slot, ys_ref, ybuf_ref, sem):
    pltpu.make_async_copy(_row_block(ys_ref, 0, T_ROW), ybuf_ref.at[slot], sem.at[slot]).wait()


def _gather_finish(t, ys_ref, ybuf_ref, sem):
    _gather_wait(t % 2, ys_ref, ybuf_ref, sem)
    return _load_row_major(ybuf_ref.at[t % 2], T_ROW)


def _gather_drain(t, nt, ys_ref, ybuf_ref, sem):
    @pl.when(t == nt - 1)
    def _():
        _gather_wait((t + 1) % 2, ys_ref, ybuf_ref, sem)


def _ple(h2, p_ref, gple_ref, wupb_ref, wgateb_ref):
    pn = _rms(h2, gple_ref[...]).astype(BF16)
    gate = jax.nn.sigmoid(_dot(pn, wgateb_ref[...]))
    up = _dot(p_ref[...].astype(BF16), wupb_ref[...])
    return h2 + up * gate


def _finish_specs(i, nt):
    return [
        pl.BlockSpec((1, 1, T_ROW), lambda t: (t, 0, 0), memory_space=pltpu.SMEM),
        pl.BlockSpec((1, 1, T_ROW), lambda t: (jnp.minimum(t + 1, nt - 1), 0, 0),
                     memory_space=pltpu.SMEM),
        pl.BlockSpec((T_ROW, D_MODEL), lambda t: (t, 0)),
        pl.BlockSpec((T_ROW, PLE_DIM), lambda t: (i * nt + t, 0)),
        pl.BlockSpec(memory_space=pl.ANY),
        pl.BlockSpec((None, 1, D_MODEL), lambda t: (i, 0, 0)),
        _resident((None, PLE_DIM, D_MODEL), lambda t: (i, 0, 0)),
        _resident((None, D_MODEL, D_MODEL), lambda t: (i, 0, 0)),
    ]


N_FINISH_INPUTS = 8

_FINISH_SCRATCH = [
    pltpu.VMEM((PLE_DIM, D_MODEL), BF16),
    pltpu.VMEM((D_MODEL, D_MODEL), BF16),
    pltpu.VMEM((2, T_ROW * ROW_SUB, LANES), F32),
    pltpu.SemaphoreType.DMA((2,)),
]


def _gla_body(h, tick, gmix_ref, wgu_ref, bg_ref, ghead_ref, winb_ref, woutb_ref,
              st_ref, y_ref):
    hn = _rms(h, gmix_ref[...]).astype(BF16)
    r0 = 2 * GLA_DK + GLA_DV
    a = _dot(hn, winb_ref[:, r0 + GLA_DV:]).astype(BF16)
    tick(0, 4)
    pre = _dot(a, wgu_ref[...].astype(BF16)) + bg_ref[...]
    la = (jnp.minimum(pre, 0.0) - jnp.log1p(jnp.exp(-jnp.abs(pre)))) * (1.0 / GLA_TAU)
    zqk = _dot(hn, winb_ref[:, 0:2 * GLA_DK])
    tick(1, 4)
    zv = _dot(hn, winb_ref[:, 2 * GLA_DK:r0])
    tick(2, 4)
    zr = _dot(hn, winb_ref[:, r0:r0 + GLA_DV])
    tick(3, 4)

    C = GLA_CHUNK
    ri = lax.broadcasted_iota(jnp.int32, (C, C), 0)
    ci = lax.broadcasted_iota(jnp.int32, (C, C), 1)
    causal = ri >= ci
    tri = causal.astype(BF16)

    for c in range(T_MIX // C):
        rows = slice(c * C, (c + 1) * C)
        la_c = la[rows]
        hi = la_c.astype(BF16)
        lo = (la_c - hi.astype(F32)).astype(BF16)
        bc = _dot(tri, hi) + _dot(tri, lo)
        bend = bc[C - 1:C, :]
        kc = zqk[rows, GLA_DK:2 * GLA_DK]
        qd = (zqk[rows, 0:GLA_DK] * (GLA_HK ** -0.5) * jnp.exp(bc)).astype(BF16)
        kd = (kc * jnp.exp(-bc)).astype(BF16)
        ke = (kc * jnp.exp(bend - bc)).astype(BF16)
        vc = zv[rows].astype(BF16)
        dec = jnp.exp(bend)
        for hd in range(GLA_HEADS):
            ks = slice(hd * GLA_HK, (hd + 1) * GLA_HK)
            vs = slice(hd * GLA_HV, (hd + 1) * GLA_HV)
            att = jnp.where(causal, _dot_nt(qd[:, ks], kd[:, ks]), 0.0).astype(BF16)
            st = st_ref[hd]
            o = _dot(att, vc[:, vs]) + _dot_nt(qd[:, ks], st.astype(BF16))
            st_ref[hd] = st * dec[:, ks] + _dot_tn(vc[:, vs], ke[:, ks])
            y_ref[rows, vs] = (jax.nn.silu(zr[rows, vs]) * _rms(o, ghead_ref[:, vs])).astype(BF16)
    return h + _dot(y_ref[...], woutb_ref[...])


def _sgu_body(h, tick, gmix_ref, lng_ref, lnb_ref, bs_ref, winb_ref, woutb_ref, wsb_ref,
              us_ref, v_ref):
    C = SGU_CHUNK
    hn = _rms(h, gmix_ref[...]).astype(BF16)

    vsum = jnp.zeros((T_MIX, 1), F32)
    for k in range(SGU_WIDTH // SGU_VBLK):
        cols = slice(k * SGU_VBLK, (k + 1) * SGU_VBLK)
        vb = jax.nn.gelu(_dot(hn, winb_ref[:, SGU_WIDTH + k * SGU_VBLK:SGU_WIDTH + (k + 1) * SGU_VBLK]),
                         approximate=True)
        v_ref[:, cols] = vb
        vsum = vsum + jnp.sum(vb, axis=-1, keepdims=True)
        tick(k, SGU_WIDTH // SGU_VBLK)
    mean = vsum * (1.0 / SGU_WIDTH)
    vsq = jnp.zeros((T_MIX, 1), F32)
    for k in range(SGU_WIDTH // SGU_VBLK):
        cols = slice(k * SGU_VBLK, (k + 1) * SGU_VBLK)
        xc = v_ref[:, cols] - mean
        vsq = vsq + jnp.sum(xc * xc, axis=-1, keepdims=True)
    rstd = lax.rsqrt(vsq * (1.0 / SGU_WIDTH) + EPS)

    for g in range(SGU_GROUPS):
        cols = slice(g * SGU_GC, (g + 1) * SGU_GC)
        u = jax.nn.gelu(_dot(hn, winb_ref[:, cols]), approximate=True)
        vn = (((v_ref[:, cols] - mean) * rstd) * lng_ref[:, cols] + lnb_ref[:, cols]).astype(BF16)
        wc = wsb_ref[g]
        for c in range(T_MIX // C):
            rows = slice(c * C, (c + 1) * C)
            s = _dot(wc, vn[rows]) + bs_ref[:, cols]
            us_ref[rows, cols] = (u[rows] * s).astype(BF16)
    return h + _dot(us_ref[...], woutb_ref[...])


def _mixer_kernel(*refs, kind, fused, j, ns):
    refs = list(refs)
    t = pl.program_id(0)
    nt = pl.num_programs(0)
    if fused:
        (dcur_ref, dnext_ref, hprev_ref, p_ref, ys_ref, gple_ref, wup_ref,
         wgate_ref) = refs[:N_FINISH_INPUTS]
        wupb_ref, wgateb_ref, ybuf_ref, gsem = refs[-len(_FINISH_SCRATCH):]
        refs = refs[N_FINISH_INPUTS:-len(_FINISH_SCRATCH)]
        tick = _gather_start(t, dcur_ref, dnext_ref, ys_ref, ybuf_ref, gsem)
    else:
        h_ref, refs = refs[0], refs[1:]
        tick = lambda k, n: None
    cnt_ref, route_ref = refs[-2:]
    refs = refs[:-2]
    if kind == "gla":
        (gmix_ref, win_hbm, wgu_ref, bg_ref, ghead_ref, wout_ref,
         gffn_ref, wrt_ref, brt_ref, triu_ref, o_ref, dest_ref, counts_ref,
         winb_ref, woutb_ref, stage_ref, ssem, st_ref, y_ref) = refs
    else:
        (gmix_ref, win_hbm, lng_ref, lnb_ref, ws_ref, bs_ref, wout_ref,
         gffn_ref, wrt_ref, brt_ref, triu_ref, o_ref, dest_ref, counts_ref,
         winb_ref, woutb_ref, wsb_ref, stage_ref, ssem, us_ref, v_ref) = refs

    @pl.when(t == 0)
    def _():
        _stage_weight(winb_ref, win_hbm.at[j], stage_ref, ssem)
        _cast_weight(woutb_ref, wout_ref)
        if fused:
            _cast_weight(wupb_ref, wup_ref)
            _cast_weight(wgateb_ref, wgate_ref)
        if kind == "sgu":
            ri = lax.broadcasted_iota(jnp.int32, (SGU_CHUNK, SGU_CHUNK), 0)
            ci = lax.broadcasted_iota(jnp.int32, (SGU_CHUNK, SGU_CHUNK), 1)
            for g in range(SGU_GROUPS):
                wsb_ref[g] = jnp.where(ri >= ci, ws_ref[g], 0.0).astype(BF16)

    if kind == "gla":
        @pl.when(t % ns == 0)
        def _():
            st_ref[...] = jnp.zeros_like(st_ref)

    if fused:
        h2 = hprev_ref[...] + _gather_finish(t, ys_ref, ybuf_ref, gsem)
        h = _ple(h2, p_ref, gple_ref, wupb_ref, wgateb_ref)
    else:
        h = h_ref[...]

    if kind == "gla":
        out = _gla_body(h, tick, gmix_ref, wgu_ref, bg_ref, ghead_ref, winb_ref,
                        woutb_ref, st_ref, y_ref)
    else:
        out = _sgu_body(h, tick, gmix_ref, lng_ref, lnb_ref, bs_ref, winb_ref,
                        woutb_ref, wsb_ref, us_ref, v_ref)
    o_ref[...] = out
    _route_tile(out, t, gffn_ref, wrt_ref, brt_ref, triu_ref, cnt_ref, route_ref)

    @pl.when(t == nt - 1)
    def _():
        _emit_dest(nt, cnt_ref, route_ref, dest_ref, counts_ref)
    if fused:
        _gather_drain(t, nt, ys_ref, ybuf_ref, gsem)


def _mixer_layer(kind, i, ns, h, finish, params, router):
    n = h.shape[0]
    nt = n // T_MIX
    j = i // 2
    tile = lambda t: (t, 0)
    if kind == "gla":
        zw = GLA_ZW
        mixer_specs = [
            pl.BlockSpec((None, 1, D_MODEL), lambda t: (i, 0, 0)),
            pl.BlockSpec(memory_space=pl.ANY),
            pl.BlockSpec((None, GLA_A_PAD, GLA_DK), lambda t: (j, 0, 0)),
            pl.BlockSpec((None, 1, GLA_DK), lambda t: (j, 0, 0)),
            pl.BlockSpec((None, 1, GLA_DV), lambda t: (j, 0, 0)),
            _resident((None, GLA_DV, D_MODEL), lambda t: (j, 0, 0)),
        ]
        mixer_scratch = [
            pltpu.VMEM((D_MODEL, zw), BF16),
            pltpu.VMEM((GLA_DV, D_MODEL), BF16),
            pltpu.VMEM((2, STAGE_ROWS, zw), F32),
            pltpu.SemaphoreType.DMA((2,)),
            pltpu.VMEM((GLA_HEADS, GLA_HV, GLA_HK), F32),
            pltpu.VMEM((T_MIX, GLA_DV), BF16),
        ]
    else:
        zw = 2 * SGU_WIDTH
        mixer_specs = [
            pl.BlockSpec((None, 1, D_MODEL), lambda t: (i, 0, 0)),
            pl.BlockSpec(memory_space=pl.ANY),
            pl.BlockSpec((None, 1, SGU_WIDTH), lambda t: (j, 0, 0)),
            pl.BlockSpec((None, 1, SGU_WIDTH), lambda t: (j, 0, 0)),
            _resident((None, SGU_GROUPS, SGU_CHUNK, SGU_CHUNK), lambda t: (j, 0, 0, 0)),
            _resident((None, SGU_CHUNK, SGU_WIDTH), lambda t: (j, 0, 0)),
            _resident((None, SGU_WIDTH, D_MODEL), lambda t: (j, 0, 0)),
        ]
        mixer_scratch = [
            pltpu.VMEM((D_MODEL, zw), BF16),
            pltpu.VMEM((SGU_WIDTH, D_MODEL), BF16),
            pltpu.VMEM((SGU_GROUPS, SGU_CHUNK, SGU_CHUNK), BF16),
            pltpu.VMEM((2, STAGE_ROWS, zw), F32),
            pltpu.SemaphoreType.DMA((2,)),
            pltpu.VMEM((T_MIX, SGU_WIDTH), BF16),
            pltpu.VMEM((T_MIX, SGU_WIDTH), F32),
        ]
    if finish is None:
        head_specs = [pl.BlockSpec((T_MIX, D_MODEL), tile)]
        head_args = (h,)
        tail_scratch = []
    else:
        dest3, p2, ys, (gple, wup, wgate) = finish
        head_specs = _finish_specs(i - 1, nt)
        head_args = (dest3, dest3, h, p2, ys, gple, wup, wgate)
        tail_scratch = _FINISH_SCRATCH
    router_specs = [
        pl.BlockSpec((None, 1, D_MODEL), lambda t: (i, 0, 0)),
        pl.BlockSpec((None, ROUTE_ROWS, D_MODEL), lambda t: (i, 0, 0)),
        pl.BlockSpec((None, ROUTE_ROWS, T_MIX), lambda t: (i, 0, 0)),
        pl.BlockSpec((T_MIX, T_MIX), lambda t: (0, 0)),
    ]
    router_scratch = [
        pltpu.VMEM((ROUTE_ROWS, LANES), F32),
        pltpu.VMEM((nt, ROUTE_OUT_ROWS, T_MIX), jnp.int32),
    ]
    return pl.pallas_call(
        functools.partial(_mixer_kernel, kind=kind, fused=finish is not None, j=j, ns=ns),
        grid=(nt,),
        in_specs=head_specs + mixer_specs + router_specs,
        out_specs=[
            pl.BlockSpec((T_MIX, D_MODEL), tile),
            pl.BlockSpec((nt, 1, T_MIX), lambda t: (0, 0, 0)),
            pl.BlockSpec((ROUTE_ROWS, LANES), lambda t: (0, 0)),
        ],
        out_shape=[
            jax.ShapeDtypeStruct((n, D_MODEL), F32),
            jax.ShapeDtypeStruct((nt, 1, T_MIX), jnp.int32),
            jax.ShapeDtypeStruct((ROUTE_ROWS, LANES), jnp.int32),
        ],
        scratch_shapes=mixer_scratch + router_scratch + tail_scratch,
        compiler_params=_params(("arbitrary",)),
        name=kind + ("_fused" if finish is not None else "") + "_mixer",
    )(*head_args, *params, *router)


def _first_argmax(rows):
    best = rows[0]
    idx = jnp.zeros(best.shape, jnp.int32)
    for j in range(1, len(rows)):
        upd = rows[j] > best
        idx = jnp.where(upd, j, idx)
        best = jnp.where(upd, rows[j], best)
    return idx, best


def _route_tile(h, t, gffn_ref, wrt_ref, brt_ref, triu_ref, cnt_ref, route_ref):
    @pl.when(t == 0)
    def _():
        cnt_ref[...] = jnp.zeros_like(cnt_ref)

    hb = _rms(h, gffn_ref[...]).astype(BF16)
    lg = _dot_nt(wrt_ref[...], hb) + brt_ref[...]
    row = lambda j: lg[j:j + 1, :]
    g, _ = _first_argmax([row(j) for j in range(MOE_GROUPS)])
    ex = []
    for e in range(MOE_EPG):
        v = row(MOE_GROUPS + e)
        for gg in range(1, MOE_GROUPS):
            v = jnp.where(g == gg, row(MOE_GROUPS + gg * MOE_EPG + e), v)
        ex.append(v)
    i1, _ = _first_argmax(ex)
    neg = jnp.full(ex[0].shape, -jnp.inf, F32)
    i2, _ = _first_argmax([jnp.where(i1 == e, neg, ex[e]) for e in range(MOE_EPG)])
    lo = jnp.minimum(i1, i2)
    hi = jnp.maximum(i1, i2)
    pair = jnp.zeros(lo.shape, jnp.int32)
    for p, (a, b) in enumerate(PAIRS):
        pair = jnp.where((lo == a) & (hi == b), p, pair)
    bucket = g * N_PAIRS + pair

    m = bucket.shape[1]
    onehot = (lax.broadcasted_iota(jnp.int32, (ROUTE_ROWS, m), 0) == bucket).astype(F32)
    before = _dot(onehot.astype(BF16), triu_ref[...])
    rank = jnp.sum(onehot * (before + cnt_ref[:, 0:1]), axis=0, keepdims=True).astype(jnp.int32)
    cnt_ref[...] = cnt_ref[...] + jnp.sum(onehot, axis=1, keepdims=True)
    r8 = lax.broadcasted_iota(jnp.int32, (ROUTE_OUT_ROWS, m), 0)
    route_ref[t] = jnp.where(r8 == 0, bucket, jnp.where(r8 == 1, rank, 0))


def _emit_dest(nt, cnt_ref, route_ref, dest_ref, counts_ref):
    cnt = cnt_ref[...].astype(jnp.int32)
    counts_ref[...] = cnt
    units = _units(cnt).astype(F32).astype(BF16)
    ri = lax.broadcasted_iota(jnp.int32, (ROUTE_ROWS, ROUTE_ROWS), 0)
    ci = lax.broadcasted_iota(jnp.int32, (ROUTE_ROWS, ROUTE_ROWS), 1)
    before = _dot((ci < ri).astype(BF16), units)
    start = (before[:, 0:1] * MOE_UNIT).astype(jnp.int32)

    def body(k, carry):
        blk = route_ref[k]
        onehot = lax.broadcasted_iota(jnp.int32, (ROUTE_ROWS, T_MIX), 0) == blk[0:1, :]
        dest_ref[k] = jnp.sum(jnp.where(onehot, start, 0), axis=0, keepdims=True) + blk[1:2, :]
        return carry

    lax.fori_loop(0, nt, body, 0)


def _dispatch_kernel(cnt_ref, dest_ref, h_ref, xs_ref, zero_ref, rows_ref, starts_ref, zsem, sem):
    @pl.when(pl.program_id(0) == 0)
    def _():
        end = _bucket_layout(cnt_ref, starts_ref)
        zero_ref[...] = jnp.zeros_like(zero_ref)
        fills = []
        for b in range(N_BUCKETS):
            c = cnt_ref[b, 0]
            whole = lax.shift_right_logical(c, MOE_UNIT_LOG2) * MOE_UNIT
            fills.append(((c & (MOE_UNIT - 1)) != 0, starts_ref[b] + whole))
        for k in range(N_BUCKETS):
            fills.append((end + k * MOE_UNIT < xs_ref.shape[0] // ROW_SUB, end + k * MOE_UNIT))

        def fill(row):
            return pltpu.make_async_copy(zero_ref, _row_block(xs_ref, row, MOE_UNIT), zsem)

        for on, row in fills:
            pl.when(on)(lambda row=row: fill(row).start())
        for on, row in fills:
            pl.when(on)(lambda row=row: fill(row).wait())

    t = pl.program_id(0)
    nt = pl.num_programs(0)
    slot = t % 2

    def drain(s):
        pltpu.make_async_copy(rows_ref.at[s], _row_block(xs_ref, 0, T_ROW), sem.at[s]).wait()

    pl.when(t >= 2)(lambda: drain(slot))
    _store_row_major(rows_ref.at[slot], h_ref[...])

    def issue(k, carry):
        base = pl.multiple_of(k * DMA_UNROLL, DMA_UNROLL)
        for u in range(DMA_UNROLL):
            d = dest_ref[0, 0, base + u]
            pltpu.make_async_copy(_row_block(rows_ref.at[slot], base + u, 1),
                                  _row_block(xs_ref, d, 1), sem.at[slot]).start(priority=u % 2)
        return carry

    lax.fori_loop(0, T_ROW // DMA_UNROLL, issue, 0)

    @pl.when(t == nt - 1)
    def _():
        pl.when(nt >= 2)(lambda: drain(1 - slot))
        drain(slot)


def _dispatch(h, dest3, cnt):
    n = h.shape[0]
    return pl.pallas_call(
        _dispatch_kernel,
        grid=(n // T_ROW,),
        in_specs=[
            pl.BlockSpec(memory_space=pltpu.SMEM),
            pl.BlockSpec((1, 1, T_ROW), lambda t: (t, 0, 0), memory_space=pltpu.SMEM),
            pl.BlockSpec((T_ROW, D_MODEL), lambda t: (t, 0)),
        ],
        out_specs=pl.BlockSpec(memory_space=pl.ANY),
        out_shape=jax.ShapeDtypeStruct((_sorted_rows(n) * ROW_SUB, LANES), F32),
        scratch_shapes=[
            pltpu.VMEM((MOE_UNIT * ROW_SUB, LANES), F32),
            pltpu.VMEM((2, T_ROW * ROW_SUB, LANES), F32),
            pltpu.SMEM((N_BUCKETS,), jnp.int32),
            pltpu.SemaphoreType.DMA(()),
            pltpu.SemaphoreType.DMA((2,)),
        ],
        compiler_params=_params(("arbitrary",)),
        name="moe_dispatch",
    )(cnt, dest3, h)


def _experts_kernel(bg_ref, bea_ref, beb_ref,
                    cnt_ref, xs_ref, gffn_ref, wr_ref, br_ref,
                    wga_ref, wua_ref, wda_ref, wgb_ref, wub_ref, wdb_ref,
                    ys_ref,
                    wgab_ref, wuab_ref, wdab_ref, wgbb_ref, wubb_ref, wdbb_ref,
                    xbuf_ref, ybuf_ref, isem, osem,
                    bg0_ref, bnt_ref, trow_ref, tunits_ref, ntot_ref):
    b = pl.program_id(0)
    pair = b % N_PAIRS

    @pl.when(b == 0)
    def _():
        def per_bucket(q, carry):
            row, g = carry
            units = _units(cnt_ref[q, 0])
            n_full = lax.shift_right_logical(units, 1)
            n_tiles = n_full + (units & 1)
            bg0_ref[q] = g
            bnt_ref[q] = n_tiles

            def per_tile(j, c):
                trow_ref[g + j] = row + j * T_MOE
                tunits_ref[g + j] = jnp.where(j < n_full, T_MOE // MOE_UNIT, 1)
                return c

            lax.fori_loop(0, n_tiles, per_tile, 0)
            return row + units * MOE_UNIT, g + n_tiles

        _, n_total = lax.fori_loop(0, N_BUCKETS, per_bucket, (jnp.int32(0), jnp.int32(0)))
        ntot_ref[0] = n_total

    def cast_a():
        _cast_weight(wgab_ref, wga_ref)
        _cast_weight(wuab_ref, wua_ref)
        _cast_weight(wdab_ref, wda_ref)

    def cast_b():
        _cast_weight(wgbb_ref, wgb_ref)
        _cast_weight(wubb_ref, wub_ref)
        _cast_weight(wdbb_ref, wdb_ref)

    a_changes = [q for q in range(N_PAIRS) if q == 0 or PAIRS[q][0] != PAIRS[q - 1][0]]
    b_changes = [q for q in range(N_PAIRS) if q == 0 or PAIRS[q][1] != PAIRS[q - 1][1]]
    pl.when(functools.reduce(jnp.logical_or, [pair == q for q in a_changes]))(cast_a)
    pl.when(functools.reduce(jnp.logical_or, [pair == q for q in b_changes]))(cast_b)

    g0 = bg0_ref[b]
    ntot = ntot_ref[0]
    grp = bg_ref[b]
    ea = bea_ref[b]
    eb = beb_ref[b]

    def compute(x32):
        x = _rms(x32, gffn_ref[...]).astype(BF16)
        lg = _dot(x, wr_ref[...]) + br_ref[...]
        lane = lax.broadcasted_iota(jnp.int32, lg.shape, 1)
        glog = jnp.where(lane < MOE_GROUPS, lg, -jnp.inf)
        pe = jnp.exp(glog - jnp.max(glog, axis=-1, keepdims=True))
        g_w = (jnp.sum(jnp.where(lane == grp, pe, 0.0), axis=-1, keepdims=True)
               / jnp.sum(pe, axis=-1, keepdims=True))
        la = jnp.sum(jnp.where(lane == MOE_GROUPS + ea, lg, 0.0), axis=-1, keepdims=True)
        lb = jnp.sum(jnp.where(lane == MOE_GROUPS + eb, lg, 0.0), axis=-1, keepdims=True)
        m = jnp.maximum(la, lb)
        pa = jnp.exp(la - m)
        pb = jnp.exp(lb - m)
        ca = pa / (pa + pb) * g_w
        cb = pb / (pa + pb) * g_w

        def expert(wg_ref, wu_ref, c):
            return (jax.nn.silu(_dot(x, wg_ref[...])) * _dot(x, wu_ref[...]) * c).astype(BF16)

        return (_dot(expert(wgab_ref, wuab_ref, ca), wdab_ref[...])
                + _dot(expert(wgbb_ref, wubb_ref, cb), wdbb_ref[...]))

    def load(g, rows):
        slot = lax.rem(g, X_SLOTS)
        return pltpu.make_async_copy(_row_block(xs_ref, trow_ref[g], rows),
                                     _row_block(xbuf_ref.at[slot], 0, rows), isem.at[slot])

    def store(g, rows):
        return pltpu.make_async_copy(_row_block(ybuf_ref.at[g % 2], 0, rows),
                                     _row_block(ys_ref, trow_ref[g], rows), osem.at[g % 2])

    def by_size(g, fn):
        pl.when(tunits_ref[g] == T_MOE // MOE_UNIT)(lambda: fn(T_MOE))
        pl.when(tunits_ref[g] == 1)(lambda: fn(MOE_UNIT))

    @pl.when(b == 0)
    def _():
        for ahead in range(X_SLOTS - 1):
            @pl.when(ahead < ntot)
            def _():
                by_size(ahead, lambda rows: load(ahead, rows).start())

    def body(j, carry):
        g = g0 + j
        nxt = g + (X_SLOTS - 1)

        @pl.when(nxt < ntot)
        def _():
            by_size(nxt, lambda rows: load(nxt, rows).start())

        @pl.when(g >= 2)
        def _():
            by_size(g - 2, lambda rows: store(g - 2, rows).wait())

        def run(rows):
            load(g, rows).wait()
            y = compute(_load_row_major(xbuf_ref.at[lax.rem(g, X_SLOTS)], rows))
            _store_row_major(ybuf_ref.at[g % 2], y)
            store(g, rows).start()

        by_size(g, run)
        return carry

    lax.fori_loop(0, bnt_ref[b], body, 0)

    @pl.when(b == N_BUCKETS - 1)
    def _():
        for back in (2, 1):
            @pl.when(ntot >= back)
            def _():
                by_size(ntot - back, lambda rows: store(ntot - back, rows).wait())


def _experts(xs, i, cnt, gffn, wr, br, wg, wu, wd):
    e0 = i * MOE_EXPERTS
    max_tiles = xs.shape[0] // ROW_SUB // T_MOE + N_BUCKETS
    bg = jnp.asarray([q // N_PAIRS for q in range(N_BUCKETS)], jnp.int32)
    bea = jnp.asarray([(q // N_PAIRS) * MOE_EPG + PAIRS[q % N_PAIRS][0] for q in range(N_BUCKETS)],
                      jnp.int32)
    beb = jnp.asarray([(q // N_PAIRS) * MOE_EPG + PAIRS[q % N_PAIRS][1] for q in range(N_BUCKETS)],
                      jnp.int32)
    ea_map = lambda b, g, ea, eb: (e0 + ea[b], 0, 0)
    eb_map = lambda b, g, ea, eb: (e0 + eb[b], 0, 0)
    up_spec = lambda m: pl.BlockSpec((None, D_MODEL, MOE_FF), m)
    down_spec = lambda m: pl.BlockSpec((None, MOE_FF, D_MODEL), m)
    grid_spec = pltpu.PrefetchScalarGridSpec(
        num_scalar_prefetch=3,
        grid=(N_BUCKETS,),
        in_specs=[
            pl.BlockSpec(memory_space=pltpu.SMEM),
            pl.BlockSpec(memory_space=pl.ANY),
            pl.BlockSpec((None, 1, D_MODEL), lambda b, *_: (i, 0, 0)),
            pl.BlockSpec((None, D_MODEL, LANES), lambda b, *_: (i, 0, 0)),
            pl.BlockSpec((None, 1, LANES), lambda b, *_: (i, 0, 0)),
            up_spec(ea_map), up_spec(ea_map), down_spec(ea_map),
            up_spec(eb_map), up_spec(eb_map), down_spec(eb_map),
        ],
        out_specs=pl.BlockSpec(memory_space=pl.ANY),
        scratch_shapes=[
            pltpu.VMEM((D_MODEL, MOE_FF), BF16), pltpu.VMEM((D_MODEL, MOE_FF), BF16),
            pltpu.VMEM((MOE_FF, D_MODEL), BF16),
            pltpu.VMEM((D_MODEL, MOE_FF), BF16), pltpu.VMEM((D_MODEL, MOE_FF), BF16),
            pltpu.VMEM((MOE_FF, D_MODEL), BF16),
            pltpu.VMEM((X_SLOTS, T_MOE * ROW_SUB, LANES), F32),
            pltpu.VMEM((2, T_MOE * ROW_SUB, LANES), F32),
            pltpu.SemaphoreType.DMA((X_SLOTS,)),
            pltpu.SemaphoreType.DMA((2,)),
            pltpu.SMEM((N_BUCKETS,), jnp.int32), pltpu.SMEM((N_BUCKETS,), jnp.int32),
            pltpu.SMEM((max_tiles,), jnp.int32), pltpu.SMEM((max_tiles,), jnp.int32),
            pltpu.SMEM((1,), jnp.int32),
        ],
    )
    return pl.pallas_call(
        _experts_kernel,
        grid_spec=grid_spec,
        out_shape=jax.ShapeDtypeStruct(xs.shape, F32),
        input_output_aliases={4: 0},
        compiler_params=_params(("arbitrary",)),
        name="moe_experts",
    )(bg, bea, beb, cnt, xs, gffn, wr, br, wg, wu, wd, wg, wu, wd)


def _final_kernel(dcur_ref, dnext_ref, h_ref, p_ref, ys_ref, gple_ref, wup_ref, wgate_ref,
                  gfin_ref, o_ref, wupb_ref, wgateb_ref, ybuf_ref, sem):
    t = pl.program_id(0)
    _gather_start(t, dcur_ref, dnext_ref, ys_ref, ybuf_ref, sem)(0, 1)

    @pl.when(t == 0)
    def _():
        _cast_weight(wupb_ref, wup_ref)
        _cast_weight(wgateb_ref, wgate_ref)

    h2 = h_ref[...] + _gather_finish(t, ys_ref, ybuf_ref, sem)
    o_ref[...] = _rms(_ple(h2, p_ref, gple_ref, wupb_ref, wgateb_ref), gfin_ref[...])
    _gather_drain(t, pl.num_programs(0), ys_ref, ybuf_ref, sem)


def _final_layer(h, dest3, p2, ys, i, gple, wup, wgate, gfin):
    n = h.shape[0]
    nt = n // T_ROW
    return pl.pallas_call(
        _final_kernel,
        grid=(nt,),
        in_specs=_finish_specs(i, nt) + [pl.BlockSpec((1, D_MODEL), lambda t: (0, 0))],
        out_specs=pl.BlockSpec((T_ROW, D_MODEL), lambda t: (t, 0)),
        out_shape=jax.ShapeDtypeStruct((n, D_MODEL), F32),
        scratch_shapes=_FINISH_SCRATCH,
        compiler_params=_params(("arbitrary",)),
        name="final_combine_ple",
    )(dest3, dest3, h, p2, ys, gple, wup, wgate, gfin)


def _sorted_rows(n):
    return n + N_BUCKETS * MOE_UNIT


def kernel(x, p, gla_w_in, gla_w_gate_up, gla_b_gate, gla_g_head, gla_w_out, sgu_w_in, sgu_ln_g,
           sgu_ln_b, sgu_w_s, sgu_b_s, sgu_w_out, norm_mix_g, norm_ffn_g, norm_ple_g,
           moe_w_route_group, moe_b_route_group, moe_w_route_expert, moe_b_route_expert,
           moe_w_gate, moe_w_up, moe_w_down, ple_w_up, ple_w_gate, final_norm_g):
    batch, seq, d = x.shape
    assert d == D_MODEL and seq % T_MIX == 0
    n = batch * seq
    depth = p.shape[0]
    assert depth == DEPTH
    na, nb = gla_w_in.shape[0], sgu_w_in.shape[0]

    gla_win = jnp.pad(gla_w_in, ((0, 0), (0, 0), (0, GLA_A_PAD - GLA_RANK)))
    gla_wgu = jnp.pad(gla_w_gate_up, ((0, 0), (0, GLA_A_PAD - GLA_RANK), (0, 0)))
    gla_bg = gla_b_gate.reshape(na, 1, GLA_DK)
    gla_gh = gla_g_head.reshape(na, 1, GLA_DV)
    sgu_lng = sgu_ln_g.reshape(nb, 1, SGU_WIDTH)
    sgu_lnb = sgu_ln_b.reshape(nb, 1, SGU_WIDTH)
    sgu_bs = jnp.repeat(jnp.transpose(sgu_b_s, (0, 2, 1)), SGU_GC, axis=-1)
    g_mix = norm_mix_g.reshape(depth, 1, D_MODEL)
    g_ffn = norm_ffn_g.reshape(depth, 1, D_MODEL)
    g_ple = norm_ple_g.reshape(depth, 1, D_MODEL)
    g_fin = final_norm_g.reshape(1, D_MODEL)
    w_route = jnp.concatenate(
        [moe_w_route_group, moe_w_route_expert.reshape(depth, D_MODEL, MOE_EXPERTS)], axis=-1)
    b_route = jnp.concatenate(
        [moe_b_route_group, moe_b_route_expert.reshape(depth, MOE_EXPERTS)], axis=-1)
    n_logits = MOE_GROUPS + MOE_EXPERTS
    wr = jnp.pad(w_route, ((0, 0), (0, 0), (0, LANES - n_logits))).astype(BF16)
    br = jnp.pad(b_route, ((0, 0), (0, LANES - n_logits))).reshape(depth, 1, LANES)
    wrt = jnp.pad(jnp.transpose(w_route, (0, 2, 1)),
                  ((0, 0), (0, ROUTE_ROWS - n_logits), (0, 0))).astype(BF16)
    brt = jnp.broadcast_to(
        jnp.pad(b_route, ((0, 0), (0, ROUTE_ROWS - n_logits)))[:, :, None],
        (depth, ROUTE_ROWS, T_MIX))
    triu = jnp.triu(jnp.ones((T_MIX, T_MIX), BF16), k=1)
    wg = moe_w_gate.reshape(depth * MOE_EXPERTS, D_MODEL, MOE_FF)
    wu = moe_w_up.reshape(depth * MOE_EXPERTS, D_MODEL, MOE_FF)
    wd = moe_w_down.reshape(depth * MOE_EXPERTS, MOE_FF, D_MODEL)
    p2 = p.reshape(depth * n, PLE_DIM)

    ns = seq // T_MIX
    h = x.reshape(n, D_MODEL)
    finish = None
    for i in range(depth):
        if i % 2 == 0:
            kind, params = "gla", (g_mix, gla_win, gla_wgu, gla_bg, gla_gh, gla_w_out)
        else:
            kind, params = "sgu", (g_mix, sgu_w_in, sgu_lng, sgu_lnb, sgu_w_s, sgu_bs, sgu_w_out)
        h, dest3, cnt = _mixer_layer(kind, i, ns, h, finish, params, (g_ffn, wrt, brt, triu))
        xs = _dispatch(h, dest3, cnt)
        ys = _experts(xs, i, cnt, g_ffn, wr, br, wg, wu, wd)
        finish = (dest3, p2, ys, (g_ple, ple_w_up, ple_w_gate))
    h = _final_layer(h, dest3, p2, ys, depth - 1, g_ple, ple_w_up, ple_w_gate, g_fin)
    return h.reshape(batch, seq, D_MODEL)
```
